```python
import math
import jax, jax.numpy as jnp
from jax import lax
import numpy as np

D_MODEL = 1024
BATCH = 4
SEQ = 8192
DEPTH = 2

CHUNK = 64
Q_BLOCK = 128
LN_EPS = 1e-5
SB_HEADS = 8
SB_HEAD_DIM = 64
SB_WIDTH = SB_HEADS * SB_HEAD_DIM
RET_HEADS = 4
RET_HEAD_DIM = 128
RET_WIDTH = RET_HEADS * RET_HEAD_DIM
EVEN_IN = 3 * SB_WIDTH + 4 * RET_WIDTH
EVEN_MIX = SB_WIDTH + RET_WIDTH
DIFF_HEADS = 8
DIFF_HEAD_DIM = 64
DIFF_V_DIM = 2 * DIFF_HEAD_DIM
DIFF_QK = DIFF_HEADS * 2 * DIFF_HEAD_DIM
DIFF_VW = DIFF_HEADS * DIFF_V_DIM
ODD_IN = 2 * DIFF_QK + DIFF_VW
N_GROUPS = 4
EXPERTS_PER_GROUP = 8
N_EXPERTS = N_GROUPS * EXPERTS_PER_GROUP
TOP_K = 2
EXPERT_FF = 512
MOE_BLOCK = 128
N_EVEN = (DEPTH + 1) // 2
N_ODD = DEPTH // 2

kernel_name = 'hybrid_sb_retention_diffattn_hmoe'


def layer_norm(x, g, b):
    xf = x.astype(jnp.float32)
    mu = jnp.mean(xf, axis=-1, keepdims=True)
    var = jnp.mean(jnp.square(xf - mu), axis=-1, keepdims=True)
    y = (xf - mu) * lax.rsqrt(var + LN_EPS) * g.astype(jnp.float32) + b.astype(jnp.float32)
    return y.astype(x.dtype)


def split_cols(a, sizes):
    out = []
    o = 0
    for s in sizes:
        out.append(a[..., o:o + s])
        o += s
    return out


def stick_breaking_attention(q, k, v):
    B, S, H, dh = q.shape
    nb = S // Q_BLOCK
    kh = k.transpose(0, 2, 1, 3)
    vh = v.transpose(0, 2, 1, 3)
    q_blocks = jnp.moveaxis(q.transpose(0, 2, 1, 3).reshape(B, H, nb, Q_BLOCK, dh), 2, 0)
    key_pos = jnp.arange(S)
    scale = dh ** -0.5

    def block(args):
        qb, start = args
        z = jnp.einsum('bhqd,bhkd->bhqk', qb, kh, preferred_element_type=jnp.float32) * scale
        q_pos = start + jnp.arange(Q_BLOCK)
        mask = key_pos[None, :] < q_pos[:, None]
        log_1m_beta = jnp.where(mask, jax.nn.log_sigmoid(-z), 0.0)
        between = lax.cumsum(log_1m_beta, axis=3, reverse=True) - log_1m_beta
        w = jnp.where(mask, jnp.exp(jax.nn.log_sigmoid(z) + between), 0.0)
        return jnp.einsum('bhqk,bhkd->bhqd', w.astype(vh.dtype), vh)

    out = lax.map(block, (q_blocks, jnp.arange(nb) * Q_BLOCK))
    return jnp.moveaxis(out, 0, 2).reshape(B, H, S, dh).transpose(0, 2, 1, 3).reshape(B, S, H * dh)


def retention(q, k, v, g, gn_gain):
    B, S, H, d = q.shape
    nc = S // CHUNK
    f32 = jnp.float32

    def chunked(t):
        return t.astype(f32).transpose(0, 2, 1, 3).reshape(B, H, nc, CHUNK, d)

    qc = chunked(q)
    kc = chunked(k) * (d ** -0.5)
    vc = chunked(v)
    log_gamma = jnp.log1p(-jnp.exp2(-5.0 - jnp.arange(H, dtype=f32)))
    pos = jnp.arange(CHUNK, dtype=f32)
    intra_decay = jnp.exp(log_gamma[:, None, None] * jnp.abs(pos[:, None] - pos[None, :]))
    scores = jnp.einsum('bhncd,bhnsd->bhncs', qc, kc) * intra_decay[None, :, None]
    intra = jnp.einsum('bhncs,bhnse->bhnce', scores, vc)
    k_decay = jnp.exp(log_gamma[:, None] * (CHUNK - 1 - pos)[None, :])
    q_decay = jnp.exp(log_gamma[:, None] * (pos + 1.0)[None, :])
    chunk_kv = jnp.einsum('bhnsd,bhnse->nbhde', kc * k_decay[None, :, None, :, None], vc)
    chunk_decay = jnp.exp(log_gamma * CHUNK)[None, :, None, None]

    def step(state, kv):
        return chunk_decay * state + kv, state

    _, prev_state = lax.scan(step, jnp.zeros((B, H, d, d), f32), chunk_kv)
    inter = jnp.einsum('bhncd,nbhde->bhnce', qc * q_decay[None, :, None, :, None], prev_state)
    o = (intra + inter).reshape(B, H, S, d)
    mu = jnp.mean(o, axis=-1, keepdims=True)
    var = jnp.mean(jnp.square(o - mu), axis=-1, keepdims=True)
    o = ((o - mu) * lax.rsqrt(var + LN_EPS)).transpose(0, 2, 1, 3).reshape(B, S, H * d)
    o = o * gn_gain.astype(f32) * jax.nn.silu(g.astype(f32).reshape(B, S, H * d))
    return o.astype(q.dtype)


def differential_attention(q, k, v, lam, lambda_init, subln_gain):
    B, S, H, _, dh = q.shape
    dv = v.shape[-1]
    nb = S // Q_BLOCK
    f32 = jnp.float32
    kh = k.transpose(0, 2, 3, 1, 4)
    vh = v.transpose(0, 2, 1, 3)
    q_blocks = jnp.moveaxis(q.transpose(0, 2, 3, 1, 4).reshape(B, H, 2, nb, Q_BLOCK, dh), 3, 0)
    slopes = jnp.exp2(-8.0 / H * (jnp.arange(H, dtype=f32) + 1.0))
    key_pos = jnp.arange(S)
    key_chunk = key_pos // CHUNK
    scale = dh ** -0.5

    def block(args):
        qb, start = args
        z = jnp.einsum('bhmqd,bhmkd->bhmqk', qb, kh, preferred_element_type=f32) * scale
        q_pos = start + jnp.arange(Q_BLOCK)
        dist = jnp.abs(q_pos[:, None] - key_pos[None, :]).astype(f32)
        z = z - slopes[:, None, None, None] * dist
        mask = key_chunk[None, :] <= (q_pos // CHUNK)[:, None]
        p = jax.nn.softmax(jnp.where(mask, z, -jnp.inf), axis=-1)
        a = p[:, :, 0] - lam * p[:, :, 1]
        return jnp.einsum('bhqk,bhkd->bhqd', a.astype(vh.dtype), vh)

    out = lax.map(block, (q_blocks, jnp.arange(nb) * Q_BLOCK))
    o = jnp.moveaxis(out, 0, 2).reshape(B, H, S, dv).astype(f32)
    o = o * lax.rsqrt(jnp.mean(jnp.square(o), axis=-1, keepdims=True) + LN_EPS)
    o = o * subln_gain.astype(f32) * (1.0 - lambda_init)
    return o.transpose(0, 2, 1, 3).reshape(B, S, H * dv).astype(v.dtype)


def hierarchical_moe(h, w_group, b_group, w_router, b_router, w1, w3, w2):
    B, S, D = h.shape
    T = B * S
    f32 = jnp.float32
    xt = h.reshape(T, D)
    group_logits = (xt @ w_group).astype(f32) + b_group.astype(f32)
    grp = jnp.argmax(group_logits, axis=-1)
    p_grp = jnp.take_along_axis(jax.nn.softmax(group_logits, axis=-1), grp[:, None], axis=1)[:, 0]
    expert_logits = ((xt @ w_router).astype(f32) + b_router.astype(f32)).reshape(T, N_GROUPS, EXPERTS_PER_GROUP)
    in_group = jnp.take_along_axis(expert_logits, grp[:, None, None], axis=1)[:, 0]
    top_vals, top_idx = lax.top_k(in_group, TOP_K)
    gates = jax.nn.softmax(top_vals, axis=-1) * p_grp[:, None]
    expert_id = (grp[:, None] * EXPERTS_PER_GROUP + top_idx).astype(jnp.int32)
    M = T * TOP_K
    flat_e = expert_id.reshape(M)
    flat_tok = jnp.arange(M, dtype=jnp.int32) // TOP_K
    order = jnp.argsort(flat_e)
    e_sorted = flat_e[order]
    tok_sorted = flat_tok[order]
    gate_sorted = gates.reshape(M)[order]
    counts = jnp.zeros((N_EXPERTS,), jnp.int32).at[flat_e].add(1)
    padded = (counts + MOE_BLOCK - 1) // MOE_BLOCK * MOE_BLOCK
    pad_end = jnp.cumsum(padded)
    pad_start = pad_end - padded
    start = jnp.cumsum(counts) - counts
    dest = pad_start[e_sorted] + jnp.arange(M, dtype=jnp.int32) - start[e_sorted]
    P = M + N_EXPERTS * MOE_BLOCK
    n_blk = P // MOE_BLOCK
    slot_tok = jnp.full((P,), T, jnp.int32).at[dest].set(tok_sorted)
    x_pad = jnp.concatenate([xt, jnp.zeros((1, D), xt.dtype)], axis=0)
    x_slots = x_pad[slot_tok].reshape(n_blk, MOE_BLOCK, D)
    blk_expert = jnp.minimum(
        jnp.searchsorted(pad_end, jnp.arange(n_blk, dtype=jnp.int32) * MOE_BLOCK, side='right'),
        N_EXPERTS - 1)

    def expert_block(args):
        xb, e = args
        return (jax.nn.silu(xb @ w1[e]) * (xb @ w3[e])) @ w2[e]

    y_slots = lax.map(expert_block, (x_slots, blk_expert)).reshape(P, D)
    y = y_slots[dest] * gate_sorted[:, None].astype(y_slots.dtype)
    return jax.ops.segment_sum(y, tok_sorted, num_segments=T).reshape(B, S, D)


def setup_inputs(seed: int = 0) -> dict:
    key = jax.random.key(seed)
    ks = jax.random.split(key, 26)
    D = D_MODEL
    beta = (8.0 * DEPTH) ** -0.25
    nrm = lambda k, shape, s: jax.random.normal(k, shape, jnp.float32) * s
    return {
        'x': nrm(ks[0], (BATCH, SEQ, D), 1.0),
        'c': nrm(ks[1], (BATCH, D), 1.0),
        'ln1_g': 1.0 + nrm(ks[2], (DEPTH, D), 0.02),
        'ln1_b': nrm(ks[3], (DEPTH, D), 0.02),
        'ln2_g': 1.0 + nrm(ks[4], (DEPTH, D), 0.02),
        'ln2_b': nrm(ks[5], (DEPTH, D), 0.02),
        'w_ada': nrm(ks[6], (DEPTH, D, 6 * D), 0.1 * D ** -0.5),
        'b_ada': nrm(ks[7], (DEPTH, 6 * D), 0.01),
        'even_w_in': nrm(ks[8], (N_EVEN, D, EVEN_IN), D ** -0.5),
        'even_w_out': nrm(ks[9], (N_EVEN, EVEN_MIX, D), beta * EVEN_MIX ** -0.5),
        'ret_gn_g': 1.0 + nrm(ks[10], (N_EVEN, RET_WIDTH), 0.02),
        'odd_w_in': nrm(ks[11], (N_ODD, D, ODD_IN), D ** -0.5),
        'odd_w_out': nrm(ks[12], (N_ODD, DIFF_VW, D), beta * DIFF_VW ** -0.5),
        'lambda_q1': nrm(ks[13], (N_ODD, DIFF_HEAD_DIM), 0.1),
        'lambda_k1': nrm(ks[14], (N_ODD, DIFF_HEAD_DIM), 0.1),
        'lambda_q2': nrm(ks[15], (N_ODD, DIFF_HEAD_DIM), 0.1),
        'lambda_k2': nrm(ks[16], (N_ODD, DIFF_HEAD_DIM), 0.1),
        'diff_subln_g': 1.0 + nrm(ks[17], (N_ODD, DIFF_V_DIM), 0.02),
        'moe_w_group': nrm(ks[18], (DEPTH, D, N_GROUPS), D ** -0.5),
        'moe_b_group': nrm(ks[19], (DEPTH, N_GROUPS), 0.01),
        'moe_w_router': nrm(ks[20], (DEPTH, D, N_EXPERTS), D ** -0.5),
        'moe_b_router': nrm(ks[21], (DEPTH, N_EXPERTS), 0.01),
        'moe_w1': nrm(ks[22], (DEPTH, N_EXPERTS, D, EXPERT_FF), D ** -0.5),
        'moe_w3': nrm(ks[23], (DEPTH, N_EXPERTS, D, EXPERT_FF), D ** -0.5),
        'moe_w2': nrm(ks[24], (DEPTH, N_EXPERTS, EXPERT_FF, D), beta * EXPERT_FF ** -0.5),
    }


def reference(x, c, ln1_g, ln1_b, ln2_g, ln2_b, w_ada, b_ada,
              even_w_in, even_w_out, ret_gn_g,
              odd_w_in, odd_w_out, lambda_q1, lambda_k1, lambda_q2, lambda_k2, diff_subln_g,
              moe_w_group, moe_b_group, moe_w_router, moe_b_router, moe_w1, moe_w3, moe_w2):
    B, S, D = x.shape
    alpha = (2.0 * DEPTH) ** 0.25
    cond = jax.nn.silu(c)
    for l in range(DEPTH):
        mod = (cond @ w_ada[l] + b_ada[l])[:, None, :]
        sh1, sc1, g1, sh2, sc2, g2 = jnp.split(mod, 6, axis=-1)
        u = x * (1.0 + sc1) + sh1
        i = l // 2
        if l % 2 == 0:
            proj = u @ even_w_in[i]
            sq, sk, sv, rq, rk, rv, rg = split_cols(proj, [SB_WIDTH] * 3 + [RET_WIDTH] * 4)
            sb_shape = (B, S, SB_HEADS, SB_HEAD_DIM)
            ret_shape = (B, S, RET_HEADS, RET_HEAD_DIM)
            a_out = stick_breaking_attention(sq.reshape(sb_shape), sk.reshape(sb_shape), sv.reshape(sb_shape))
            b_out = retention(rq.reshape(ret_shape), rk.reshape(ret_shape), rv.reshape(ret_shape),
                              rg.reshape(ret_shape), ret_gn_g[i])
            mix = jnp.concatenate([a_out, b_out], axis=-1) @ even_w_out[i]
        else:
            proj = u @ odd_w_in[i]
            dq, dk, dv = split_cols(proj, [DIFF_QK, DIFF_QK, DIFF_VW])
            lambda_init = 0.8 - 0.6 * math.exp(-0.3 * l)
            lam = (jnp.exp(jnp.sum(lambda_q1[i].astype(jnp.float32) * lambda_k1[i].astype(jnp.float32)))
                   - jnp.exp(jnp.sum(lambda_q2[i].astype(jnp.float32) * lambda_k2[i].astype(jnp.float32)))
                   + lambda_init)
            c_out = differential_attention(dq.reshape(B, S, DIFF_HEADS, 2, DIFF_HEAD_DIM),
                                           dk.reshape(B, S, DIFF_HEADS, 2, DIFF_HEAD_DIM),
                                           dv.reshape(B, S, DIFF_HEADS, DIFF_V_DIM),
                                           lam, lambda_init, diff_subln_g[i])
            mix = c_out @ odd_w_out[i]
        x = layer_norm(alpha * x + (1.0 + g1) * mix, ln1_g[l], ln1_b[l])
        u = x * (1.0 + sc2) + sh2
        f = hierarchical_moe(u, moe_w_group[l], moe_b_group[l], moe_w_router[l], moe_b_router[l],
                             moe_w1[l], moe_w3[l], moe_w2[l])
        x = layer_norm(alpha * x + (1.0 + g2) * f, ln2_g[l], ln2_b[l])
    return x
```

```python
import functools
import math

import jax
import jax.numpy as jnp
from jax import lax
from jax.experimental import pallas as pl
from jax.experimental.pallas import tpu as pltpu

F32 = jnp.float32
I32 = jnp.int32
MXU_DTYPE = jnp.bfloat16
HIGHEST = lax.Precision.HIGHEST

LN_EPS = 1e-5
CHUNK = 64
LANES = 128
SB_HEADS, SB_HEAD_DIM = 8, 64
RET_HEADS, RET_HEAD_DIM = 4, 128
DIFF_HEADS, DIFF_HEAD_DIM = 8, 64
SB_WIDTH = SB_HEADS * SB_HEAD_DIM
RET_WIDTH = RET_HEADS * RET_HEAD_DIM
DIFF_QK = DIFF_HEADS * 2 * DIFF_HEAD_DIM
N_GROUPS, EXPERTS_PER_GROUP = 4, 8
N_EXPERTS = N_GROUPS * EXPERTS_PER_GROUP
TOP_K = 2
ROUTER_ROWS = 40

VMEM_LIMIT = 56 * 1024 * 1024


def _params(*sem):
    return pltpu.CompilerParams(dimension_semantics=sem, vmem_limit_bytes=VMEM_LIMIT)


def _dot(a, b):
    return jnp.dot(a, b, preferred_element_type=F32)


def _dot_nt(a, b):
    return lax.dot_general(a, b, (((1,), (1,)), ((), ())), preferred_element_type=F32)


def _dot_tn(a, b):
    return lax.dot_general(a, b, (((0,), (0,)), ((), ())), preferred_element_type=F32)


def _silu(x):
    return x * (1.0 / (1.0 + jnp.exp(-x)))


def _ada_kernel(c_ref, w_ref, b_ref, o_ref):
    o_ref[0] = jnp.dot(_silu(c_ref[...]), w_ref[0], preferred_element_type=F32, precision=HIGHEST) + b_ref[0]


def _ada(c, w_ada, b_ada):
    depth, d, n = w_ada.shape
    bp = 8
    cp = jnp.zeros((bp, d), F32).at[: c.shape[0]].set(c)
    tn = 1536
    out = pl.pallas_call(
        _ada_kernel,
        grid=(depth, n // tn),
        in_specs=[
            pl.BlockSpec((bp, d), lambda l, j: (0, 0)),
            pl.BlockSpec((1, d, tn), lambda l, j: (l, 0, j)),
            pl.BlockSpec((1, 1, tn), lambda l, j: (l, 0, j)),
        ],
        out_specs=pl.BlockSpec((1, bp, tn), lambda l, j: (l, 0, j)),
        out_shape=jax.ShapeDtypeStruct((depth, bp, n), F32),
        compiler_params=_params("parallel", "parallel"),
        name="ada_mod",
    )(cp, w_ada, b_ada.reshape(depth, 1, n))
    return out[:, : c.shape[0]]


def _inproj_kernel(x_ref, sc_ref, sh_ref, w_ref, o_ref, *, tn):
    u = (x_ref[0] * sc_ref[0] + sh_ref[0]).astype(MXU_DTYPE)
    for j in range(o_ref.shape[2] // tn):
        o_ref[0, :, j * tn:(j + 1) * tn] = _dot(u, w_ref[:, j * tn:(j + 1) * tn]).astype(o_ref.dtype)


def _inproj(x, scale1p, shift, w):
    b, s, d = x.shape
    n = w.shape[1]
    tm = min(512, s)
    return pl.pallas_call(
        functools.partial(_inproj_kernel, tn=512),
        grid=(b, s // tm),
        in_specs=[
            pl.BlockSpec((1, tm, d), lambda i, j: (i, j, 0)),
            pl.BlockSpec((1, 1, d), lambda i, j: (i, 0, 0)),
            pl.BlockSpec((1, 1, d), lambda i, j: (i, 0, 0)),
            pl.BlockSpec((d, n), lambda i, j: (0, 0)),
        ],
        out_specs=pl.BlockSpec((1, tm, n), lambda i, j: (i, j, 0)),
        out_shape=jax.ShapeDtypeStruct((b, s, n), MXU_DTYPE),
        compiler_params=_params("parallel", "parallel"),
        name="in_proj",
    )(x, scale1p, shift, w.astype(MXU_DTYPE))


def _sb_kernel(q_ref, k_ref, v_ref, o_ref, acc_ref, carry_ref, *, tq):
    qi = pl.program_id(2)
    lane = lax.broadcasted_iota(I32, (1, LANES), 1)
    row = lax.broadcasted_iota(I32, (tq, tq), 0)
    col = lax.broadcasted_iota(I32, (tq, tq), 1)
    causal = col < row
    neg_later = jnp.where(row > col, -1.0, 0.0).astype(MXU_DTYPE)
    neg_ones = jnp.full((tq, LANES), -1.0, MXU_DTYPE)
    q2 = q_ref[0]
    scale = SB_HEAD_DIM ** -0.5

    def key_block(qh, j, diagonal):
        start = pl.multiple_of(j * tq, tq)
        kj = k_ref[0, pl.ds(start, tq), :]
        vj = v_ref[0, pl.ds(start, tq), :]
        z = _dot_nt(qh, kj)
        sp = jnp.maximum(z, 0.0) + jnp.log(1.0 + jnp.exp(-jnp.abs(z)))
        if diagonal:
            sp = jnp.where(causal, sp, 0.0)
        spb = sp.astype(MXU_DTYPE)
        carry = carry_ref[...]
        between = _dot(spb, neg_later) + jnp.concatenate([carry] * (tq // LANES), axis=1)
        w = jnp.exp(z - sp + between)
        if diagonal:
            w = jnp.where(causal, w, 0.0)
        acc_ref[...] += _dot(w.astype(MXU_DTYPE), vj)
        carry_ref[...] = carry + _dot(spb, neg_ones)

    heads = []
    for h in range(2):
        in_head = (lane // SB_HEAD_DIM) == h
        qh = (jnp.where(in_head, q2, jnp.zeros_like(q2)).astype(F32) * scale).astype(MXU_DTYPE)
        acc_ref[...] = jnp.zeros_like(acc_ref)
        carry_ref[...] = jnp.zeros_like(carry_ref)
        key_block(qh, qi, True)

        def body(it, _, qh=qh):
            key_block(qh, qi - 1 - it, False)
            return 0

        lax.fori_loop(0, qi, body, 0)
        heads.append(acc_ref[...])
    o_ref[0] = jnp.where(lane < SB_HEAD_DIM, heads[0], heads[1]).astype(o_ref.dtype)


def _sb_attention(proj):
    b, s, _ = proj.shape
    tq = min(256, s)
    n_pairs = SB_WIDTH // LANES
    return pl.pallas_call(
        functools.partial(_sb_kernel, tq=tq),
        grid=(b, n_pairs, s // tq),
        in_specs=[
            pl.BlockSpec((1, tq, LANES), lambda i, p, j: (i, j, p)),
            pl.BlockSpec((1, s, LANES), lambda i, p, j: (i, 0, n_pairs + p)),
            pl.BlockSpec((1, s, LANES), lambda i, p, j: (i, 0, 2 * n_pairs + p)),
        ],
        out_specs=pl.BlockSpec((1, tq, LANES), lambda i, p, j: (i, j, p)),
        out_shape=jax.ShapeDtypeStruct((b, s, SB_WIDTH), MXU_DTYPE),
        scratch_shapes=[pltpu.VMEM((tq, LANES), F32), pltpu.VMEM((tq, LANES), F32)],
        compiler_params=_params("parallel", "parallel", "parallel"),
        name="sb_attention",
    )(proj, proj, proj)


def _ret_kernel(lg_ref, q_ref, k_ref, v_ref, g_ref, gn_ref, o_ref, state_ref, *, tb):
    h = pl.program_id(1)
    blk = pl.program_id(2)

    @pl.when(blk == 0)
    def _():
        state_ref[...] = jnp.zeros_like(state_ref)

    lg = lg_ref[h]
    scale = RET_HEAD_DIM ** -0.5
    q = q_ref[0].astype(F32)
    k = k_ref[0].astype(F32)
    v = v_ref[0]
    row = lax.broadcasted_iota(I32, (tb, tb), 0)
    col = lax.broadcasted_iota(I32, (tb, tb), 1)
    dist = jnp.abs(row - col).astype(F32)
    decay = jnp.where(col // CHUNK <= row // CHUNK, jnp.exp(lg * dist) * scale, 0.0)
    pos = lax.broadcasted_iota(I32, (tb, 1), 0).astype(F32)
    scores = _dot_nt(q.astype(MXU_DTYPE), k.astype(MXU_DTYPE)) * decay
    intra = _dot(scores.astype(MXU_DTYPE), v)
    state = state_ref[...]
    q_in = (q * jnp.exp(lg * (pos + 1.0))).astype(MXU_DTYPE)
    inter = _dot(q_in, state.astype(MXU_DTYPE))
    k_out = (k * (jnp.exp(lg * (tb - 1.0 - pos)) * scale)).astype(MXU_DTYPE)
    block_decay = jnp.exp(lg * jnp.full((1, RET_HEAD_DIM), float(tb), F32))
    state_ref[...] = block_decay * state + _dot_tn(k_out, v)
    o = intra + inter
    mu = jnp.mean(o, axis=-1, keepdims=True)
    oc = o - mu
    var = jnp.mean(oc * oc, axis=-1, keepdims=True)
    o = oc * lax.rsqrt(var + LN_EPS) * gn_ref[0] * _silu(g_ref[0].astype(F32))
    o_ref[0] = o.astype(o_ref.dtype)


def _retention(proj, gn_gain):
    b, s, _ = proj.shape
    tb = min(256, s)
    c0 = 3 * SB_WIDTH // LANES
    log_gamma = jnp.log1p(-jnp.exp2(-5.0 - jnp.arange(RET_HEADS, dtype=F32)))

    def col(which):
        return lambda i, h, j, lg: (i, j, c0 + which * RET_HEADS + h)

    grid_spec = pltpu.PrefetchScalarGridSpec(
        num_scalar_prefetch=1,
        grid=(b, RET_HEADS, s // tb),
        in_specs=[pl.BlockSpec((1, tb, LANES), col(w)) for w in range(4)]
        + [pl.BlockSpec((1, 1, LANES), lambda i, h, j, lg: (h, 0, 0))],
        out_specs=pl.BlockSpec((1, tb, LANES), lambda i, h, j, lg: (i, j, h)),
        scratch_shapes=[pltpu.VMEM((RET_HEAD_DIM, RET_HEAD_DIM), F32)],
    )
    return pl.pallas_call(
        functools.partial(_ret_kernel, tb=tb),
        grid_spec=grid_spec,
        out_shape=jax.ShapeDtypeStruct((b, s, RET_WIDTH), MXU_DTYPE),
        compiler_params=_params("parallel", "parallel", "arbitrary"),
        name="retention",
    )(log_gamma, proj, proj, proj, proj, gn_gain.astype(F32).reshape(RET_HEADS, 1, RET_HEAD_DIM))


def _diff_kernel(slope_ref, q_ref, k_ref, v_ref, lam_ref, gain_ref, o_ref, m_ref, l_ref, acc_ref,
                 *, tq, lambda_init):
    h = pl.program_id(1)
    qi = pl.program_id(2)
    slope = slope_ref[h]
    lane = lax.broadcasted_iota(I32, (1, LANES), 1)
    q2 = q_ref[0]
    scale = DIFF_HEAD_DIM ** -0.5
    zero = jnp.zeros_like(q2)
    qs = jnp.concatenate([jnp.where(lane < DIFF_HEAD_DIM, q2, zero), jnp.where(lane >= DIFF_HEAD_DIM, q2, zero)],
                         axis=0)
    qs = (qs.astype(F32) * scale).astype(MXU_DTYPE)
    m_ref[...] = jnp.full(m_ref.shape, -jnp.inf, F32)
    l_ref[...] = jnp.zeros_like(l_ref)
    acc_ref[...] = jnp.zeros_like(acc_ref)

    def key_block(j, bias, mask):
        start = pl.multiple_of(j * tq, tq)
        kj = k_ref[0, pl.ds(start, tq), :]
        vj = v_ref[0, pl.ds(start, tq), :]
        z = _dot_nt(qs, kj) + bias
        if mask is not None:
            z = jnp.where(mask, z, -jnp.inf)
        m_old = m_ref[...]
        m_new = jnp.maximum(m_old, jnp.max(z, axis=-1, keepdims=True))
        p = jnp.exp(z - m_new)
        a = jnp.exp(m_old - m_new)
        l_ref[...] = a * l_ref[...] + jnp.sum(p, axis=-1, keepdims=True)
        acc_ref[...] = a * acc_ref[...] + _dot(p.astype(MXU_DTYPE), vj)
        m_ref[...] = m_new

    row = lax.broadcasted_iota(I32, (tq, tq), 0)
    col = lax.broadcasted_iota(I32, (tq, tq), 1)
    visible = col // CHUNK <= row // CHUNK
    bias_d = slope * (row - jnp.abs(row - col)).astype(F32)
    key_block(qi, jnp.concatenate([bias_d, bias_d], axis=0), jnp.concatenate([visible, visible], axis=0))

    key_pos = lax.broadcasted_iota(I32, (1, tq), 1)

    def body(j, _):
        bias = slope * (key_pos + (j - qi) * tq).astype(F32)
        key_block(j, bias, None)
        return 0

    lax.fori_loop(0, qi, body, 0)

    lam_v = lam_ref[...]
    lam = (jnp.exp(jnp.sum(lam_v[0:1] * lam_v[1:2], axis=-1, keepdims=True))
           - jnp.exp(jnp.sum(lam_v[2:3] * lam_v[3:4], axis=-1, keepdims=True)) + lambda_init)
    o = acc_ref[...] / l_ref[...]
    o = o[:tq] - lam * o[tq:]
    o = o * lax.rsqrt(jnp.mean(o * o, axis=-1, keepdims=True) + LN_EPS)
    o_ref[0] = (o * gain_ref[...] * (1.0 - lambda_init)).astype(o_ref.dtype)


def _diff_attention(proj, lam_rows, subln_gain, lambda_init):
    b, s, _ = proj.shape
    tq = min(256, s)
    dv = subln_gain.shape[-1]
    slopes = jnp.exp2(-8.0 / DIFF_HEADS * (jnp.arange(DIFF_HEADS, dtype=F32) + 1.0))
    kb = DIFF_QK // LANES
    grid_spec = pltpu.PrefetchScalarGridSpec(
        num_scalar_prefetch=1,
        grid=(b, DIFF_HEADS, s // tq),
        in_specs=[
            pl.BlockSpec((1, tq, LANES), lambda i, h, j, sl: (i, j, h)),
            pl.BlockSpec((1, s, LANES), lambda i, h, j, sl: (i, 0, kb + h)),
            pl.BlockSpec((1, s, dv), lambda i, h, j, sl: (i, 0, 2 * kb + h)),
            pl.BlockSpec((4, DIFF_HEAD_DIM), lambda i, h, j, sl: (0, 0)),
            pl.BlockSpec((1, dv), lambda i, h, j, sl: (0, 0)),
        ],
        out_specs=pl.BlockSpec((1, tq, dv), lambda i, h, j, sl: (i, j, h)),
        scratch_shapes=[pltpu.VMEM((2 * tq, 1), F32), pltpu.VMEM((2 * tq, 1), F32),
                        pltpu.VMEM((2 * tq, dv), F32)],
    )
    return pl.pallas_call(
        functools.partial(_diff_kernel, tq=tq, lambda_init=lambda_init),
        grid_spec=grid_spec,
        out_shape=jax.ShapeDtypeStruct((b, s, DIFF_HEADS * dv), MXU_DTYPE),
        compiler_params=_params("parallel", "parallel", "parallel"),
        name="diff_attention",
    )(slopes, proj, proj, proj, lam_rows.astype(F32), subln_gain.astype(F32).reshape(1, dv))


def _layer_norm(y, g, b):
    mu = jnp.mean(y, axis=-1, keepdims=True)
    yc = y - mu
    var = jnp.mean(yc * yc, axis=-1, keepdims=True)
    return yc * lax.rsqrt(var + LN_EPS) * g + b


def _outproj_kernel(*refs, n_in, alpha):
    a_refs, w_refs = refs[:n_in], refs[n_in:2 * n_in]
    x_ref, gate_ref, lng_ref, lnb_ref, sc_ref, sh_ref, wr_ref, br_ref, xo_ref, u_ref, lg_ref = refs[2 * n_in:]
    mix = _dot(a_refs[0][0], w_refs[0][...])
    for a_ref, w_ref in zip(a_refs[1:], w_refs[1:]):
        mix += _dot(a_ref[0], w_ref[...])
    xn = _layer_norm(alpha * x_ref[0] + gate_ref[0] * mix, lng_ref[...], lnb_ref[...])
    xo_ref[0] = xn
    u = xn * sc_ref[0] + sh_ref[0]
    u_ref[0] = u
    lg_ref[...] = lax.dot_general(wr_ref[...], u, (((1,), (1,)), ((), ())), preferred_element_type=F32,
                                  precision=HIGHEST) + br_ref[...]


def _outproj(parts, weights, x, gate1p, ln_g, ln_b, scale1p, shift, w_router_t, b_router, alpha):
    b, s, d = x.shape
    tm = min(256, s)
    n_in = len(parts)
    vec = pl.BlockSpec((1, 1, d), lambda i, j: (i, 0, 0))
    row = pl.BlockSpec((1, d), lambda i, j: (0, 0))
    in_specs = [pl.BlockSpec((1, tm, p.shape[-1]), lambda i, j: (i, j, 0)) for p in parts]
    in_specs += [pl.BlockSpec(w.shape, lambda i, j: (0, 0)) for w in weights]
    in_specs += [pl.BlockSpec((1, tm, d), lambda i, j: (i, j, 0)), vec, row, row, vec, vec,
                 pl.BlockSpec((ROUTER_ROWS, d), lambda i, j: (0, 0)),
                 pl.BlockSpec((ROUTER_ROWS, 1), lambda i, j: (0, 0))]
    nb = s // tm
    return pl.pallas_call(
        functools.partial(_outproj_kernel, n_in=n_in, alpha=alpha),
        grid=(b, nb),
        in_specs=in_specs,
        out_specs=[pl.BlockSpec((1, tm, d), lambda i, j: (i, j, 0)),
                   pl.BlockSpec((1, tm, d), lambda i, j: (i, j, 0)),
                   pl.BlockSpec((ROUTER_ROWS, tm), lambda i, j: (0, i * nb + j))],
        out_shape=[jax.ShapeDtypeStruct((b, s, d), F32), jax.ShapeDtypeStruct((b, s, d), F32),
                   jax.ShapeDtypeStruct((ROUTER_ROWS, b * s), F32)],
        compiler_params=_params("parallel", "parallel"),
        name="out_proj_norm",
    )(*parts, *[w.astype(MXU_DTYPE) for w in weights], x, gate1p, ln_g.reshape(1, d), ln_b.reshape(1, d),
      scale1p, shift, w_router_t, b_router)


def _route_kernel(lg_ref, ids_ref, gates_ref, cnt_ref, run_ref, *, tm):
    @pl.when(pl.program_id(0) == 0)
    def _():
        run_ref[...] = jnp.zeros_like(run_ref)

    lg = lg_ref[...]
    g0 = N_EXPERTS
    g_max = lg[g0:g0 + 1]
    grp = jnp.zeros((1, tm), I32)
    for i in range(1, N_GROUPS):
        gi = lg[g0 + i:g0 + i + 1]
        better = gi > g_max
        grp = jnp.where(better, i, grp)
        g_max = jnp.where(better, gi, g_max)
    den = jnp.exp(lg[g0:g0 + 1] - g_max)
    for i in range(1, N_GROUPS):
        den += jnp.exp(lg[g0 + i:g0 + i + 1] - g_max)
    p_grp = 1.0 / den

    cand = lg[0:EXPERTS_PER_GROUP]
    for g in range(1, N_GROUPS):
        cand = jnp.where(grp == g, lg[g * EXPERTS_PER_GROUP:(g + 1) * EXPERTS_PER_GROUP], cand)
    ridx = lax.broadcasted_iota(I32, (EXPERTS_PER_GROUP, tm), 0).astype(F32)
    none = float(EXPERTS_PER_GROUP)
    v1 = jnp.max(cand, axis=0, keepdims=True)
    i1 = jnp.min(jnp.where(cand == v1, ridx, none), axis=0, keepdims=True)
    rest = jnp.where(ridx == i1, -jnp.inf, cand)
    v2 = jnp.max(rest, axis=0, keepdims=True)
    i2 = jnp.min(jnp.where(rest == v2, ridx, none), axis=0, keepdims=True)
    e21 = jnp.exp(v2 - v1)
    gate1 = p_grp / (1.0 + e21)
    gate2 = p_grp * e21 / (1.0 + e21)
    ex1 = grp * EXPERTS_PER_GROUP + i1.astype(I32)
    ex2 = grp * EXPERTS_PER_GROUP + i2.astype(I32)

    eidx = lax.broadcasted_iota(I32, (N_EXPERTS, tm), 0)
    oh1 = jnp.where(eidx == ex1, 1.0, 0.0)
    oh2 = jnp.where(eidx == ex2, 1.0, 0.0)
    oh = (oh1 + oh2).astype(MXU_DTYPE)
    earlier = jnp.where(lax.broadcasted_iota(I32, (tm, tm), 0) < lax.broadcasted_iota(I32, (tm, tm), 1),
                        1.0, 0.0).astype(MXU_DTYPE)
    run = run_ref[...]
    before = _dot(oh, earlier) + jnp.concatenate([run] * (tm // LANES), axis=1)
    rank1 = jnp.sum(oh1 * before, axis=0, keepdims=True).astype(I32)
    rank2 = jnp.sum(oh2 * before, axis=0, keepdims=True).astype(I32)
    run = run + _dot(oh, jnp.ones((tm, LANES), MXU_DTYPE))
    run_ref[...] = run
    cnt_ref[...] = run
    ids_ref[...] = jnp.concatenate([ex1, ex2, rank1, rank2, jnp.zeros((4, tm), I32)], axis=0)
    gates_ref[...] = jnp.concatenate([gate1, gate2, jnp.zeros((6, tm), F32)], axis=0)


def _route(logits_t):
    t = logits_t.shape[1]
    tm = min(512, t)
    return pl.pallas_call(
        functools.partial(_route_kernel, tm=tm),
        grid=(t // tm,),
        in_specs=[pl.BlockSpec((ROUTER_ROWS, tm), lambda i: (0, i))],
        out_specs=[pl.BlockSpec((8, tm), lambda i: (0, i)), pl.BlockSpec((8, tm), lambda i: (0, i)),
                   pl.BlockSpec((N_EXPERTS, LANES), lambda i: (0, 0))],
        out_shape=[jax.ShapeDtypeStruct((8, t), I32), jax.ShapeDtypeStruct((8, t), F32),
                   jax.ShapeDtypeStruct((N_EXPERTS, LANES), F32)],
        scratch_shapes=[pltpu.VMEM((N_EXPERTS, LANES), F32)],
        compiler_params=_params("arbitrary"),
        name="route",
    )(logits_t)


def _gather_rows(idx_ref, src_hbm, dst, sem, n):
    def body(r, _):
        pltpu.make_async_copy(src_hbm.at[pl.ds(idx_ref[0, 0, r], 1), :], dst.at[pl.ds(r, 1), :], sem).start()
        return 0

    lax.fori_loop(0, n, body, 0)


def _wait_rows(src_hbm, dst, sem, n):
    pltpu.make_async_copy(src_hbm.at[pl.ds(0, n), :], dst, sem).wait()


def _expert_kernel(be_ref, nu_ref, idx_ref, idx_next_ref, x_hbm, w1_ref, w3_ref, w2_ref, y_ref, xbuf, sem, *, tm):
    i = pl.program_id(0)
    n_used = nu_ref[0]
    slot = i % 2

    @pl.when(i == 0)
    def _():
        _gather_rows(idx_ref, x_hbm, xbuf.at[0], sem.at[0], tm)

    @pl.when(i + 1 < n_used)
    def _():
        _gather_rows(idx_next_ref, x_hbm, xbuf.at[1 - slot], sem.at[1 - slot], tm)

    @pl.when(i < n_used)
    def _():
        _wait_rows(x_hbm, xbuf.at[slot], sem.at[slot], tm)
        xb = xbuf[slot].astype(MXU_DTYPE)
        hidden = _silu(_dot(xb, w1_ref[0])) * _dot(xb, w3_ref[0])
        y_ref[...] = _dot(hidden.astype(MXU_DTYPE), w2_ref[0])

    @pl.when(i >= n_used)
    def _():
        y_ref[...] = jnp.zeros_like(y_ref)


def _experts(u_flat, slot_tok, blk_expert, n_used, w1, w3, w2, tm):
    t, d = u_flat.shape
    n_blk = slot_tok.shape[0] // tm
    ff = w1.shape[-1]
    idx3 = slot_tok.reshape(n_blk, 1, tm)
    last = n_blk - 1
    grid_spec = pltpu.PrefetchScalarGridSpec(
        num_scalar_prefetch=2,
        grid=(n_blk,),
        in_specs=[
            pl.BlockSpec((1, 1, tm), lambda i, be, nu: (i, 0, 0), memory_space=pltpu.SMEM),
            pl.BlockSpec((1, 1, tm), lambda i, be, nu: (jnp.minimum(i + 1, last), 0, 0), memory_space=pltpu.SMEM),
            pl.BlockSpec(memory_space=pl.ANY),
            pl.BlockSpec((1, d, ff), lambda i, be, nu: (be[i], 0, 0)),
            pl.BlockSpec((1, d, ff), lambda i, be, nu: (be[i], 0, 0)),
            pl.BlockSpec((1, ff, d), lambda i, be, nu: (be[i], 0, 0)),
        ],
        out_specs=pl.BlockSpec((tm, d), lambda i, be, nu: (i, 0)),
        scratch_shapes=[pltpu.VMEM((2, tm, d), F32), pltpu.SemaphoreType.DMA((2,))],
    )
    return pl.pallas_call(
        functools.partial(_expert_kernel, tm=tm),
        grid_spec=grid_spec,
        out_shape=jax.ShapeDtypeStruct((n_blk * tm, d), F32),
        compiler_params=_params("arbitrary"),
        name="experts",
    )(blk_expert, n_used, idx3, idx3, u_flat, w1.astype(MXU_DTYPE), w3.astype(MXU_DTYPE), w2.astype(MXU_DTYPE))


def _combine_kernel(d1_ref, d2_ref, d1n_ref, d2n_ref, y_hbm, gates_ref, x_ref, gate_ref, lng_ref, lnb_ref,
                    o_ref, ybuf, sem, *, tm, alpha):
    i = pl.program_id(0)
    n = pl.num_programs(0)
    slot = i % 2

    def start(a_ref, b_ref, s):
        _gather_rows(a_ref, y_hbm, ybuf.at[s, 0], sem.at[s, 0], tm)
        _gather_rows(b_ref, y_hbm, ybuf.at[s, 1], sem.at[s, 1], tm)

    @pl.when(i == 0)
    def _():
        start(d1_ref, d2_ref, 0)

    @pl.when(i + 1 < n)
    def _():
        start(d1n_ref, d2n_ref, 1 - slot)

    _wait_rows(y_hbm, ybuf.at[slot, 0], sem.at[slot, 0], tm)
    _wait_rows(y_hbm, ybuf.at[slot, 1], sem.at[slot, 1], tm)
    gates = gates_ref[...]
    f = gates[:, 0:1] * ybuf[slot, 0] + gates[:, 1:2] * ybuf[slot, 1]
    o_ref[...] = _layer_norm(alpha * x_ref[...] + gate_ref[0] * f, lng_ref[...], lnb_ref[...])


def _combine(y_slots, dest1, dest2, gates, x_flat, gate1p, ln_g, ln_b, seq, alpha):
    t, d = x_flat.shape
    tm = min(256, seq)
    n_blk = t // tm
    per_seq = seq // tm
    d1 = dest1.reshape(n_blk, 1, tm)
    d2 = dest2.reshape(n_blk, 1, tm)
    cur = pl.BlockSpec((1, 1, tm), lambda i: (i, 0, 0), memory_space=pltpu.SMEM)
    nxt = pl.BlockSpec((1, 1, tm), lambda i: (jnp.minimum(i + 1, n_blk - 1), 0, 0), memory_space=pltpu.SMEM)
    row = pl.BlockSpec((1, d), lambda i: (0, 0))
    return pl.pallas_call(
        functools.partial(_combine_kernel, tm=tm, alpha=alpha),
        grid=(n_blk,),
        in_specs=[cur, cur, nxt, nxt, pl.BlockSpec(memory_space=pl.ANY),
                  pl.BlockSpec((tm, TOP_K), lambda i: (i, 0)),
                  pl.BlockSpec((tm, d), lambda i: (i, 0)),
                  pl.BlockSpec((1, 1, d), lambda i: (i // per_seq, 0, 0)), row, row],
        out_specs=pl.BlockSpec((tm, d), lambda i: (i, 0)),
        out_shape=jax.ShapeDtypeStruct((t, d), F32),
        scratch_shapes=[pltpu.VMEM((2, 2, tm, d), F32), pltpu.SemaphoreType.DMA((2, 2))],
        compiler_params=_params("arbitrary"),
        name="combine_norm",
    )(d1, d2, d1, d2, y_slots, gates, x_flat, gate1p, ln_g.reshape(1, d), ln_b.reshape(1, d))


EXPERT_TILE = 256


def _moe(u, logits_t, x, gate2p, ln_g, ln_b, w1, w3, w2, alpha):
    b, s, d = x.shape
    t = b * s
    ids, gates8, cnt = _route(logits_t)
    counts = cnt[:, 0].astype(I32)
    padded = (counts + EXPERT_TILE - 1) // EXPERT_TILE * EXPERT_TILE
    pad_end = jnp.cumsum(padded)
    pad_start = pad_end - padded
    dest1 = pad_start[ids[0]] + ids[2]
    dest2 = pad_start[ids[1]] + ids[3]
    n_slots = t * TOP_K + N_EXPERTS * EXPERT_TILE
    n_blk = n_slots // EXPERT_TILE
    tok = jnp.arange(t, dtype=I32)
    slot_tok = jnp.zeros((n_slots,), I32).at[dest1].set(tok).at[dest2].set(tok)
    blk_expert = jnp.minimum(
        jnp.searchsorted(pad_end, jnp.arange(n_blk, dtype=I32) * EXPERT_TILE, side='right'),
        N_EXPERTS - 1).astype(I32)
    n_used = (pad_end[-1:] // EXPERT_TILE).astype(I32)
    y_slots = _experts(u.reshape(t, d), slot_tok, blk_expert, n_used, w1, w3, w2, EXPERT_TILE)
    out = _combine(y_slots, dest1, dest2, gates8[:TOP_K].T, x.reshape(t, d), gate2p, ln_g, ln_b, s, alpha)
    return out.reshape(b, s, d)


def kernel(x, c, ln1_g, ln1_b, ln2_g, ln2_b, w_ada, b_ada, even_w_in, even_w_out, ret_gn_g, odd_w_in, odd_w_out, lambda_q1, lambda_k1, lambda_q2, lambda_k2, diff_subln_g, moe_w_group, moe_b_group, moe_w_router, moe_b_router, moe_w1, moe_w3, moe_w2):
    b, s, d = x.shape
    depth = w_ada.shape[0]
    alpha = (2.0 * depth) ** 0.25
    mod = _ada(c, w_ada, b_ada)
    for l in range(depth):
        sh1, sc1, g1, sh2, sc2, g2 = [m[:, None, :] for m in jnp.split(mod[l], 6, axis=-1)]
        i = l // 2
        if l % 2 == 0:
            proj = _inproj(x, 1.0 + sc1, sh1, even_w_in[i])
            parts = [_sb_attention(proj), _retention(proj, ret_gn_g[i])]
            w_out = even_w_out[i]
            weights = [w_out[:SB_WIDTH], w_out[SB_WIDTH:]]
        else:
            proj = _inproj(x, 1.0 + sc1, sh1, odd_w_in[i])
            lambda_init = 0.8 - 0.6 * math.exp(-0.3 * l)
            lam_rows = jnp.stack([lambda_q1[i], lambda_k1[i], lambda_q2[i], lambda_k2[i]])
            parts = [_diff_attention(proj, lam_rows, diff_subln_g[i], lambda_init)]
            weights = [odd_w_out[i]]
        w_router_t = jnp.zeros((ROUTER_ROWS, d), F32).at[:N_EXPERTS].set(moe_w_router[l].T.astype(F32))
        w_router_t = w_router_t.at[N_EXPERTS:N_EXPERTS + N_GROUPS].set(moe_w_group[l].T.astype(F32))
        b_router = jnp.zeros((ROUTER_ROWS, 1), F32).at[:N_EXPERTS, 0].set(moe_b_router[l].astype(F32))
        b_router = b_router.at[N_EXPERTS:N_EXPERTS + N_GROUPS, 0].set(moe_b_group[l].astype(F32))
        x, u, logits_t = _outproj(parts, weights, x, 1.0 + g1, ln1_g[l], ln1_b[l], 1.0 + sc2, sh2,
                                  w_router_t, b_router, alpha)
        x = _moe(u, logits_t, x, 1.0 + g2, ln2_g[l], ln2_b[l], moe_w1[l], moe_w3[l], moe_w2[l], alpha)
    return x
```

```python
import functools
import math

import jax
import jax.numpy as jnp
from jax import lax
from jax.experimental import pallas as pl
from jax.experimental.pallas import tpu as pltpu

F32 = jnp.float32
I32 = jnp.int32
MXU_DTYPE = jnp.bfloat16
HIGHEST = lax.Precision.HIGHEST
LOG2E = math.log2(math.e)

LN_EPS = 1e-5
CHUNK = 64
LANES = 128
SB_HEADS, SB_HEAD_DIM = 8, 64
RET_HEADS, RET_HEAD_DIM = 4, 128
DIFF_HEADS, DIFF_HEAD_DIM = 8, 64
SB_WIDTH = SB_HEADS * SB_HEAD_DIM
RET_WIDTH = RET_HEADS * RET_HEAD_DIM
DIFF_QK = DIFF_HEADS * 2 * DIFF_HEAD_DIM
N_GROUPS, EXPERTS_PER_GROUP = 4, 8
N_EXPERTS = N_GROUPS * EXPERTS_PER_GROUP
TOP_K = 2
ROUTER_ROWS = 40

VMEM_LIMIT = 56 * 1024 * 1024


def _params(*sem):
    return pltpu.CompilerParams(dimension_semantics=sem, vmem_limit_bytes=VMEM_LIMIT)


def _dot(a, b):
    return jnp.dot(a, b, preferred_element_type=F32)


def _dot_nt(a, b):
    return lax.dot_general(a, b, (((1,), (1,)), ((), ())), preferred_element_type=F32)


def _dot_tn(a, b):
    return lax.dot_general(a, b, (((0,), (0,)), ((), ())), preferred_element_type=F32)


def _silu(x):
    return x * (1.0 / (1.0 + jnp.exp(-x)))


def _ada_kernel(c_ref, w_ref, b_ref, o_ref):
    o_ref[0] = jnp.dot(_silu(c_ref[...]), w_ref[0], preferred_element_type=F32, precision=HIGHEST) + b_ref[0]


def _ada(c, w_ada, b_ada):
    depth, d, n = w_ada.shape
    bp = 8
    cp = jnp.zeros((bp, d), F32).at[: c.shape[0]].set(c)
    tn = 1536
    out = pl.pallas_call(
        _ada_kernel,
        grid=(depth, n // tn),
        in_specs=[
            pl.BlockSpec((bp, d), lambda l, j: (0, 0)),
            pl.BlockSpec((1, d, tn), lambda l, j: (l, 0, j)),
            pl.BlockSpec((1, 1, tn), lambda l, j: (l, 0, j)),
        ],
        out_specs=pl.BlockSpec((1, bp, tn), lambda l, j: (l, 0, j)),
        out_shape=jax.ShapeDtypeStruct((depth, bp, n), F32),
        compiler_params=_params("parallel", "parallel"),
        name="ada_mod",
    )(cp, w_ada, b_ada.reshape(depth, 1, n))
    return out[:, : c.shape[0]]


def _inproj_kernel(x_ref, sc_ref, sh_ref, w_ref, wvt_ref, o_ref, vt_ref, *, tn):
    u = (x_ref[0] * sc_ref[0] + sh_ref[0]).astype(MXU_DTYPE)
    for j in range(o_ref.shape[2] // tn):
        o_ref[0, :, j * tn:(j + 1) * tn] = _dot(u, w_ref[:, j * tn:(j + 1) * tn]).astype(o_ref.dtype)
    for j in range(vt_ref.shape[1] // tn):
        vt_ref[0, j * tn:(j + 1) * tn, :] = _dot_nt(wvt_ref[j * tn:(j + 1) * tn, :], u).astype(vt_ref.dtype)


def _inproj(x, scale1p, shift, w, w_v):
    b, s, d = x.shape
    n, n_v = w.shape[1], w_v.shape[1]
    tm = min(512, s)
    return pl.pallas_call(
        functools.partial(_inproj_kernel, tn=512),
        grid=(b, s // tm),
        in_specs=[
            pl.BlockSpec((1, tm, d), lambda i, j: (i, j, 0)),
            pl.BlockSpec((1, 1, d), lambda i, j: (i, 0, 0)),
            pl.BlockSpec((1, 1, d), lambda i, j: (i, 0, 0)),
            pl.BlockSpec((d, n), lambda i, j: (0, 0)),
            pl.BlockSpec((n_v, d), lambda i, j: (0, 0)),
        ],
        out_specs=[pl.BlockSpec((1, tm, n), lambda i, j: (i, j, 0)),
                   pl.BlockSpec((1, n_v, tm), lambda i, j: (i, 0, j))],
        out_shape=[jax.ShapeDtypeStruct((b, s, n), MXU_DTYPE), jax.ShapeDtypeStruct((b, n_v, s), MXU_DTYPE)],
        compiler_params=_params("parallel", "parallel"),
        name="in_proj",
    )(x, scale1p, shift, w.astype(MXU_DTYPE), w_v.T.astype(MXU_DTYPE))


def _sb_kernel(q_ref, k_ref, vt_ref, o_ref, acc_ref, carry_ref, *, tq):
    qi = pl.program_id(2)
    tk = tq
    extra = 16
    lane = lax.broadcasted_iota(I32, (1, LANES), 1)
    key = lax.broadcasted_iota(I32, (tk, tq), 0)
    qry = lax.broadcasted_iota(I32, (tk, tq), 1)
    causal = key < qry
    r = lax.broadcasted_iota(I32, (tk + extra, tk), 0)
    c = lax.broadcasted_iota(I32, (tk + extra, tk), 1)
    neg_later = jnp.where(r >= tk, -1.0, jnp.where(c > r, -1.0, 0.0)).astype(MXU_DTYPE)
    q2 = q_ref[0]
    scale2 = SB_HEAD_DIM ** -0.5 * LOG2E

    qh = [(jnp.where((lane // SB_HEAD_DIM) == h, q2, jnp.zeros_like(q2)).astype(F32) * scale2).astype(MXU_DTYPE)
          for h in range(2)]
    acc_ref[...] = jnp.zeros_like(acc_ref)
    carry_ref[...] = jnp.zeros_like(carry_ref)

    def key_block(j, diagonal):
        start = pl.multiple_of(j * tk, tk)
        kj = k_ref[0, pl.ds(start, tk), :]
        vtj = vt_ref[0, :, pl.ds(start, tk)]
        for h in range(2):
            y = _dot_nt(kj, qh[h])
            sp = jnp.maximum(y, 0.0) + jnp.log2(1.0 + jnp.exp2(-jnp.abs(y)))
            if diagonal:
                sp = jnp.where(causal, sp, 0.0)
            sums = _dot(neg_later, sp.astype(MXU_DTYPE))
            carry = carry_ref[h]
            w = jnp.exp2(y - sp + sums[:tk] + carry)
            if diagonal:
                w = jnp.where(causal, w, 0.0)
            acc_ref[h] += _dot(vtj, w.astype(MXU_DTYPE))
            carry_ref[h] = carry + sums[tk:tk + 1]

    key_block(qi, True)

    def body(it, _):
        key_block(qi - 1 - it, False)
        return 0

    lax.fori_loop(0, qi, body, 0)
    sub = lax.broadcasted_iota(I32, (LANES, 1), 0)
    o_ref[0] = jnp.where(sub < SB_HEAD_DIM, acc_ref[0], acc_ref[1]).T.astype(o_ref.dtype)


def _sb_attention(proj, v_t):
    b, s, _ = proj.shape
    tq = min(256, s)
    n_pairs = SB_WIDTH // LANES
    return pl.pallas_call(
        functools.partial(_sb_kernel, tq=tq),
        grid=(b, n_pairs, s // tq),
        in_specs=[
            pl.BlockSpec((1, tq, LANES), lambda i, p, j: (i, j, p)),
            pl.BlockSpec((1, s, LANES), lambda i, p, j: (i, 0, n_pairs + p)),
            pl.BlockSpec((1, LANES, s), lambda i, p, j: (i, p, 0)),
        ],
        out_specs=pl.BlockSpec((1, tq, LANES), lambda i, p, j: (i, j, p)),
        out_shape=jax.ShapeDtypeStruct((b, s, SB_WIDTH), MXU_DTYPE),
        scratch_shapes=[pltpu.VMEM((2, LANES, tq), F32), pltpu.VMEM((2, 1, tq), F32)],
        compiler_params=_params("parallel", "parallel", "parallel"),
        name="sb_attention",
    )(proj, proj, v_t)


def _ret_kernel(lg_ref, q_ref, k_ref, v_ref, g_ref, gn_ref, o_ref, state_ref, *, tb):
    h = pl.program_id(1)
    blk = pl.program_id(2)

    @pl.when(blk == 0)
    def _():
        state_ref[...] = jnp.zeros_like(state_ref)

    lg = lg_ref[h]
    scale = RET_HEAD_DIM ** -0.5
    q = q_ref[0].astype(F32)
    k = k_ref[0].astype(F32)
    v = v_ref[0]
    row = lax.broadcasted_iota(I32, (tb, tb), 0)
    col = lax.broadcasted_iota(I32, (tb, tb), 1)
    dist = jnp.abs(row - col).astype(F32)
    decay = jnp.where(col // CHUNK <= row // CHUNK, jnp.exp(lg * dist) * scale, 0.0)
    pos = lax.broadcasted_iota(I32, (tb, 1), 0).astype(F32)
    scores = _dot_nt(q.astype(MXU_DTYPE), k.astype(MXU_DTYPE)) * decay
    intra = _dot(scores.astype(MXU_DTYPE), v)
    state = state_ref[...]
    q_in = (q * jnp.exp(lg * (pos + 1.0))).astype(MXU_DTYPE)
    inter = _dot(q_in, state.astype(MXU_DTYPE))
    k_out = (k * (jnp.exp(lg * (tb - 1.0 - pos)) * scale)).astype(MXU_DTYPE)
    block_decay = jnp.exp(lg * jnp.full((1, RET_HEAD_DIM), float(tb), F32))
    state_ref[...] = block_decay * state + _dot_tn(k_out, v)
    o = intra + inter
    mu = jnp.mean(o, axis=-1, keepdims=True)
    oc = o - mu
    var = jnp.mean(oc * oc, axis=-1, keepdims=True)
    o = oc * lax.rsqrt(var + LN_EPS) * gn_ref[0] * _silu(g_ref[0].astype(F32))
    o_ref[0] = o.astype(o_ref.dtype)


def _retention(proj, gn_gain, first_col_block):
    b, s, _ = proj.shape
    tb = min(256, s)
    log_gamma = jnp.log1p(-jnp.exp2(-5.0 - jnp.arange(RET_HEADS, dtype=F32)))

    def col(which):
        return lambda i, h, j, lg: (i, j, first_col_block + which * RET_HEADS + h)

    grid_spec = pltpu.PrefetchScalarGridSpec(
        num_scalar_prefetch=1,
        grid=(b, RET_HEADS, s // tb),
        in_specs=[pl.BlockSpec((1, tb, LANES), col(w)) for w in range(4)]
        + [pl.BlockSpec((1, 1, LANES), lambda i, h, j, lg: (h, 0, 0))],
        out_specs=pl.BlockSpec((1, tb, LANES), lambda i, h, j, lg: (i, j, h)),
        scratch_shapes=[pltpu.VMEM((RET_HEAD_DIM, RET_HEAD_DIM), F32)],
    )
    return pl.pallas_call(
        functools.partial(_ret_kernel, tb=tb),
        grid_spec=grid_spec,
        out_shape=jax.ShapeDtypeStruct((b, s, RET_WIDTH), MXU_DTYPE),
        compiler_params=_params("parallel", "parallel", "arbitrary"),
        name="retention",
    )(log_gamma, proj, proj, proj, proj, gn_gain.astype(F32).reshape(RET_HEADS, 1, RET_HEAD_DIM))


def _diff_kernel(slope_ref, q_ref, k_ref, vt_ref, lam_ref, gain_ref, o_ref, m_ref, l_ref, acc_ref, base_ref,
                 *, tq, lambda_init):
    hp = pl.program_id(1)
    qi = pl.program_id(2)
    tk = tq
    nh = DIFF_HEADS_PER_STEP
    dv = LANES
    lane = lax.broadcasted_iota(I32, (1, LANES), 1)
    scale2 = DIFF_HEAD_DIM ** -0.5 * LOG2E
    slope2, qs = [], []
    for hh in range(nh):
        slope2.append(slope_ref[hp * nh + hh] * LOG2E)
        q2 = q_ref[0, :, hh * LANES:(hh + 1) * LANES]
        zero = jnp.zeros_like(q2)
        stacked = jnp.concatenate([jnp.where(lane < DIFF_HEAD_DIM, q2, zero),
                                   jnp.where(lane >= DIFF_HEAD_DIM, q2, zero)], axis=0)
        qs.append((stacked.astype(F32) * scale2).astype(MXU_DTYPE))
        base_ref[hh] = slope2[hh] * lax.broadcasted_iota(I32, (tk, LANES), 0).astype(F32)
    m_ref[...] = jnp.full(m_ref.shape, -jnp.inf, F32)
    l_ref[...] = jnp.zeros_like(l_ref)
    acc_ref[...] = jnp.zeros_like(acc_ref)

    def key_block(j, diagonal):
        start = pl.multiple_of(j * tk, tk)
        for hh in range(nh):
            kj = k_ref[0, pl.ds(start, tk), hh * LANES:(hh + 1) * LANES]
            vtj = vt_ref[0, hh * dv:(hh + 1) * dv, pl.ds(start, tk)]
            z = _dot_nt(kj, qs[hh])
            if diagonal:
                key = lax.broadcasted_iota(I32, (tk, tq), 0)
                qry = lax.broadcasted_iota(I32, (tk, tq), 1)
                visible = key // CHUNK <= qry // CHUNK
                bias_d = slope2[hh] * (qry - jnp.abs(qry - key)).astype(F32)
                z = z + jnp.concatenate([bias_d, bias_d], axis=1)
                z = jnp.where(jnp.concatenate([visible, visible], axis=1), z, -jnp.inf)
                shift = 0.0
            else:
                z = z + jnp.concatenate([base_ref[hh]] * (2 * tq // LANES), axis=1)
                shift = slope2[hh] * ((j - qi) * tq).astype(F32)
            m_old = m_ref[hh]
            m_new = jnp.maximum(m_old, jnp.max(z, axis=0, keepdims=True) + shift)
            p = jnp.exp2(z - (m_new - shift))
            a = jnp.exp2(m_old - m_new)
            l_ref[hh] = a * l_ref[hh] + jnp.sum(p, axis=0, keepdims=True)
            acc_ref[hh] = a * acc_ref[hh] + _dot(vtj, p.astype(MXU_DTYPE))
            m_ref[hh] = m_new

    key_block(qi, True)

    def body(j, _):
        key_block(j, False)
        return 0

    lax.fori_loop(0, qi, body, 0)

    lam_v = lam_ref[...]
    lam = (jnp.exp(jnp.sum(lam_v[0:1] * lam_v[1:2], axis=-1, keepdims=True))
           - jnp.exp(jnp.sum(lam_v[2:3] * lam_v[3:4], axis=-1, keepdims=True)) + lambda_init)
    for hh in range(nh):
        o = acc_ref[hh] * (1.0 / l_ref[hh])
        o = o[:, :tq] - lam * o[:, tq:]
        o = o * lax.rsqrt(jnp.mean(o * o, axis=0, keepdims=True) + LN_EPS)
        o_ref[0, :, hh * dv:(hh + 1) * dv] = (o * gain_ref[...] * (1.0 - lambda_init)).T.astype(o_ref.dtype)


DIFF_HEADS_PER_STEP = 2


def _diff_attention(proj, v_t, lam_rows, subln_gain, lambda_init):
    b, s, _ = proj.shape
    tq = min(256, s)
    dv = subln_gain.shape[-1]
    nh = DIFF_HEADS_PER_STEP
    slopes = jnp.exp2(-8.0 / DIFF_HEADS * (jnp.arange(DIFF_HEADS, dtype=F32) + 1.0))
    kb = DIFF_QK // (nh * LANES)
    grid_spec = pltpu.PrefetchScalarGridSpec(
        num_scalar_prefetch=1,
        grid=(b, DIFF_HEADS // nh, s // tq),
        in_specs=[
            pl.BlockSpec((1, tq, nh * LANES), lambda i, h, j, sl: (i, j, h)),
            pl.BlockSpec((1, s, nh * LANES), lambda i, h, j, sl: (i, 0, kb + h)),
            pl.BlockSpec((1, nh * dv, s), lambda i, h, j, sl: (i, h, 0)),
            pl.BlockSpec((4, DIFF_HEAD_DIM), lambda i, h, j, sl: (0, 0)),
            pl.BlockSpec((dv, 1), lambda i, h, j, sl: (0, 0)),
        ],
        out_specs=pl.BlockSpec((1, tq, nh * dv), lambda i, h, j, sl: (i, j, h)),
        scratch_shapes=[pltpu.VMEM((nh, 1, 2 * tq), F32), pltpu.VMEM((nh, 1, 2 * tq), F32),
                        pltpu.VMEM((nh, dv, 2 * tq), F32), pltpu.VMEM((nh, tq, LANES), F32)],
    )
    return pl.pallas_call(
        functools.partial(_diff_kernel, tq=tq, lambda_init=lambda_init),
        grid_spec=grid_spec,
        out_shape=jax.ShapeDtypeStruct((b, s, DIFF_HEADS * dv), MXU_DTYPE),
        compiler_params=_params("parallel", "parallel", "parallel"),
        name="diff_attention",
    )(slopes, proj, proj, v_t, lam_rows.astype(F32), subln_gain.astype(F32).reshape(dv, 1))


def _layer_norm(y, g, b):
    mu = jnp.mean(y, axis=-1, keepdims=True)
    yc = y - mu
    var = jnp.mean(yc * yc, axis=-1, keepdims=True)
    return yc * lax.rsqrt(var + LN_EPS) * g + b


def _outproj_kernel(*refs, n_in, alpha):
    a_refs, w_refs = refs[:n_in], refs[n_in:2 * n_in]
    x_ref, gate_ref, lng_ref, lnb_ref, sc_ref, sh_ref, wr_ref, br_ref, xo_ref, u_ref, lg_ref = refs[2 * n_in:]
    mix = _dot(a_refs[0][0], w_refs[0][...])
    for a_ref, w_ref in zip(a_refs[1:], w_refs[1:]):
        mix += _dot(a_ref[0], w_ref[...])
    xn = _layer_norm(alpha * x_ref[0] + gate_ref[0] * mix, lng_ref[...], lnb_ref[...])
    xo_ref[0] = xn
    u = xn * sc_ref[0] + sh_ref[0]
    u_ref[0] = u
    lg_ref[...] = lax.dot_general(wr_ref[...], u, (((1,), (1,)), ((), ())), preferred_element_type=F32,
                                  precision=HIGHEST) + br_ref[...]


def _outproj(parts, weights, x, gate1p, ln_g, ln_b, scale1p, shift, w_router_t, b_router, alpha):
    b, s, d = x.shape
    tm = min(256, s)
    n_in = len(parts)
    vec = pl.BlockSpec((1, 1, d), lambda i, j: (i, 0, 0))
    row = pl.BlockSpec((1, d), lambda i, j: (0, 0))
    in_specs = [pl.BlockSpec((1, tm, p.shape[-1]), lambda i, j: (i, j, 0)) for p in parts]
    in_specs += [pl.BlockSpec(w.shape, lambda i, j: (0, 0)) for w in weights]
    in_specs += [pl.BlockSpec((1, tm, d), lambda i, j: (i, j, 0)), vec, row, row, vec, vec,
                 pl.BlockSpec((ROUTER_ROWS, d), lambda i, j: (0, 0)),
                 pl.BlockSpec((ROUTER_ROWS, 1), lambda i, j: (0, 0))]
    nb = s // tm
    return pl.pallas_call(
        functools.partial(_outproj_kernel, n_in=n_in, alpha=alpha),
        grid=(b, nb),
        in_specs=in_specs,
        out_specs=[pl.BlockSpec((1, tm, d), lambda i, j: (i, j, 0)),
                   pl.BlockSpec((1, tm, d), lambda i, j: (i, j, 0)),
                   pl.BlockSpec((ROUTER_ROWS, tm), lambda i, j: (0, i * nb + j))],
        out_shape=[jax.ShapeDtypeStruct((b, s, d), F32), jax.ShapeDtypeStruct((b, s, d), F32),
                   jax.ShapeDtypeStruct((ROUTER_ROWS, b * s), F32)],
        compiler_params=_params("parallel", "parallel"),
        name="out_proj_norm",
    )(*parts, *[w.astype(MXU_DTYPE) for w in weights], x, gate1p, ln_g.reshape(1, d), ln_b.reshape(1, d),
      scale1p, shift, w_router_t, b_router)


def _route_kernel(lg_ref, ids_ref, gates_ref, cnt_ref, run_ref, *, tm):
    @pl.when(pl.program_id(0) == 0)
    def _():
        run_ref[...] = jnp.zeros_like(run_ref)

    lg = lg_ref[...]
    g0 = N_EXPERTS
    g_max = lg[g0:g0 + 1]
    grp = jnp.zeros((1, tm), I32)
    for i in range(1, N_GROUPS):
        gi = lg[g0 + i:g0 + i + 1]
        better = gi > g_max
        grp = jnp.where(better, i, grp)
        g_max = jnp.where(better, gi, g_max)
    den = jnp.exp(lg[g0:g0 + 1] - g_max)
    for i in range(1, N_GROUPS):
        den += jnp.exp(lg[g0 + i:g0 + i + 1] - g_max)
    p_grp = 1.0 / den

    cand = lg[0:EXPERTS_PER_GROUP]
    for g in range(1, N_GROUPS):
        cand = jnp.where(grp == g, lg[g * EXPERTS_PER_GROUP:(g + 1) * EXPERTS_PER_GROUP], cand)
    ridx = lax.broadcasted_iota(I32, (EXPERTS_PER_GROUP, tm), 0).astype(F32)
    none = float(EXPERTS_PER_GROUP)
    v1 = jnp.max(cand, axis=0, keepdims=True)
    i1 = jnp.min(jnp.where(cand == v1, ridx, none), axis=0, keepdims=True)
    rest = jnp.where(ridx == i1, -jnp.inf, cand)
    v2 = jnp.max(rest, axis=0, keepdims=True)
    i2 = jnp.min(jnp.where(rest == v2, ridx, none), axis=0, keepdims=True)
    e21 = jnp.exp(v2 - v1)
    gate1 = p_grp / (1.0 + e21)
    gate2 = p_grp * e21 / (1.0 + e21)
    ex1 = grp * EXPERTS_PER_GROUP + i1.astype(I32)
    ex2 = grp * EXPERTS_PER_GROUP + i2.astype(I32)

    eidx = lax.broadcasted_iota(I32, (N_EXPERTS, tm), 0)
    oh1 = jnp.where(eidx == ex1, 1.0, 0.0)
    oh2 = jnp.where(eidx == ex2, 1.0, 0.0)
    oh = (oh1 + oh2).astype(MXU_DTYPE)
    earlier = jnp.where(lax.broadcasted_iota(I32, (tm, tm), 0) < lax.broadcasted_iota(I32, (tm, tm), 1),
                        1.0, 0.0).astype(MXU_DTYPE)
    run = run_ref[...]
    before = _dot(oh, earlier) + jnp.concatenate([run] * (tm // LANES), axis=1)
    rank1 = jnp.sum(oh1 * before, axis=0, keepdims=True).astype(I32)
    rank2 = jnp.sum(oh2 * before, axis=0, keepdims=True).astype(I32)
    run = run + _dot(oh, jnp.ones((tm, LANES), MXU_DTYPE))
    run_ref[...] = run
    cnt_ref[...] = run
    ids_ref[...] = jnp.concatenate([ex1, ex2, rank1, rank2, jnp.zeros((4, tm), I32)], axis=0)
    gates_ref[...] = jnp.concatenate([gate1, gate2, jnp.zeros((6, tm), F32)], axis=0)


def _route(logits_t):
    t = logits_t.shape[1]
    tm = min(512, t)
    return pl.pallas_call(
        functools.partial(_route_kernel, tm=tm),
        grid=(t // tm,),
        in_specs=[pl.BlockSpec((ROUTER_ROWS, tm), lambda i: (0, i))],
        out_specs=[pl.BlockSpec((8, tm), lambda i: (0, i)), pl.BlockSpec((8, tm), lambda i: (0, i)),
                   pl.BlockSpec((N_EXPERTS, LANES), lambda i: (0, 0))],
        out_shape=[jax.ShapeDtypeStruct((8, t), I32), jax.ShapeDtypeStruct((8, t), F32),
                   jax.ShapeDtypeStruct((N_EXPERTS, LANES), F32)],
        scratch_shapes=[pltpu.VMEM((N_EXPERTS, LANES), F32)],
        compiler_params=_params("arbitrary"),
        name="route",
    )(logits_t)


def _gather_rows(idx_ref, src_hbm, dst, sem, n):
    def body(r, _):
        pltpu.make_async_copy(src_hbm.at[pl.ds(idx_ref[0, 0, r], 1), :], dst.at[pl.ds(r, 1), :], sem).start()
        return 0

    lax.fori_loop(0, n, body, 0)


def _wait_rows(src_hbm, dst, sem, n):
    pltpu.make_async_copy(src_hbm.at[pl.ds(0, n), :], dst, sem).wait()


def _expert_kernel(be_ref, nu_ref, idx_ref, idx_next_ref, x_hbm, w1_ref, w3_ref, w2_ref, y_ref, xbuf, sem, *, tm):
    i = pl.program_id(0)
    n_used = nu_ref[0]
    slot = i % 2

    @pl.when(i == 0)
    def _():
        _gather_rows(idx_ref, x_hbm, xbuf.at[0], sem.at[0], tm)

    @pl.when(i + 1 < n_used)
    def _():
        _gather_rows(idx_next_ref, x_hbm, xbuf.at[1 - slot], sem.at[1 - slot], tm)

    @pl.when(i < n_used)
    def _():
        _wait_rows(x_hbm, xbuf.at[slot], sem.at[slot], tm)
        xb = xbuf[slot].astype(MXU_DTYPE)
        hidden = _silu(_dot(xb, w1_ref[0])) * _dot(xb, w3_ref[0])
        y_ref[...] = _dot(hidden.astype(MXU_DTYPE), w2_ref[0])

    @pl.when(i >= n_used)
    def _():
        y_ref[...] = jnp.zeros_like(y_ref)


def _experts(u_flat, slot_tok, blk_expert, n_used, w1, w3, w2, tm):
    t, d = u_flat.shape
    n_blk = slot_tok.shape[0] // tm
    ff = w1.shape[-1]
    idx3 = slot_tok.reshape(n_blk, 1, tm)
    last = n_blk - 1
    grid_spec = pltpu.PrefetchScalarGridSpec(
        num_scalar_prefetch=2,
        grid=(n_blk,),
        in_specs=[
            pl.BlockSpec((1, 1, tm), lambda i, be, nu: (i, 0, 0), memory_space=pltpu.SMEM),
            pl.BlockSpec((1, 1, tm), lambda i, be, nu: (jnp.minimum(i + 1, last), 0, 0), memory_space=pltpu.SMEM),
            pl.BlockSpec(memory_space=pl.ANY),
            pl.BlockSpec((1, d, ff), lambda i, be, nu: (be[i], 0, 0)),
            pl.BlockSpec((1, d, ff), lambda i, be, nu: (be[i], 0, 0)),
            pl.BlockSpec((1, ff, d), lambda i, be, nu: (be[i], 0, 0)),
        ],
        out_specs=pl.BlockSpec((tm, d), lambda i, be, nu: (i, 0)),
        scratch_shapes=[pltpu.VMEM((2, tm, d), F32), pltpu.SemaphoreType.DMA((2,))],
    )
    return pl.pallas_call(
        functools.partial(_expert_kernel, tm=tm),
        grid_spec=grid_spec,
        out_shape=jax.ShapeDtypeStruct((n_blk * tm, d), F32),
        compiler_params=_params("arbitrary"),
        name="experts",
    )(blk_expert, n_used, idx3, idx3, u_flat, w1.astype(MXU_DTYPE), w3.astype(MXU_DTYPE), w2.astype(MXU_DTYPE))


def _combine_kernel(d1_ref, d2_ref, d1n_ref, d2n_ref, y_hbm, gates_ref, x_ref, gate_ref, lng_ref, lnb_ref,
                    o_ref, ybuf, sem, *, tm, alpha):
    i = pl.program_id(0)
    n = pl.num_programs(0)
    slot = i % 2

    def start(a_ref, b_ref, s):
        _gather_rows(a_ref, y_hbm, ybuf.at[s, 0], sem.at[s, 0], tm)
        _gather_rows(b_ref, y_hbm, ybuf.at[s, 1], sem.at[s, 1], tm)

    @pl.when(i == 0)
    def _():
        start(d1_ref, d2_ref, 0)

    @pl.when(i + 1 < n)
    def _():
        start(d1n_ref, d2n_ref, 1 - slot)

    _wait_rows(y_hbm, ybuf.at[slot, 0], sem.at[slot, 0], tm)
    _wait_rows(y_hbm, ybuf.at[slot, 1], sem.at[slot, 1], tm)
    gates = gates_ref[...]
    f = gates[:, 0:1] * ybuf[slot, 0] + gates[:, 1:2] * ybuf[slot, 1]
    o_ref[...] = _layer_norm(alpha * x_ref[...] + gate_ref[0] * f, lng_ref[...], lnb_ref[...])


def _combine(y_slots, dest1, dest2, gates, x_flat, gate1p, ln_g, ln_b, seq, alpha):
    t, d = x_flat.shape
    tm = min(256, seq)
    n_blk = t // tm
    per_seq = seq // tm
    d1 = dest1.reshape(n_blk, 1, tm)
    d2 = dest2.reshape(n_blk, 1, tm)
    cur = pl.BlockSpec((1, 1, tm), lambda i: (i, 0, 0), memory_space=pltpu.SMEM)
    nxt = pl.BlockSpec((1, 1, tm), lambda i: (jnp.minimum(i + 1, n_blk - 1), 0, 0), memory_space=pltpu.SMEM)
    row = pl.BlockSpec((1, d), lambda i: (0, 0))
    return pl.pallas_call(
        functools.partial(_combine_kernel, tm=tm, alpha=alpha),
        grid=(n_blk,),
        in_specs=[cur, cur, nxt, nxt, pl.BlockSpec(memory_space=pl.ANY),
                  pl.BlockSpec((tm, TOP_K), lambda i: (i, 0)),
                  pl.BlockSpec((tm, d), lambda i: (i, 0)),
                  pl.BlockSpec((1, 1, d), lambda i: (i // per_seq, 0, 0)), row, row],
        out_specs=pl.BlockSpec((tm, d), lambda i: (i, 0)),
        out_shape=jax.ShapeDtypeStruct((t, d), F32),
        scratch_shapes=[pltpu.VMEM((2, 2, tm, d), F32), pltpu.SemaphoreType.DMA((2, 2))],
        compiler_params=_params("arbitrary"),
        name="combine_norm",
    )(d1, d2, d1, d2, y_slots, gates, x_flat, gate1p, ln_g.reshape(1, d), ln_b.reshape(1, d))


EXPERT_TILE = 256


def _moe(u, logits_t, x, gate2p, ln_g, ln_b, w1, w3, w2, alpha):
    b, s, d = x.shape
    t = b * s
    ids, gates8, cnt = _route(logits_t)
    counts = cnt[:, 0].astype(I32)
    padded = (counts + EXPERT_TILE - 1) // EXPERT_TILE * EXPERT_TILE
    pad_end = jnp.cumsum(padded)
    pad_start = pad_end - padded
    dest1 = pad_start[ids[0]] + ids[2]
    dest2 = pad_start[ids[1]] + ids[3]
    n_slots = t * TOP_K + N_EXPERTS * EXPERT_TILE
    n_blk = n_slots // EXPERT_TILE
    tok = jnp.arange(t, dtype=I32)
    slot_tok = jnp.zeros((n_slots,), I32).at[dest1].set(tok).at[dest2].set(tok)
    blk_expert = jnp.minimum(
        jnp.searchsorted(pad_end, jnp.arange(n_blk, dtype=I32) * EXPERT_TILE, side='right'),
        N_EXPERTS - 1).astype(I32)
    n_used = (pad_end[-1:] // EXPERT_TILE).astype(I32)
    y_slots = _experts(u.reshape(t, d), slot_tok, blk_expert, n_used, w1, w3, w2, EXPERT_TILE)
    out = _combine(y_slots, dest1, dest2, gates8[:TOP_K].T, x.reshape(t, d), gate2p, ln_g, ln_b, s, alpha)
    return out.reshape(b, s, d)


def kernel(x, c, ln1_g, ln1_b, ln2_g, ln2_b, w_ada, b_ada, even_w_in, even_w_out, ret_gn_g, odd_w_in, odd_w_out, lambda_q1, lambda_k1, lambda_q2, lambda_k2, diff_subln_g, moe_w_group, moe_b_group, moe_w_router, moe_b_router, moe_w1, moe_w3, moe_w2):
    b, s, d = x.shape
    depth = w_ada.shape[0]
    alpha = (2.0 * depth) ** 0.25
    mod = _ada(c, w_ada, b_ada)
    for l in range(depth):
        sh1, sc1, g1, sh2, sc2, g2 = [m[:, None, :] for m in jnp.split(mod[l], 6, axis=-1)]
        i = l // 2
        if l % 2 == 0:
            w_in = even_w_in[i]
            w_main = jnp.concatenate([w_in[:, :2 * SB_WIDTH], w_in[:, 3 * SB_WIDTH:]], axis=1)
            proj, v_t = _inproj(x, 1.0 + sc1, sh1, w_main, w_in[:, 2 * SB_WIDTH:3 * SB_WIDTH])
            parts = [_sb_attention(proj, v_t), _retention(proj, ret_gn_g[i], 2 * SB_WIDTH // LANES)]
            w_out = even_w_out[i]
            weights = [w_out[:SB_WIDTH], w_out[SB_WIDTH:]]
        else:
            w_in = odd_w_in[i]
            proj, v_t = _inproj(x, 1.0 + sc1, sh1, w_in[:, :2 * DIFF_QK], w_in[:, 2 * DIFF_QK:])
            lambda_init = 0.8 - 0.6 * math.exp(-0.3 * l)
            lam_rows = jnp.stack([lambda_q1[i], lambda_k1[i], lambda_q2[i], lambda_k2[i]])
            parts = [_diff_attention(proj, v_t, lam_rows, diff_subln_g[i], lambda_init)]
            weights = [odd_w_out[i]]
        w_router_t = jnp.zeros((ROUTER_ROWS, d), F32).at[:N_EXPERTS].set(moe_w_router[l].T.astype(F32))
        w_router_t = w_router_t.at[N_EXPERTS:N_EXPERTS + N_GROUPS].set(moe_w_group[l].T.astype(F32))
        b_router = jnp.zeros((ROUTER_ROWS, 1), F32).at[:N_EXPERTS, 0].set(moe_b_router[l].astype(F32))
        b_router = b_router.at[N_EXPERTS:N_EXPERTS + N_GROUPS, 0].set(moe_b_group[l].astype(F32))
        x, u, logits_t = _outproj(parts, weights, x, 1.0 + g1, ln1_g[l], ln1_b[l], 1.0 + sc2, sh2,
                                  w_router_t, b_router, alpha)
        x = _moe(u, logits_t, x, 1.0 + g2, ln2_g[l], ln2_b[l], moe_w1[l], moe_w3[l], moe_w2[l], alpha)
    return x
```

```python
import functools
import math

import jax
import jax.numpy as jnp
import numpy as np
from jax import lax
from jax.experimental import pallas as pl
from jax.experimental.pallas import tpu as pltpu

F32 = jnp.float32
I32 = jnp.int32
MXU_DTYPE = jnp.bfloat16
HIGHEST = lax.Precision.HIGHEST
LOG2E = math.log2(math.e)

LN_EPS = 1e-5
CHUNK = 64
LANES = 128
SB_HEADS, SB_HEAD_DIM = 8, 64
RET_HEADS, RET_HEAD_DIM = 4, 128
DIFF_HEADS, DIFF_HEAD_DIM = 8, 64
SB_WIDTH = SB_HEADS * SB_HEAD_DIM
RET_WIDTH = RET_HEADS * RET_HEAD_DIM
DIFF_QK = DIFF_HEADS * 2 * DIFF_HEAD_DIM
N_GROUPS, EXPERTS_PER_GROUP = 4, 8
N_EXPERTS = N_GROUPS * EXPERTS_PER_GROUP
TOP_K = 2
ROUTER_ROWS = 40

VMEM_LIMIT = 56 * 1024 * 1024


def _params(*sem):
    return pltpu.CompilerParams(dimension_semantics=sem, vmem_limit_bytes=VMEM_LIMIT)


def _dot(a, b):
    return jnp.dot(a, b, preferred_element_type=F32)


def _dot_nt(a, b):
    return lax.dot_general(a, b, (((1,), (1,)), ((), ())), preferred_element_type=F32)


def _dot_tn(a, b):
    return lax.dot_general(a, b, (((0,), (0,)), ((), ())), preferred_element_type=F32)


def _silu(x):
    return x * (1.0 / (1.0 + jnp.exp(-x)))


def _ada_kernel(c_ref, w_ref, b_ref, o_ref):
    o_ref[0] = jnp.dot(_silu(c_ref[...]), w_ref[0], preferred_element_type=F32, precision=HIGHEST) + b_ref[0]


def _ada(c, w_ada, b_ada):
    depth, d, n = w_ada.shape
    bp = 8
    cp = jnp.zeros((bp, d), F32).at[: c.shape[0]].set(c)
    tn = 1536
    out = pl.pallas_call(
        _ada_kernel,
        grid=(depth, n // tn),
        in_specs=[
            pl.BlockSpec((bp, d), lambda l, j: (0, 0)),
            pl.BlockSpec((1, d, tn), lambda l, j: (l, 0, j)),
            pl.BlockSpec((1, 1, tn), lambda l, j: (l, 0, j)),
        ],
        out_specs=pl.BlockSpec((1, bp, tn), lambda l, j: (l, 0, j)),
        out_shape=jax.ShapeDtypeStruct((depth, bp, n), F32),
        compiler_params=_params("parallel", "parallel"),
        name="ada_mod",
    )(cp, w_ada, b_ada.reshape(depth, 1, n))
    return out[:, : c.shape[0]]


def _inproj_kernel(x_ref, sc_ref, sh_ref, w_ref, wvt_ref, o_ref, vt_ref, *, tn):
    u = (x_ref[0] * sc_ref[0] + sh_ref[0]).astype(MXU_DTYPE)
    for j in range(o_ref.shape[2] // tn):
        o_ref[0, :, j * tn:(j + 1) * tn] = _dot(u, w_ref[:, j * tn:(j + 1) * tn]).astype(o_ref.dtype)
    for j in range(vt_ref.shape[1] // tn):
        vt_ref[0, j * tn:(j + 1) * tn, :] = _dot_nt(wvt_ref[j * tn:(j + 1) * tn, :], u).astype(vt_ref.dtype)


def _inproj(x, scale1p, shift, w, w_v):
    b, s, d = x.shape
    n, n_v = w.shape[1], w_v.shape[1]
    tm = min(512, s)
    return pl.pallas_call(
        functools.partial(_inproj_kernel, tn=512),
        grid=(b, s // tm),
        in_specs=[
            pl.BlockSpec((1, tm, d), lambda i, j: (i, j, 0)),
            pl.BlockSpec((1, 1, d), lambda i, j: (i, 0, 0)),
            pl.BlockSpec((1, 1, d), lambda i, j: (i, 0, 0)),
            pl.BlockSpec((d, n), lambda i, j: (0, 0)),
            pl.BlockSpec((n_v, d), lambda i, j: (0, 0)),
        ],
        out_specs=[pl.BlockSpec((1, tm, n), lambda i, j: (i, j, 0)),
                   pl.BlockSpec((1, n_v, tm), lambda i, j: (i, 0, j))],
        out_shape=[jax.ShapeDtypeStruct((b, s, n), MXU_DTYPE), jax.ShapeDtypeStruct((b, n_v, s), MXU_DTYPE)],
        compiler_params=_params("parallel", "parallel"),
        name="in_proj",
    )(x, scale1p, shift, w.astype(MXU_DTYPE), w_v.T.astype(MXU_DTYPE))


MASKED = -float("inf")
ITEM_PLAIN, ITEM_DIAGONAL, ITEM_NULL = 0, 1, 2
FLAG_FIRST, FLAG_LAST = 1, 2


def _triangle_items(n_q, pad, diagonal_first):
    items = []
    for qb in range(n_q):
        order = range(qb, -1, -1) if diagonal_first else range(qb + 1)
        for n, kb in enumerate(order):
            flags = (FLAG_FIRST if n == 0 else 0) | (FLAG_LAST if n == qb else 0)
            items.append((qb, kb, ITEM_DIAGONAL if kb == qb else ITEM_PLAIN, flags))
    null = (0, 0, ITEM_NULL, 0)
    return np.asarray([null] * pad + items + [null] * (pad + 1), np.int32).T.copy(), len(items)


SB_STAGES = 5


def _sb_kernel(tab_ref, q_ref, k_ref, vt_ref, o_ref, qh_s, acc_ref, carry_ref, mask_ref, y_s, sp_s, yms_s,
               sums_s, w_s, *, tq, n_items):
    tk = tq
    extra = 16
    heads = range(2)
    lane = lax.broadcasted_iota(I32, (1, LANES), 1)
    key = lax.broadcasted_iota(I32, (tk, tq), 0)
    qry = lax.broadcasted_iota(I32, (tk, tq), 1)
    r = lax.broadcasted_iota(I32, (tk + extra, tk), 0)
    c = lax.broadcasted_iota(I32, (tk + extra, tk), 1)
    neg_later = jnp.where(r >= tk, -1.0, jnp.where(c > r, -1.0, 0.0)).astype(MXU_DTYPE)
    scale2 = SB_HEAD_DIM ** -0.5 * LOG2E

    def prepare_queries(blk, _):
        rows = pl.ds(pl.multiple_of(blk * tq, tq), tq)
        q2 = q_ref[0, rows, :]
        for h in heads:
            qh_s[h, rows, :] = (jnp.where((lane // SB_HEAD_DIM) == h, q2, jnp.zeros_like(q2)).astype(F32)
                                * scale2).astype(MXU_DTYPE)
        return 0

    lax.fori_loop(0, q_ref.shape[1] // tq, prepare_queries, 0)
    acc_ref[...] = jnp.zeros_like(acc_ref)
    carry_ref[...] = jnp.zeros_like(carry_ref)
    mask_ref[ITEM_PLAIN] = jnp.zeros((tk, tq), F32)
    mask_ref[ITEM_DIAGONAL] = jnp.where(key < qry, 0.0, MASKED)
    mask_ref[ITEM_NULL] = jnp.full((tk, tq), MASKED, F32)
    y_s[...] = jnp.zeros_like(y_s)
    sp_s[...] = jnp.zeros_like(sp_s)
    yms_s[...] = jnp.full(yms_s.shape, MASKED, F32)
    sums_s[...] = jnp.zeros_like(sums_s)
    w_s[...] = jnp.zeros_like(w_s)

    def rows_of(block, size):
        return pl.ds(pl.multiple_of(block * size, size), size)

    def trip(it, parity):
        col_a, col_a1, col_b1, col_c = it + 4, it + 3, it + 1, it
        kj = k_ref[0, rows_of(tab_ref[1, col_a], tk), :]
        qrows = rows_of(tab_ref[0, col_a], tq)
        for h in heads:
            y_s[h, parity] = _dot_nt(kj, qh_s[h, qrows, :])
            sums_s[h, parity] = _dot(neg_later, sp_s[h])
        vtj = vt_ref[0, :, rows_of(tab_ref[1, col_c], tk)]
        keep = jnp.where((tab_ref[3, col_c] & FLAG_FIRST) != 0, 0.0, 1.0)
        for h in heads:
            acc_ref[h] = acc_ref[h] * keep + _dot(vtj, w_s[h])
        keep_carry = jnp.where((tab_ref[3, col_b1] & FLAG_FIRST) != 0, 0.0, 1.0)
        for h in heads:
            sums = sums_s[h, 1 - parity]
            carry = carry_ref[h] * keep_carry
            w_s[h] = jnp.exp2(yms_s[h, parity] + sums[:tk] + carry).astype(MXU_DTYPE)
            carry_ref[h] = carry + sums[tk:tk + 1]
        mask = mask_ref[tab_ref[2, col_a1]]
        for h in heads:
            ym = y_s[h, 1 - parity] + mask
            sp = jnp.maximum(ym, 0.0) + jnp.log2(1.0 + jnp.exp2(-jnp.abs(ym)))
            sp_s[h] = sp.astype(MXU_DTYPE)
            yms_s[h, parity] = ym - sp

        @pl.when((tab_ref[3, col_c] & FLAG_LAST) != 0)
        def _():
            sub = lax.broadcasted_iota(I32, (LANES, 1), 0)
            o_ref[0, rows_of(tab_ref[0, col_c], tq), :] = jnp.where(
                sub < SB_HEAD_DIM, acc_ref[0], acc_ref[1]).T.astype(o_ref.dtype)

    def trip_pair(i, _):
        trip(2 * i, 0)
        trip(2 * i + 1, 1)
        return 0

    lax.fori_loop(0, pl.cdiv(n_items + SB_STAGES - 1, 2), trip_pair, 0)


def _sb_attention(proj, v_t):
    b, s, _ = proj.shape
    tq = min(256, s)
    n_pairs = SB_WIDTH // LANES
    table, n_items = _triangle_items(s // tq, SB_STAGES - 1, diagonal_first=True)
    grid_spec = pltpu.PrefetchScalarGridSpec(
        num_scalar_prefetch=1,
        grid=(b, n_pairs),
        in_specs=[
            pl.BlockSpec((1, s, LANES), lambda i, p, t: (i, 0, p)),
            pl.BlockSpec((1, s, LANES), lambda i, p, t: (i, 0, n_pairs + p)),
            pl.BlockSpec((1, LANES, s), lambda i, p, t: (i, p, 0)),
        ],
        out_specs=pl.BlockSpec((1, s, LANES), lambda i, p, t: (i, 0, p)),
        scratch_shapes=[
            pltpu.VMEM((2, s, LANES), MXU_DTYPE),
            pltpu.VMEM((2, LANES, tq), F32),
            pltpu.VMEM((2, 1, tq), F32),
            pltpu.VMEM((3, tq, tq), F32),
            pltpu.VMEM((2, 2, tq, tq), F32),
            pltpu.VMEM((2, tq, tq), MXU_DTYPE),
            pltpu.VMEM((2, 2, tq, tq), F32),
            pltpu.VMEM((2, 2, tq + 16, tq), F32),
            pltpu.VMEM((2, tq, tq), MXU_DTYPE),
        ],
    )
    return pl.pallas_call(
        functools.partial(_sb_kernel, tq=tq, n_items=n_items),
        grid_spec=grid_spec,
        out_shape=jax.ShapeDtypeStruct((b, s, SB_WIDTH), MXU_DTYPE),
        compiler_params=_params("parallel", "parallel"),
        name="sb_attention",
    )(jnp.asarray(table), proj, proj, v_t)


def _sb_kernel_old(q_ref, k_ref, vt_ref, o_ref, acc_ref, carry_ref, *, tq):
    qi = pl.program_id(2)
    tk = tq
    extra = 16
    lane = lax.broadcasted_iota(I32, (1, LANES), 1)
    key = lax.broadcasted_iota(I32, (tk, tq), 0)
    qry = lax.broadcasted_iota(I32, (tk, tq), 1)
    causal = key < qry
    r = lax.broadcasted_iota(I32, (tk + extra, tk), 0)
    c = lax.broadcasted_iota(I32, (tk + extra, tk), 1)
    neg_later = jnp.where(r >= tk, -1.0, jnp.where(c > r, -1.0, 0.0)).astype(MXU_DTYPE)
    q2 = q_ref[0]
    scale2 = SB_HEAD_DIM ** -0.5 * LOG2E

    qh = [(jnp.where((lane // SB_HEAD_DIM) == h, q2, jnp.zeros_like(q2)).astype(F32) * scale2).astype(MXU_DTYPE)
          for h in range(2)]
    acc_ref[...] = jnp.zeros_like(acc_ref)
    carry_ref[...] = jnp.zeros_like(carry_ref)

    def key_block(j, diagonal):
        start = pl.multiple_of(j * tk, tk)
        kj = k_ref[0, pl.ds(start, tk), :]
        vtj = vt_ref[0, :, pl.ds(start, tk)]
        for h in range(2):
            y = _dot_nt(kj, qh[h])
            sp = jnp.maximum(y, 0.0) + jnp.log2(1.0 + jnp.exp2(-jnp.abs(y)))
            if diagonal:
                sp = jnp.where(causal, sp, 0.0)
            sums = _dot(neg_later, sp.astype(MXU_DTYPE))
            carry = carry_ref[h]
            w = jnp.exp2(y - sp + sums[:tk] + carry)
            if diagonal:
                w = jnp.where(causal, w, 0.0)
            acc_ref[h] += _dot(vtj, w.astype(MXU_DTYPE))
            carry_ref[h] = carry + sums[tk:tk + 1]

    key_block(qi, True)

    def body(it, _):
        key_block(qi - 1 - it, False)
        return 0

    lax.fori_loop(0, qi, body, 0)
    sub = lax.broadcasted_iota(I32, (LANES, 1), 0)
    o_ref[0] = jnp.where(sub < SB_HEAD_DIM, acc_ref[0], acc_ref[1]).T.astype(o_ref.dtype)


def _sb_attention_old(proj, v_t):
    b, s, _ = proj.shape
    tq = min(256, s)
    n_pairs = SB_WIDTH // LANES
    return pl.pallas_call(
        functools.partial(_sb_kernel_old, tq=tq),
        grid=(b, n_pairs, s // tq),
        in_specs=[
            pl.BlockSpec((1, tq, LANES), lambda i, p, j: (i, j, p)),
            pl.BlockSpec((1, s, LANES), lambda i, p, j: (i, 0, n_pairs + p)),
            pl.BlockSpec((1, LANES, s), lambda i, p, j: (i, p, 0)),
        ],
        out_specs=pl.BlockSpec((1, tq, LANES), lambda i, p, j: (i, j, p)),
        out_shape=jax.ShapeDtypeStruct((b, s, SB_WIDTH), MXU_DTYPE),
        scratch_shapes=[pltpu.VMEM((2, LANES, tq), F32), pltpu.VMEM((2, 1, tq), F32)],
        compiler_params=_params("parallel", "parallel", "parallel"),
        name="sb_attention_old",
    )(proj, proj, v_t)


def _ret_kernel(lg_ref, q_ref, k_ref, v_ref, g_ref, gn_ref, o_ref, state_ref, *, tb):
    h = pl.program_id(1)
    blk = pl.program_id(2)

    @pl.when(blk == 0)
    def _():
        state_ref[...] = jnp.zeros_like(state_ref)

    lg = lg_ref[h]
    scale = RET_HEAD_DIM ** -0.5
    q = q_ref[0].astype(F32)
    k = k_ref[0].astype(F32)
    v = v_ref[0]
    row = lax.broadcasted_iota(I32, (tb, tb), 0)
    col = lax.broadcasted_iota(I32, (tb, tb), 1)
    dist = jnp.abs(row - col).astype(F32)
    decay = jnp.where(col // CHUNK <= row // CHUNK, jnp.exp(lg * dist) * scale, 0.0)
    pos = lax.broadcasted_iota(I32, (tb, 1), 0).astype(F32)
    scores = _dot_nt(q.astype(MXU_DTYPE), k.astype(MXU_DTYPE)) * decay
    intra = _dot(scores.astype(MXU_DTYPE), v)
    state = state_ref[...]
    q_in = (q * jnp.exp(lg * (pos + 1.0))).astype(MXU_DTYPE)
    inter = _dot(q_in, state.astype(MXU_DTYPE))
    k_out = (k * (jnp.exp(lg * (tb - 1.0 - pos)) * scale)).astype(MXU_DTYPE)
    block_decay = jnp.exp(lg * jnp.full((1, RET_HEAD_DIM), float(tb), F32))
    state_ref[...] = block_decay * state + _dot_tn(k_out, v)
    o = intra + inter
    mu = jnp.mean(o, axis=-1, keepdims=True)
    oc = o - mu
    var = jnp.mean(oc * oc, axis=-1, keepdims=True)
    o = oc * lax.rsqrt(var + LN_EPS) * gn_ref[0] * _silu(g_ref[0].astype(F32))
    o_ref[0] = o.astype(o_ref.dtype)


def _retention(proj, gn_gain, first_col_block):
    b, s, _ = proj.shape
    tb = min(256, s)
    log_gamma = jnp.log1p(-jnp.exp2(-5.0 - jnp.arange(RET_HEADS, dtype=F32)))

    def col(which):
        return lambda i, h, j, lg: (i, j, first_col_block + which * RET_HEADS + h)

    grid_spec = pltpu.PrefetchScalarGridSpec(
        num_scalar_prefetch=1,
        grid=(b, RET_HEADS, s // tb),
        in_specs=[pl.BlockSpec((1, tb, LANES), col(w)) for w in range(4)]
        + [pl.BlockSpec((1, 1, LANES), lambda i, h, j, lg: (h, 0, 0))],
        out_specs=pl.BlockSpec((1, tb, LANES), lambda i, h, j, lg: (i, j, h)),
        scratch_shapes=[pltpu.VMEM((RET_HEAD_DIM, RET_HEAD_DIM), F32)],
    )
    return pl.pallas_call(
        functools.partial(_ret_kernel, tb=tb),
        grid_spec=grid_spec,
        out_shape=jax.ShapeDtypeStruct((b, s, RET_WIDTH), MXU_DTYPE),
        compiler_params=_params("parallel", "parallel", "arbitrary"),
        name="retention",
    )(log_gamma, proj, proj, proj, proj, gn_gain.astype(F32).reshape(RET_HEADS, 1, RET_HEAD_DIM))


DIFF_STAGES = 3
DIFF_HEADS_PER_STEP = 2


def _diff_kernel(slope_ref, tab_ref, q_ref, k_ref, vt_ref, lam_ref, gain_ref, o_ref, qs_s, m_ref, l_ref, acc_ref,
                 bias_ref, z_s, p_s, a_s, lfin_s, *, tq, n_items, lambda_init):
    hp = pl.program_id(1)
    tk = tq
    nh = DIFF_HEADS_PER_STEP
    dv = LANES
    heads = range(nh)
    lane = lax.broadcasted_iota(I32, (1, LANES), 1)
    scale2 = DIFF_HEAD_DIM ** -0.5 * LOG2E
    slope2 = [slope_ref[hp * nh + hh] * LOG2E for hh in heads]

    def rows_of(block, size):
        return pl.ds(pl.multiple_of(block * size, size), size)

    def prepare_queries(blk, _):
        for hh in heads:
            q2 = q_ref[0, rows_of(blk, tq), hh * LANES:(hh + 1) * LANES]
            zero = jnp.zeros_like(q2)
            stacked = jnp.concatenate([jnp.where(lane < DIFF_HEAD_DIM, q2, zero),
                                       jnp.where(lane >= DIFF_HEAD_DIM, q2, zero)], axis=0)
            qs_s[hh, rows_of(blk, 2 * tq), :] = (stacked.astype(F32) * scale2).astype(MXU_DTYPE)
        return 0

    lax.fori_loop(0, q_ref.shape[1] // tq, prepare_queries, 0)
    key = lax.broadcasted_iota(I32, (tk, tq), 0)
    qry = lax.broadcasted_iota(I32, (tk, tq), 1)
    visible = key // CHUNK <= qry // CHUNK
    for hh in heads:
        plain = slope2[hh] * key.astype(F32)
        diag = jnp.where(visible, slope2[hh] * (qry - jnp.abs(qry - key)).astype(F32), MASKED)
        bias_ref[hh, ITEM_PLAIN] = jnp.concatenate([plain, plain], axis=1)
        bias_ref[hh, ITEM_DIAGONAL] = jnp.concatenate([diag, diag], axis=1)
    m_ref[...] = jnp.zeros_like(m_ref)
    l_ref[...] = jnp.zeros_like(l_ref)
    acc_ref[...] = jnp.zeros_like(acc_ref)
    z_s[...] = jnp.zeros_like(z_s)
    p_s[...] = jnp.zeros_like(p_s)
    a_s[...] = jnp.ones_like(a_s)
    lfin_s[...] = jnp.ones_like(lfin_s)

    def trip(it, parity):
        col_a, col_a1, col_c = it + 2, it + 1, it
        krows = rows_of(tab_ref[1, col_a], tk)
        qrows = rows_of(tab_ref[0, col_a], 2 * tq)
        for hh in heads:
            z_s[hh, parity] = _dot_nt(k_ref[0, krows, hh * LANES:(hh + 1) * LANES], qs_s[hh, qrows, :])
        vrows = rows_of(tab_ref[1, col_c], tk)
        for hh in heads:
            acc_ref[hh] = a_s[hh] * acc_ref[hh] + _dot(vt_ref[0, hh * dv:(hh + 1) * dv, vrows], p_s[hh])
        lfin = [lfin_s[hh] for hh in heads]
        kind = tab_ref[2, col_a1]
        first = (tab_ref[3, col_a1] & FLAG_FIRST) != 0
        offset = ((tab_ref[1, col_a1] - tab_ref[0, col_a1]) * tq).astype(F32)
        for hh in heads:
            shift = jnp.where(kind == ITEM_PLAIN, slope2[hh] * offset, jnp.where(kind == ITEM_DIAGONAL, 0.0, MASKED))
            z = z_s[hh, 1 - parity] + bias_ref[hh, jnp.minimum(kind, ITEM_DIAGONAL)]
            m_old = jnp.where(first, MASKED, m_ref[hh])
            m_new = jnp.maximum(m_old, jnp.max(z, axis=0, keepdims=True) + shift)
            p = jnp.exp2(z - (m_new - shift))
            a = jnp.exp2(m_old - m_new)
            l_new = a * l_ref[hh] + jnp.sum(p, axis=0, keepdims=True)
            p_s[hh] = p.astype(MXU_DTYPE)
            a_s[hh] = a
            lfin_s[hh] = l_new
            l_ref[hh] = l_new
            m_ref[hh] = m_new

        @pl.when((tab_ref[3, col_c] & FLAG_LAST) != 0)
        def _():
            lam_v = lam_ref[...]
            lam = (jnp.exp(jnp.sum(lam_v[0:1] * lam_v[1:2], axis=-1, keepdims=True))
                   - jnp.exp(jnp.sum(lam_v[2:3] * lam_v[3:4], axis=-1, keepdims=True)) + lambda_init)
            orows = rows_of(tab_ref[0, col_c], tq)
            for hh in heads:
                o = acc_ref[hh] * (1.0 / lfin[hh])
                o = o[:, :tq] - lam * o[:, tq:]
                o = o * lax.rsqrt(jnp.mean(o * o, axis=0, keepdims=True) + LN_EPS)
                o_ref[0, orows, hh * dv:(hh + 1) * dv] = (o * gain_ref[...] * (1.0 - lambda_init)).T.astype(
                    o_ref.dtype)

    def trip_pair(i, _):
        trip(2 * i, 0)
        trip(2 * i + 1, 1)
        return 0

    lax.fori_loop(0, pl.cdiv(n_items + DIFF_STAGES - 1, 2), trip_pair, 0)


def _diff_attention(proj, v_t, lam_rows, subln_gain, lambda_init):
    b, s, _ = proj.shape
    tq = min(256, s)
    dv = subln_gain.shape[-1]
    nh = DIFF_HEADS_PER_STEP
    slopes = jnp.exp2(-8.0 / DIFF_HEADS * (jnp.arange(DIFF_HEADS, dtype=F32) + 1.0))
    kb = DIFF_QK // (nh * LANES)
    table, n_items = _triangle_items(s // tq, DIFF_STAGES - 1, diagonal_first=False)
    grid_spec = pltpu.PrefetchScalarGridSpec(
        num_scalar_prefetch=2,
        grid=(b, DIFF_HEADS // nh),
        in_specs=[
            pl.BlockSpec((1, s, nh * LANES), lambda i, h, sl, t: (i, 0, h)),
            pl.BlockSpec((1, s, nh * LANES), lambda i, h, sl, t: (i, 0, kb + h)),
            pl.BlockSpec((1, nh * dv, s), lambda i, h, sl, t: (i, h, 0)),
            pl.BlockSpec((4, DIFF_HEAD_DIM), lambda i, h, sl, t: (0, 0)),
            pl.BlockSpec((dv, 1), lambda i, h, sl, t: (0, 0)),
        ],
        out_specs=pl.BlockSpec((1, s, nh * dv), lambda i, h, sl, t: (i, 0, h)),
        scratch_shapes=[
            pltpu.VMEM((nh, 2 * s, LANES), MXU_DTYPE),
            pltpu.VMEM((nh, 1, 2 * tq), F32),
            pltpu.VMEM((nh, 1, 2 * tq), F32),
            pltpu.VMEM((nh, dv, 2 * tq), F32),
            pltpu.VMEM((nh, 2, tq, 2 * tq), F32),
            pltpu.VMEM((nh, 2, tq, 2 * tq), F32),
            pltpu.VMEM((nh, tq, 2 * tq), MXU_DTYPE),
            pltpu.VMEM((nh, 1, 2 * tq), F32),
            pltpu.VMEM((nh, 1, 2 * tq), F32),
        ],
    )
    return pl.pallas_call(
        functools.partial(_diff_kernel, tq=tq, n_items=n_items, lambda_init=lambda_init),
        grid_spec=grid_spec,
        out_shape=jax.ShapeDtypeStruct((b, s, DIFF_HEADS * dv), MXU_DTYPE),
        compiler_params=_params("parallel", "parallel"),
        name="diff_attention",
    )(slopes, jnp.asarray(table), proj, proj, v_t, lam_rows.astype(F32), subln_gain.astype(F32).reshape(dv, 1))


def _diff_kernel_old(slope_ref, q_ref, k_ref, vt_ref, lam_ref, gain_ref, o_ref, m_ref, l_ref, acc_ref, base_ref,
                     *, tq, lambda_init):
    hp = pl.program_id(1)
    qi = pl.program_id(2)
    tk = tq
    nh = DIFF_HEADS_PER_STEP
    dv = LANES
    lane = lax.broadcasted_iota(I32, (1, LANES), 1)
    scale2 = DIFF_HEAD_DIM ** -0.5 * LOG2E
    slope2, qs = [], []
    for hh in range(nh):
        slope2.append(slope_ref[hp * nh + hh] * LOG2E)
        q2 = q_ref[0, :, hh * LANES:(hh + 1) * LANES]
        zero = jnp.zeros_like(q2)
        stacked = jnp.concatenate([jnp.where(lane < DIFF_HEAD_DIM, q2, zero),
                                   jnp.where(lane >= DIFF_HEAD_DIM, q2, zero)], axis=0)
        qs.append((stacked.astype(F32) * scale2).astype(MXU_DTYPE))
        base_ref[hh] = slope2[hh] * lax.broadcasted_iota(I32, (tk, LANES), 0).astype(F32)
    m_ref[...] = jnp.full(m_ref.shape, -jnp.inf, F32)
    l_ref[...] = jnp.zeros_like(l_ref)
    acc_ref[...] = jnp.zeros_like(acc_ref)

    def key_block(j, diagonal):
        start = pl.multiple_of(j * tk, tk)
        for hh in range(nh):
            kj = k_ref[0, pl.ds(start, tk), hh * LANES:(hh + 1) * LANES]
            vtj = vt_ref[0, hh * dv:(hh + 1) * dv, pl.ds(start, tk)]
            z = _dot_nt(kj, qs[hh])
            if diagonal:
                key = lax.broadcasted_iota(I32, (tk, tq), 0)
                qry = lax.broadcasted_iota(I32, (tk, tq), 1)
                visible = key // CHUNK <= qry // CHUNK
                bias_d = slope2[hh] * (qry - jnp.abs(qry - key)).astype(F32)
                z = z + jnp.concatenate([bias_d, bias_d], axis=1)
                z = jnp.where(jnp.concatenate([visible, visible], axis=1), z, -jnp.inf)
                shift = 0.0
            else:
                z = z + jnp.concatenate([base_ref[hh]] * (2 * tq // LANES), axis=1)
                shift = slope2[hh] * ((j - qi) * tq).astype(F32)
            m_old = m_ref[hh]
            m_new = jnp.maximum(m_old, jnp.max(z, axis=0, keepdims=True) + shift)
            p = jnp.exp2(z - (m_new - shift))
            a = jnp.exp2(m_old - m_new)
            l_ref[hh] = a * l_ref[hh] + jnp.sum(p, axis=0, keepdims=True)
            acc_ref[hh] = a * acc_ref[hh] + _dot(vtj, p.astype(MXU_DTYPE))
            m_ref[hh] = m_new

    key_block(qi, True)

    def body(j, _):
        key_block(j, False)
        return 0

    lax.fori_loop(0, qi, body, 0)

    lam_v = lam_ref[...]
    lam = (jnp.exp(jnp.sum(lam_v[0:1] * lam_v[1:2], axis=-1, keepdims=True))
           - jnp.exp(jnp.sum(lam_v[2:3] * lam_v[3:4], axis=-1, keepdims=True)) + lambda_init)
    for hh in range(nh):
        o = acc_ref[hh] * (1.0 / l_ref[hh])
        o = o[:, :tq] - lam * o[:, tq:]
        o = o * lax.rsqrt(jnp.mean(o * o, axis=0, keepdims=True) + LN_EPS)
        o_ref[0, :, hh * dv:(hh + 1) * dv] = (o * gain_ref[...] * (1.0 - lambda_init)).T.astype(o_ref.dtype)


def _diff_attention_old(proj, v_t, lam_rows, subln_gain, lambda_init):
    b, s, _ = proj.shape
    tq = min(256, s)
    dv = subln_gain.shape[-1]
    nh = DIFF_HEADS_PER_STEP
    slopes = jnp.exp2(-8.0 / DIFF_HEADS * (jnp.arange(DIFF_HEADS, dtype=F32) + 1.0))
    kb = DIFF_QK // (nh * LANES)
    grid_spec = pltpu.PrefetchScalarGridSpec(
        num_scalar_prefetch=1,
        grid=(b, DIFF_HEADS // nh, s // tq),
        in_specs=[
            pl.BlockSpec((1, tq, nh * LANES), lambda i, h, j, sl: (i, j, h)),
            pl.BlockSpec((1, s, nh * LANES), lambda i, h, j, sl: (i, 0, kb + h)),
            pl.BlockSpec((1, nh * dv, s), lambda i, h, j, sl: (i, h, 0)),
            pl.BlockSpec((4, DIFF_HEAD_DIM), lambda i, h, j, sl: (0, 0)),
            pl.BlockSpec((dv, 1), lambda i, h, j, sl: (0, 0)),
        ],
        out_specs=pl.BlockSpec((1, tq, nh * dv), lambda i, h, j, sl: (i, j, h)),
        scratch_shapes=[pltpu.VMEM((nh, 1, 2 * tq), F32), pltpu.VMEM((nh, 1, 2 * tq), F32),
                        pltpu.VMEM((nh, dv, 2 * tq), F32), pltpu.VMEM((nh, tq, LANES), F32)],
    )
    return pl.pallas_call(
        functools.partial(_diff_kernel_old, tq=tq, lambda_init=lambda_init),
        grid_spec=grid_spec,
        out_shape=jax.ShapeDtypeStruct((b, s, DIFF_HEADS * dv), MXU_DTYPE),
        compiler_params=_params("parallel", "parallel", "parallel"),
        name="diff_attention_old",
    )(slopes, proj, proj, v_t, lam_rows.astype(F32), subln_gain.astype(F32).reshape(dv, 1))


def _layer_norm(y, g, b):
    mu = jnp.mean(y, axis=-1, keepdims=True)
    yc = y - mu
    var = jnp.mean(yc * yc, axis=-1, keepdims=True)
    return yc * lax.rsqrt(var + LN_EPS) * g + b


def _outproj_kernel(*refs, n_in, alpha):
    a_refs, w_refs = refs[:n_in], refs[n_in:2 * n_in]
    x_ref, gate_ref, lng_ref, lnb_ref, sc_ref, sh_ref, wr_ref, br_ref, xo_ref, u_ref, lg_ref = refs[2 * n_in:]
    mix = _dot(a_refs[0][0], w_refs[0][...])
    for a_ref, w_ref in zip(a_refs[1:], w_refs[1:]):
        mix += _dot(a_ref[0], w_ref[...])
    xn = _layer_norm(alpha * x_ref[0] + gate_ref[0] * mix, lng_ref[...], lnb_ref[...])
    xo_ref[0] = xn
    u = xn * sc_ref[0] + sh_ref[0]
    u_ref[0] = u
    lg_ref[...] = lax.dot_general(wr_ref[...], u, (((1,), (1,)), ((), ())), preferred_element_type=F32,
                                  precision=HIGHEST) + br_ref[...]


def _outproj(parts, weights, x, gate1p, ln_g, ln_b, scale1p, shift, w_router_t, b_router, alpha):
    b, s, d = x.shape
    tm = min(256, s)
    n_in = len(parts)
    vec = pl.BlockSpec((1, 1, d), lambda i, j: (i, 0, 0))
    row = pl.BlockSpec((1, d), lambda i, j: (0, 0))
    in_specs = [pl.BlockSpec((1, tm, p.shape[-1]), lambda i, j: (i, j, 0)) for p in parts]
    in_specs += [pl.BlockSpec(w.shape, lambda i, j: (0, 0)) for w in weights]
    in_specs += [pl.BlockSpec((1, tm, d), lambda i, j: (i, j, 0)), vec, row, row, vec, vec,
                 pl.BlockSpec((ROUTER_ROWS, d), lambda i, j: (0, 0)),
                 pl.BlockSpec((ROUTER_ROWS, 1), lambda i, j: (0, 0))]
    nb = s // tm
    return pl.pallas_call(
        functools.partial(_outproj_kernel, n_in=n_in, alpha=alpha),
        grid=(b, nb),
        in_specs=in_specs,
        out_specs=[pl.BlockSpec((1, tm, d), lambda i, j: (i, j, 0)),
                   pl.BlockSpec((1, tm, d), lambda i, j: (i, j, 0)),
                   pl.BlockSpec((ROUTER_ROWS, tm), lambda i, j: (0, i * nb + j))],
        out_shape=[jax.ShapeDtypeStruct((b, s, d), F32), jax.ShapeDtypeStruct((b, s, d), F32),
                   jax.ShapeDtypeStruct((ROUTER_ROWS, b * s), F32)],
        compiler_params=_params("parallel", "parallel"),
        name="out_proj_norm",
    )(*parts, *[w.astype(MXU_DTYPE) for w in weights], x, gate1p, ln_g.reshape(1, d), ln_b.reshape(1, d),
      scale1p, shift, w_router_t, b_router)


def _route_kernel(lg_ref, ids_ref, gates_ref, cnt_ref, run_ref, *, tm):
    @pl.when(pl.program_id(0) == 0)
    def _():
        run_ref[...] = jnp.zeros_like(run_ref)

    lg = lg_ref[...]
    g0 = N_EXPERTS
    g_max = lg[g0:g0 + 1]
    grp = jnp.zeros((1, tm), I32)
    for i in range(1, N_GROUPS):
        gi = lg[g0 + i:g0 + i + 1]
        better = gi > g_max
        grp = jnp.where(better, i, grp)
        g_max = jnp.where(better, gi, g_max)
    den = jnp.exp(lg[g0:g0 + 1] - g_max)
    for i in range(1, N_GROUPS):
        den += jnp.exp(lg[g0 + i:g0 + i + 1] - g_max)
    p_grp = 1.0 / den

    cand = lg[0:EXPERTS_PER_GROUP]
    for g in range(1, N_GROUPS):
        cand = jnp.where(grp == g, lg[g * EXPERTS_PER_GROUP:(g + 1) * EXPERTS_PER_GROUP], cand)
    ridx = lax.broadcasted_iota(I32, (EXPERTS_PER_GROUP, tm), 0).astype(F32)
    none = float(EXPERTS_PER_GROUP)
    v1 = jnp.max(cand, axis=0, keepdims=True)
    i1 = jnp.min(jnp.where(cand == v1, ridx, none), axis=0, keepdims=True)
    rest = jnp.where(ridx == i1, -jnp.inf, cand)
    v2 = jnp.max(rest, axis=0, keepdims=True)
    i2 = jnp.min(jnp.where(rest == v2, ridx, none), axis=0, keepdims=True)
    e21 = jnp.exp(v2 - v1)
    gate1 = p_grp / (1.0 + e21)
    gate2 = p_grp * e21 / (1.0 + e21)
    ex1 = grp * EXPERTS_PER_GROUP + i1.astype(I32)
    ex2 = grp * EXPERTS_PER_GROUP + i2.astype(I32)

    eidx = lax.broadcasted_iota(I32, (N_EXPERTS, tm), 0)
    oh1 = jnp.where(eidx == ex1, 1.0, 0.0)
    oh2 = jnp.where(eidx == ex2, 1.0, 0.0)
    oh = (oh1 + oh2).astype(MXU_DTYPE)
    earlier = jnp.where(lax.broadcasted_iota(I32, (tm, tm), 0) < lax.broadcasted_iota(I32, (tm, tm), 1),
                        1.0, 0.0).astype(MXU_DTYPE)
    run = run_ref[...]
    before = _dot(oh, earlier) + jnp.concatenate([run] * (tm // LANES), axis=1)
    rank1 = jnp.sum(oh1 * before, axis=0, keepdims=True).astype(I32)
    rank2 = jnp.sum(oh2 * before, axis=0, keepdims=True).astype(I32)
    run = run + _dot(oh, jnp.ones((tm, LANES), MXU_DTYPE))
    run_ref[...] = run
    cnt_ref[...] = run
    ids_ref[...] = jnp.concatenate([ex1, ex2, rank1, rank2, jnp.zeros((4, tm), I32)], axis=0)
    gates_ref[...] = jnp.concatenate([gate1, gate2, jnp.zeros((6, tm), F32)], axis=0)


def _route(logits_t):
    t = logits_t.shape[1]
    tm = min(512, t)
    return pl.pallas_call(
        functools.partial(_route_kernel, tm=tm),
        grid=(t // tm,),
        in_specs=[pl.BlockSpec((ROUTER_ROWS, tm), lambda i: (0, i))],
        out_specs=[pl.BlockSpec((8, tm), lambda i: (0, i)), pl.BlockSpec((8, tm), lambda i: (0, i)),
                   pl.BlockSpec((N_EXPERTS, LANES), lambda i: (0, 0))],
        out_shape=[jax.ShapeDtypeStruct((8, t), I32), jax.ShapeDtypeStruct((8, t), F32),
                   jax.ShapeDtypeStruct((N_EXPERTS, LANES), F32)],
        scratch_shapes=[pltpu.VMEM((N_EXPERTS, LANES), F32)],
        compiler_params=_params("arbitrary"),
        name="route",
    )(logits_t)


def _gather_rows(idx_ref, src_hbm, dst, sem, n):
    def body(r, _):
        pltpu.make_async_copy(src_hbm.at[pl.ds(idx_ref[0, 0, r], 1), :], dst.at[pl.ds(r, 1), :], sem).start()
        return 0

    lax.fori_loop(0, n, body, 0)


def _wait_rows(src_hbm, dst, sem, n):
    pltpu.make_async_copy(src_hbm.at[pl.ds(0, n), :], dst, sem).wait()


def _expert_kernel(be_ref, nu_ref, idx_ref, idx_next_ref, x_hbm, w1_ref, w3_ref, w2_ref, y_ref, xbuf, sem, *, tm):
    i = pl.program_id(0)
    n_used = nu_ref[0]
    slot = i % 2

    @pl.when(i == 0)
    def _():
        _gather_rows(idx_ref, x_hbm, xbuf.at[0], sem.at[0], tm)

    @pl.when(i + 1 < n_used)
    def _():
        _gather_rows(idx_next_ref, x_hbm, xbuf.at[1 - slot], sem.at[1 - slot], tm)

    @pl.when(i < n_used)
    def _():
        _wait_rows(x_hbm, xbuf.at[slot], sem.at[slot], tm)
        xb = xbuf[slot].astype(MXU_DTYPE)
        hidden = _silu(_dot(xb, w1_ref[0])) * _dot(xb, w3_ref[0])
        y_ref[...] = _dot(hidden.astype(MXU_DTYPE), w2_ref[0])

    @pl.when(i >= n_used)
    def _():
        y_ref[...] = jnp.zeros_like(y_ref)


def _experts(u_flat, slot_tok, blk_expert, n_used, w1, w3, w2, tm):
    t, d = u_flat.shape
    n_blk = slot_tok.shape[0] // tm
    ff = w1.shape[-1]
    idx3 = slot_tok.reshape(n_blk, 1, tm)
    last = n_blk - 1
    grid_spec = pltpu.PrefetchScalarGridSpec(
        num_scalar_prefetch=2,
        grid=(n_blk,),
        in_specs=[
            pl.BlockSpec((1, 1, tm), lambda i, be, nu: (i, 0, 0), memory_space=pltpu.SMEM),
            pl.BlockSpec((1, 1, tm), lambda i, be, nu: (jnp.minimum(i + 1, last), 0, 0), memory_space=pltpu.SMEM),
            pl.BlockSpec(memory_space=pl.ANY),
            pl.BlockSpec((1, d, ff), lambda i, be, nu: (be[i], 0, 0)),
            pl.BlockSpec((1, d, ff), lambda i, be, nu: (be[i], 0, 0)),
            pl.BlockSpec((1, ff, d), lambda i, be, nu: (be[i], 0, 0)),
        ],
        out_specs=pl.BlockSpec((tm, d), lambda i, be, nu: (i, 0)),
        scratch_shapes=[pltpu.VMEM((2, tm, d), F32), pltpu.SemaphoreType.DMA((2,))],
    )
    return pl.pallas_call(
        functools.partial(_expert_kernel, tm=tm),
        grid_spec=grid_spec,
        out_shape=jax.ShapeDtypeStruct((n_blk * tm, d), F32),
        compiler_params=_params("arbitrary"),
        name="experts",
    )(blk_expert, n_used, idx3, idx3, u_flat, w1.astype(MXU_DTYPE), w3.astype(MXU_DTYPE), w2.astype(MXU_DTYPE))


def _combine_kernel(d1_ref, d2_ref, d1n_ref, d2n_ref, y_hbm, gates_ref, x_ref, gate_ref, lng_ref, lnb_ref,
                    o_ref, ybuf, sem, *, tm, alpha):
    i = pl.program_id(0)
    n = pl.num_programs(0)
    slot = i % 2

    def start(a_ref, b_ref, s):
        _gather_rows(a_ref, y_hbm, ybuf.at[s, 0], sem.at[s, 0], tm)
        _gather_rows(b_ref, y_hbm, ybuf.at[s, 1], sem.at[s, 1], tm)

    @pl.when(i == 0)
    def _():
        start(d1_ref, d2_ref, 0)

    @pl.when(i + 1 < n)
    def _():
        start(d1n_ref, d2n_ref, 1 - slot)

    _wait_rows(y_hbm, ybuf.at[slot, 0], sem.at[slot, 0], tm)
    _wait_rows(y_hbm, ybuf.at[slot, 1], sem.at[slot, 1], tm)
    gates = gates_ref[...]
    f = gates[:, 0:1] * ybuf[slot, 0] + gates[:, 1:2] * ybuf[slot, 1]
    o_ref[...] = _layer_norm(alpha * x_ref[...] + gate_ref[0] * f, lng_ref[...], lnb_ref[...])


def _combine(y_slots, dest1, dest2, gates, x_flat, gate1p, ln_g, ln_b, seq, alpha):
    t, d = x_flat.shape
    tm = min(256, seq)
    n_blk = t // tm
    per_seq = seq // tm
    d1 = dest1.reshape(n_blk, 1, tm)
    d2 = dest2.reshape(n_blk, 1, tm)
    cur = pl.BlockSpec((1, 1, tm), lambda i: (i, 0, 0), memory_space=pltpu.SMEM)
    nxt = pl.BlockSpec((1, 1, tm), lambda i: (jnp.minimum(i + 1, n_blk - 1), 0, 0), memory_space=pltpu.SMEM)
    row = pl.BlockSpec((1, d), lambda i: (0, 0))
    return pl.pallas_call(
        functools.partial(_combine_kernel, tm=tm, alpha=alpha),
        grid=(n_blk,),
        in_specs=[cur, cur, nxt, nxt, pl.BlockSpec(memory_space=pl.ANY),
                  pl.BlockSpec((tm, TOP_K), lambda i: (i, 0)),
                  pl.BlockSpec((tm, d), lambda i: (i, 0)),
                  pl.BlockSpec((1, 1, d), lambda i: (i // per_seq, 0, 0)), row, row],
        out_specs=pl.BlockSpec((tm, d), lambda i: (i, 0)),
        out_shape=jax.ShapeDtypeStruct((t, d), F32),
        scratch_shapes=[pltpu.VMEM((2, 2, tm, d), F32), pltpu.SemaphoreType.DMA((2, 2))],
        compiler_params=_params("arbitrary"),
        name="combine_norm",
    )(d1, d2, d1, d2, y_slots, gates, x_flat, gate1p, ln_g.reshape(1, d), ln_b.reshape(1, d))


EXPERT_TILE = 256


def _moe(u, logits_t, x, gate2p, ln_g, ln_b, w1, w3, w2, alpha):
    b, s, d = x.shape
    t = b * s
    ids, gates8, cnt = _route(logits_t)
    counts = cnt[:, 0].astype(I32)
    padded = (counts + EXPERT_TILE - 1) // EXPERT_TILE * EXPERT_TILE
    pad_end = jnp.cumsum(padded)
    pad_start = pad_end - padded
    dest1 = pad_start[ids[0]] + ids[2]
    dest2 = pad_start[ids[1]] + ids[3]
    n_slots = t * TOP_K + N_EXPERTS * EXPERT_TILE
    n_blk = n_slots // EXPERT_TILE
    tok = jnp.arange(t, dtype=I32)
    slot_tok = jnp.zeros((n_slots,), I32).at[dest1].set(tok).at[dest2].set(tok)
    blk_expert = jnp.minimum(
        jnp.searchsorted(pad_end, jnp.arange(n_blk, dtype=I32) * EXPERT_TILE, side='right'),
        N_EXPERTS - 1).astype(I32)
    n_used = (pad_end[-1:] // EXPERT_TILE).astype(I32)
    y_slots = _experts(u.reshape(t, d), slot_tok, blk_expert, n_used, w1, w3, w2, EXPERT_TILE)
    out = _combine(y_slots, dest1, dest2, gates8[:TOP_K].T, x.reshape(t, d), gate2p, ln_g, ln_b, s, alpha)
    return out.reshape(b, s, d)


def kernel(x, c, ln1_g, ln1_b, ln2_g, ln2_b, w_ada, b_ada, even_w_in, even_w_out, ret_gn_g, odd_w_in, odd_w_out, lambda_q1, lambda_k1, lambda_q2, lambda_k2, diff_subln_g, moe_w_group, moe_b_group, moe_w_router, moe_b_router, moe_w1, moe_w3, moe_w2):
    b, s, d = x.shape
    depth = w_ada.shape[0]
    alpha = (2.0 * depth) ** 0.25
    mod = _ada(c, w_ada, b_ada)
    for l in range(depth):
        sh1, sc1, g1, sh2, sc2, g2 = [m[:, None, :] for m in jnp.split(mod[l], 6, axis=-1)]
        i = l // 2
        if l % 2 == 0:
            w_in = even_w_in[i]
            w_main = jnp.concatenate([w_in[:, :2 * SB_WIDTH], w_in[:, 3 * SB_WIDTH:]], axis=1)
            proj, v_t = _inproj(x, 1.0 + sc1, sh1, w_main, w_in[:, 2 * SB_WIDTH:3 * SB_WIDTH])
            parts = [_sb_attention_old(proj, v_t), _retention(proj, ret_gn_g[i], 2 * SB_WIDTH // LANES)]
            w_out = even_w_out[i]
            weights = [w_out[:SB_WIDTH], w_out[SB_WIDTH:]]
        else:
            w_in = odd_w_in[i]
            proj, v_t = _inproj(x, 1.0 + sc1, sh1, w_in[:, :2 * DIFF_QK], w_in[:, 2 * DIFF_QK:])
            lambda_init = 0.8 - 0.6 * math.exp(-0.3 * l)
            lam_rows = jnp.stack([lambda_q1[i], lambda_k1[i], lambda_q2[i], lambda_k2[i]])
            parts = [_diff_attention(proj, v_t, lam_rows, diff_subln_g[i], lambda_init)]
            weights = [odd_w_out[i]]
        w_router_t = jnp.zeros((ROUTER_ROWS, d), F32).at[:N_EXPERTS].set(moe_w_router[l].T.astype(F32))
        w_router_t = w_router_t.at[N_EXPERTS:N_EXPERTS + N_GROUPS].set(moe_w_group[l].T.astype(F32))
        b_router = jnp.zeros((ROUTER_ROWS, 1), F32).at[:N_EXPERTS, 0].set(moe_b_router[l].astype(F32))
        b_router = b_router.at[N_EXPERTS:N_EXPERTS + N_GROUPS, 0].set(moe_b_group[l].astype(F32))
        x, u, logits_t = _outproj(parts, weights, x, 1.0 + g1, ln1_g[l], ln1_b[l], 1.0 + sc2, sh2,
                                  w_router_t, b_router, alpha)
        x = _moe(u, logits_t, x, 1.0 + g2, ln2_g[l], ln2_b[l], moe_w1[l], moe_w3[l], moe_w2[l], alpha)
    return x
```

```python
import functools
import math

import jax
import jax.numpy as jnp
import numpy as np
from jax import lax
from jax.experimental import pallas as pl
from jax.experimental.pallas import tpu as pltpu

F32 = jnp.float32
I32 = jnp.int32
MXU_DTYPE = jnp.bfloat16
HIGHEST = lax.Precision.HIGHEST
LOG2E = math.log2(math.e)

LN_EPS = 1e-5
CHUNK = 64
LANES = 128
SB_HEADS, SB_HEAD_DIM = 8, 64
RET_HEADS, RET_HEAD_DIM = 4, 128
DIFF_HEADS, DIFF_HEAD_DIM = 8, 64
SB_WIDTH = SB_HEADS * SB_HEAD_DIM
RET_WIDTH = RET_HEADS * RET_HEAD_DIM
DIFF_QK = DIFF_HEADS * 2 * DIFF_HEAD_DIM
N_GROUPS, EXPERTS_PER_GROUP = 4, 8
N_EXPERTS = N_GROUPS * EXPERTS_PER_GROUP
TOP_K = 2
ROUTER_ROWS = 40

VMEM_LIMIT = 56 * 1024 * 1024


def _params(*sem):
    return pltpu.CompilerParams(dimension_semantics=sem, vmem_limit_bytes=VMEM_LIMIT)


def _dot(a, b):
    return jnp.dot(a, b, preferred_element_type=F32)


def _dot_nt(a, b):
    return lax.dot_general(a, b, (((1,), (1,)), ((), ())), preferred_element_type=F32)


def _dot_tn(a, b):
    return lax.dot_general(a, b, (((0,), (0,)), ((), ())), preferred_element_type=F32)


def _silu(x):
    return x * (1.0 / (1.0 + jnp.exp(-x)))


def _ada_kernel(c_ref, w_ref, b_ref, o_ref):
    o_ref[0] = jnp.dot(_silu(c_ref[...]), w_ref[0], preferred_element_type=F32, precision=HIGHEST) + b_ref[0]


def _ada(c, w_ada, b_ada):
    depth, d, n = w_ada.shape
    bp = 8
    cp = jnp.zeros((bp, d), F32).at[: c.shape[0]].set(c)
    tn = 1536
    out = pl.pallas_call(
        _ada_kernel,
        grid=(depth, n // tn),
        in_specs=[
            pl.BlockSpec((bp, d), lambda l, j: (0, 0)),
            pl.BlockSpec((1, d, tn), lambda l, j: (l, 0, j)),
            pl.BlockSpec((1, 1, tn), lambda l, j: (l, 0, j)),
        ],
        out_specs=pl.BlockSpec((1, bp, tn), lambda l, j: (l, 0, j)),
        out_shape=jax.ShapeDtypeStruct((depth, bp, n), F32),
        compiler_params=_params("parallel", "parallel"),
        name="ada_mod",
    )(cp, w_ada, b_ada.reshape(depth, 1, n))
    return out[:, : c.shape[0]]


def _inproj_kernel(x_ref, sc_ref, sh_ref, w_ref, wvt_ref, o_ref, vt_ref, *, tn):
    u = (x_ref[0] * sc_ref[0] + sh_ref[0]).astype(MXU_DTYPE)
    for j in range(o_ref.shape[2] // tn):
        o_ref[0, :, j * tn:(j + 1) * tn] = _dot(u, w_ref[:, j * tn:(j + 1) * tn]).astype(o_ref.dtype)
    for j in range(vt_ref.shape[1] // tn):
        vt_ref[0, j * tn:(j + 1) * tn, :] = _dot_nt(wvt_ref[j * tn:(j + 1) * tn, :], u).astype(vt_ref.dtype)


def _inproj(x, scale1p, shift, w, w_v):
    b, s, d = x.shape
    n, n_v = w.shape[1], w_v.shape[1]
    tm = min(512, s)
    return pl.pallas_call(
        functools.partial(_inproj_kernel, tn=512),
        grid=(b, s // tm),
        in_specs=[
            pl.BlockSpec((1, tm, d), lambda i, j: (i, j, 0)),
            pl.BlockSpec((1, 1, d), lambda i, j: (i, 0, 0)),
            pl.BlockSpec((1, 1, d), lambda i, j: (i, 0, 0)),
            pl.BlockSpec((d, n), lambda i, j: (0, 0)),
            pl.BlockSpec((n_v, d), lambda i, j: (0, 0)),
        ],
        out_specs=[pl.BlockSpec((1, tm, n), lambda i, j: (i, j, 0)),
                   pl.BlockSpec((1, n_v, tm), lambda i, j: (i, 0, j))],
        out_shape=[jax.ShapeDtypeStruct((b, s, n), MXU_DTYPE), jax.ShapeDtypeStruct((b, n_v, s), MXU_DTYPE)],
        compiler_params=_params("parallel", "parallel"),
        name="in_proj",
    )(x, scale1p, shift, w.astype(MXU_DTYPE), w_v.T.astype(MXU_DTYPE))


MASKED = -float("inf")
ITEM_PLAIN, ITEM_DIAGONAL, ITEM_NULL = 0, 1, 2
FLAG_FIRST, FLAG_LAST = 1, 2


def _triangle_items(n_q, pad, diagonal_first):
    items = []
    for qb in range(n_q):
        order = range(qb, -1, -1) if diagonal_first else range(qb + 1)
        for n, kb in enumerate(order):
            flags = (FLAG_FIRST if n == 0 else 0) | (FLAG_LAST if n == qb else 0)
            items.append((qb, kb, ITEM_DIAGONAL if kb == qb else ITEM_PLAIN, flags))
    null = (0, 0, ITEM_NULL, 0)
    return np.asarray([null] * pad + items + [null] * (pad + 1), np.int32).T.copy(), len(items)


SB_STAGES = 3


def _sb_kernel(tab_ref, q_ref, k_ref, vt_ref, o_ref, qh_s, acc_ref, carry_ref, mask_ref, y_s, yms_s, sums_s,
               *, tq, n_items):
    tk = tq
    extra = 16
    heads = range(2)
    lane = lax.broadcasted_iota(I32, (1, LANES), 1)
    key = lax.broadcasted_iota(I32, (tk, tq), 0)
    qry = lax.broadcasted_iota(I32, (tk, tq), 1)
    r = lax.broadcasted_iota(I32, (tk + extra, tk), 0)
    c = lax.broadcasted_iota(I32, (tk + extra, tk), 1)
    neg_later = jnp.where(r >= tk, -1.0, jnp.where(c > r, -1.0, 0.0)).astype(MXU_DTYPE)
    scale2 = SB_HEAD_DIM ** -0.5 * LOG2E

    def prepare_queries(blk, _):
        rows = pl.ds(pl.multiple_of(blk * tq, tq), tq)
        q2 = q_ref[0, rows, :]
        for h in heads:
            qh_s[h, rows, :] = (jnp.where((lane // SB_HEAD_DIM) == h, q2, jnp.zeros_like(q2)).astype(F32)
                                * scale2).astype(MXU_DTYPE)
        return 0

    lax.fori_loop(0, q_ref.shape[1] // tq, prepare_queries, 0)
    acc_ref[...] = jnp.zeros_like(acc_ref)
    carry_ref[...] = jnp.zeros_like(carry_ref)
    mask_ref[ITEM_PLAIN] = jnp.zeros((tk, tq), F32)
    mask_ref[ITEM_DIAGONAL] = jnp.where(key < qry, 0.0, MASKED)
    mask_ref[ITEM_NULL] = jnp.full((tk, tq), MASKED, F32)
    y_s[...] = jnp.zeros_like(y_s)
    yms_s[...] = jnp.full(yms_s.shape, MASKED, F32)
    sums_s[...] = jnp.zeros_like(sums_s)

    def rows_of(block, size):
        return pl.ds(pl.multiple_of(block * size, size), size)

    def trip(it, parity):
        col_x, col_y, col_c = it + 2, it + 1, it
        kj = k_ref[0, rows_of(tab_ref[1, col_x], tk), :]
        qrows = rows_of(tab_ref[0, col_x], tq)
        for h in heads:
            y_s[h, parity] = _dot_nt(kj, qh_s[h, qrows, :])
        mask = mask_ref[tab_ref[2, col_y]]
        for h in heads:
            ym = y_s[h, 1 - parity] + mask
            sp = jnp.maximum(ym, jnp.log2(1.0 + jnp.exp2(jnp.minimum(ym, 126.0))))
            yms_s[h, parity] = ym - sp
            sums_s[h, parity] = _dot(neg_later, sp.astype(MXU_DTYPE))
        vtj = vt_ref[0, :, rows_of(tab_ref[1, col_c], tk)]
        keep = jnp.where((tab_ref[3, col_c] & FLAG_FIRST) != 0, 0.0, 1.0)
        for h in heads:
            sums = sums_s[h, 1 - parity]
            carry = carry_ref[h] * keep
            w = jnp.exp2(yms_s[h, 1 - parity] + sums[:tk] + carry).astype(MXU_DTYPE)
            acc_ref[h] = acc_ref[h] * keep + _dot(vtj, w)
            carry_ref[h] = carry + sums[tk:tk + 1]

        @pl.when((tab_ref[3, col_c] & FLAG_LAST) != 0)
        def _():
            sub = lax.broadcasted_iota(I32, (LANES, 1), 0)
            o_ref[0, rows_of(tab_ref[0, col_c], tq), :] = jnp.where(
                sub < SB_HEAD_DIM, acc_ref[0], acc_ref[1]).T.astype(o_ref.dtype)

    def trip_pair(i, _):
        trip(2 * i, 0)
        trip(2 * i + 1, 1)
        return 0

    lax.fori_loop(0, pl.cdiv(n_items + SB_STAGES - 1, 2), trip_pair, 0)


def _sb_attention(proj, v_t):
    b, s, _ = proj.shape
    tq = min(256, s)
    n_pairs = SB_WIDTH // LANES
    table, n_items = _triangle_items(s // tq, SB_STAGES - 1, diagonal_first=True)
    grid_spec = pltpu.PrefetchScalarGridSpec(
        num_scalar_prefetch=1,
        grid=(b, n_pairs),
        in_specs=[
            pl.BlockSpec((1, s, LANES), lambda i, p, t: (i, 0, p)),
            pl.BlockSpec((1, s, LANES), lambda i, p, t: (i, 0, n_pairs + p)),
            pl.BlockSpec((1, LANES, s), lambda i, p, t: (i, p, 0)),
        ],
        out_specs=pl.BlockSpec((1, s, LANES), lambda i, p, t: (i, 0, p)),
        scratch_shapes=[
            pltpu.VMEM((2, s, LANES), MXU_DTYPE),
            pltpu.VMEM((2, LANES, tq), F32),
            pltpu.VMEM((2, 1, tq), F32),
            pltpu.VMEM((3, tq, tq), F32),
            pltpu.VMEM((2, 2, tq, tq), F32),
            pltpu.VMEM((2, 2, tq, tq), F32),
            pltpu.VMEM((2, 2, tq + 16, tq), F32),
        ],
    )
    return pl.pallas_call(
        functools.partial(_sb_kernel, tq=tq, n_items=n_items),
        grid_spec=grid_spec,
        out_shape=jax.ShapeDtypeStruct((b, s, SB_WIDTH), MXU_DTYPE),
        compiler_params=_params("parallel", "parallel"),
        name="sb_attention",
    )(jnp.asarray(table), proj, proj, v_t)


def _sb_kernel_old(q_ref, k_ref, vt_ref, o_ref, acc_ref, carry_ref, *, tq):
    qi = pl.program_id(2)
    tk = tq
    extra = 16
    lane = lax.broadcasted_iota(I32, (1, LANES), 1)
    key = lax.broadcasted_iota(I32, (tk, tq), 0)
    qry = lax.broadcasted_iota(I32, (tk, tq), 1)
    causal = key < qry
    r = lax.broadcasted_iota(I32, (tk + extra, tk), 0)
    c = lax.broadcasted_iota(I32, (tk + extra, tk), 1)
    neg_later = jnp.where(r >= tk, -1.0, jnp.where(c > r, -1.0, 0.0)).astype(MXU_DTYPE)
    q2 = q_ref[0]
    scale2 = SB_HEAD_DIM ** -0.5 * LOG2E

    qh = [(jnp.where((lane // SB_HEAD_DIM) == h, q2, jnp.zeros_like(q2)).astype(F32) * scale2).astype(MXU_DTYPE)
          for h in range(2)]
    acc_ref[...] = jnp.zeros_like(acc_ref)
    carry_ref[...] = jnp.zeros_like(carry_ref)

    def key_block(j, diagonal):
        start = pl.multiple_of(j * tk, tk)
        kj = k_ref[0, pl.ds(start, tk), :]
        vtj = vt_ref[0, :, pl.ds(start, tk)]
        for h in range(2):
            y = _dot_nt(kj, qh[h])
            sp = jnp.maximum(y, 0.0) + jnp.log2(1.0 + jnp.exp2(-jnp.abs(y)))
            if diagonal:
                sp = jnp.where(causal, sp, 0.0)
            sums = _dot(neg_later, sp.astype(MXU_DTYPE))
            carry = carry_ref[h]
            w = jnp.exp2(y - sp + sums[:tk] + carry)
            if diagonal:
                w = jnp.where(causal, w, 0.0)
            acc_ref[h] += _dot(vtj, w.astype(MXU_DTYPE))
            carry_ref[h] = carry + sums[tk:tk + 1]

    key_block(qi, True)

    def body(it, _):
        key_block(qi - 1 - it, False)
        return 0

    lax.fori_loop(0, qi, body, 0)
    sub = lax.broadcasted_iota(I32, (LANES, 1), 0)
    o_ref[0] = jnp.where(sub < SB_HEAD_DIM, acc_ref[0], acc_ref[1]).T.astype(o_ref.dtype)


def _sb_attention_old(proj, v_t):
    b, s, _ = proj.shape
    tq = min(256, s)
    n_pairs = SB_WIDTH // LANES
    return pl.pallas_call(
        functools.partial(_sb_kernel_old, tq=tq),
        grid=(b, n_pairs, s // tq),
        in_specs=[
            pl.BlockSpec((1, tq, LANES), lambda i, p, j: (i, j, p)),
            pl.BlockSpec((1, s, LANES), lambda i, p, j: (i, 0, n_pairs + p)),
            pl.BlockSpec((1, LANES, s), lambda i, p, j: (i, p, 0)),
        ],
        out_specs=pl.BlockSpec((1, tq, LANES), lambda i, p, j: (i, j, p)),
        out_shape=jax.ShapeDtypeStruct((b, s, SB_WIDTH), MXU_DTYPE),
        scratch_shapes=[pltpu.VMEM((2, LANES, tq), F32), pltpu.VMEM((2, 1, tq), F32)],
        compiler_params=_params("parallel", "parallel", "parallel"),
        name="sb_attention_old",
    )(proj, proj, v_t)


def _ret_kernel(lg_ref, q_ref, k_ref, v_ref, g_ref, gn_ref, o_ref, state_ref, *, tb):
    h = pl.program_id(1)
    blk = pl.program_id(2)

    @pl.when(blk == 0)
    def _():
        state_ref[...] = jnp.zeros_like(state_ref)

    lg = lg_ref[h]
    scale = RET_HEAD_DIM ** -0.5
    q = q_ref[0].astype(F32)
    k = k_ref[0].astype(F32)
    v = v_ref[0]
    row = lax.broadcasted_iota(I32, (tb, tb), 0)
    col = lax.broadcasted_iota(I32, (tb, tb), 1)
    dist = jnp.abs(row - col).astype(F32)
    decay = jnp.where(col // CHUNK <= row // CHUNK, jnp.exp(lg * dist) * scale, 0.0)
    pos = lax.broadcasted_iota(I32, (tb, 1), 0).astype(F32)
    scores = _dot_nt(q.astype(MXU_DTYPE), k.astype(MXU_DTYPE)) * decay
    intra = _dot(scores.astype(MXU_DTYPE), v)
    state = state_ref[...]
    q_in = (q * jnp.exp(lg * (pos + 1.0))).astype(MXU_DTYPE)
    inter = _dot(q_in, state.astype(MXU_DTYPE))
    k_out = (k * (jnp.exp(lg * (tb - 1.0 - pos)) * scale)).astype(MXU_DTYPE)
    block_decay = jnp.exp(lg * jnp.full((1, RET_HEAD_DIM), float(tb), F32))
    state_ref[...] = block_decay * state + _dot_tn(k_out, v)
    o = intra + inter
    mu = jnp.mean(o, axis=-1, keepdims=True)
    oc = o - mu
    var = jnp.mean(oc * oc, axis=-1, keepdims=True)
    o = oc * lax.rsqrt(var + LN_EPS) * gn_ref[0] * _silu(g_ref[0].astype(F32))
    o_ref[0] = o.astype(o_ref.dtype)


def _retention(proj, gn_gain, first_col_block):
    b, s, _ = proj.shape
    tb = min(256, s)
    log_gamma = jnp.log1p(-jnp.exp2(-5.0 - jnp.arange(RET_HEADS, dtype=F32)))

    def col(which):
        return lambda i, h, j, lg: (i, j, first_col_block + which * RET_HEADS + h)

    grid_spec = pltpu.PrefetchScalarGridSpec(
        num_scalar_prefetch=1,
        grid=(b, RET_HEADS, s // tb),
        in_specs=[pl.BlockSpec((1, tb, LANES), col(w)) for w in range(4)]
        + [pl.BlockSpec((1, 1, LANES), lambda i, h, j, lg: (h, 0, 0))],
        out_specs=pl.BlockSpec((1, tb, LANES), lambda i, h, j, lg: (i, j, h)),
        scratch_shapes=[pltpu.VMEM((RET_HEAD_DIM, RET_HEAD_DIM), F32)],
    )
    return pl.pallas_call(
        functools.partial(_ret_kernel, tb=tb),
        grid_spec=grid_spec,
        out_shape=jax.ShapeDtypeStruct((b, s, RET_WIDTH), MXU_DTYPE),
        compiler_params=_params("parallel", "parallel", "arbitrary"),
        name="retention",
    )(log_gamma, proj, proj, proj, proj, gn_gain.astype(F32).reshape(RET_HEADS, 1, RET_HEAD_DIM))


DIFF_STAGES = 3
DIFF_HEADS_PER_STEP = 2


def _diff_kernel(slope_ref, tab_ref, q_ref, k_ref, vt_ref, lam_ref, gain_ref, o_ref, qs_s, m_ref, l_ref, acc_ref,
                 bias_ref, z_s, p_s, a_s, lfin_s, *, tq, n_items, lambda_init):
    hp = pl.program_id(1)
    tk = tq
    nh = DIFF_HEADS_PER_STEP
    dv = LANES
    heads = range(nh)
    lane = lax.broadcasted_iota(I32, (1, LANES), 1)
    scale2 = DIFF_HEAD_DIM ** -0.5 * LOG2E
    slope2 = [slope_ref[hp * nh + hh] * LOG2E for hh in heads]

    def rows_of(block, size):
        return pl.ds(pl.multiple_of(block * size, size), size)

    def prepare_queries(blk, _):
        for hh in heads:
            q2 = q_ref[0, rows_of(blk, tq), hh * LANES:(hh + 1) * LANES]
            zero = jnp.zeros_like(q2)
            stacked = jnp.concatenate([jnp.where(lane < DIFF_HEAD_DIM, q2, zero),
                                       jnp.where(lane >= DIFF_HEAD_DIM, q2, zero)], axis=0)
            qs_s[hh, rows_of(blk, 2 * tq), :] = (stacked.astype(F32) * scale2).astype(MXU_DTYPE)
        return 0

    lax.fori_loop(0, q_ref.shape[1] // tq, prepare_queries, 0)
    key = lax.broadcasted_iota(I32, (tk, tq), 0)
    qry = lax.broadcasted_iota(I32, (tk, tq), 1)
    visible = key // CHUNK <= qry // CHUNK
    for hh in heads:
        plain = slope2[hh] * key.astype(F32)
        diag = jnp.where(visible, slope2[hh] * (qry - jnp.abs(qry - key)).astype(F32), MASKED)
        bias_ref[hh, ITEM_PLAIN] = jnp.concatenate([plain, plain], axis=1)
        bias_ref[hh, ITEM_DIAGONAL] = jnp.concatenate([diag, diag], axis=1)
    m_ref[...] = jnp.zeros_like(m_ref)
    l_ref[...] = jnp.zeros_like(l_ref)
    acc_ref[...] = jnp.zeros_like(acc_ref)
    z_s[...] = jnp.zeros_like(z_s)
    p_s[...] = jnp.zeros_like(p_s)
    a_s[...] = jnp.ones_like(a_s)
    lfin_s[...] = jnp.ones_like(lfin_s)

    def trip(it, parity):
        col_a, col_a1, col_c = it + 2, it + 1, it
        krows = rows_of(tab_ref[1, col_a], tk)
        qrows = rows_of(tab_ref[0, col_a], 2 * tq)
        for hh in heads:
            z_s[hh, parity] = _dot_nt(k_ref[0, krows, hh * LANES:(hh + 1) * LANES], qs_s[hh, qrows, :])
        vrows = rows_of(tab_ref[1, col_c], tk)
        for hh in heads:
            acc_ref[hh] = a_s[hh] * acc_ref[hh] + _dot(vt_ref[0, hh * dv:(hh + 1) * dv, vrows], p_s[hh])
        lfin = [lfin_s[hh] for hh in heads]
        kind = tab_ref[2, col_a1]
        first = (tab_ref[3, col_a1] & FLAG_FIRST) != 0
        offset = ((tab_ref[1, col_a1] - tab_ref[0, col_a1]) * tq).astype(F32)
        for hh in heads:
            shift = jnp.where(kind == ITEM_PLAIN, slope2[hh] * offset, jnp.where(kind == ITEM_DIAGONAL, 0.0, MASKED))
            z = z_s[hh, 1 - parity] + bias_ref[hh, jnp.minimum(kind, ITEM_DIAGONAL)]
            m_old = jnp.where(first, MASKED, m_ref[hh])
            m_new = jnp.maximum(m_old, jnp.max(z, axis=0, keepdims=True) + shift)
            p = jnp.exp2(z - (m_new - shift))
            a = jnp.exp2(m_old - m_new)
            l_new = a * l_ref[hh] + jnp.sum(p, axis=0, keepdims=True)
            p_s[hh] = p.astype(MXU_DTYPE)
            a_s[hh] = a
            lfin_s[hh] = l_new
            l_ref[hh] = l_new
            m_ref[hh] = m_new

        @pl.when((tab_ref[3, col_c] & FLAG_LAST) != 0)
        def _():
            lam_v = lam_ref[...]
            lam = (jnp.exp(jnp.sum(lam_v[0:1] * lam_v[1:2], axis=-1, keepdims=True))
                   - jnp.exp(jnp.sum(lam_v[2:3] * lam_v[3:4], axis=-1, keepdims=True)) + lambda_init)
            orows = rows_of(tab_ref[0, col_c], tq)
            for hh in heads:
                o = acc_ref[hh] * (1.0 / lfin[hh])
                o = o[:, :tq] - lam * o[:, tq:]
                o = o * lax.rsqrt(jnp.mean(o * o, axis=0, keepdims=True) + LN_EPS)
                o_ref[0, orows, hh * dv:(hh + 1) * dv] = (o * gain_ref[...] * (1.0 - lambda_init)).T.astype(
                    o_ref.dtype)

    def trip_pair(i, _):
        trip(2 * i, 0)
        trip(2 * i + 1, 1)
        return 0

    lax.fori_loop(0, pl.cdiv(n_items + DIFF_STAGES - 1, 2), trip_pair, 0)


def _diff_attention(proj, v_t, lam_rows, subln_gain, lambda_init):
    b, s, _ = proj.shape
    tq = min(256, s)
    dv = subln_gain.shape[-1]
    nh = DIFF_HEADS_PER_STEP
    slopes = jnp.exp2(-8.0 / DIFF_HEADS * (jnp.arange(DIFF_HEADS, dtype=F32) + 1.0))
    kb = DIFF_QK // (nh * LANES)
    table, n_items = _triangle_items(s // tq, DIFF_STAGES - 1, diagonal_first=False)
    grid_spec = pltpu.PrefetchScalarGridSpec(
        num_scalar_prefetch=2,
        grid=(b, DIFF_HEADS // nh),
        in_specs=[
            pl.BlockSpec((1, s, nh * LANES), lambda i, h, sl, t: (i, 0, h)),
            pl.BlockSpec((1, s, nh * LANES), lambda i, h, sl, t: (i, 0, kb + h)),
            pl.BlockSpec((1, nh * dv, s), lambda i, h, sl, t: (i, h, 0)),
            pl.BlockSpec((4, DIFF_HEAD_DIM), lambda i, h, sl, t: (0, 0)),
            pl.BlockSpec((dv, 1), lambda i, h, sl, t: (0, 0)),
        ],
        out_specs=pl.BlockSpec((1, s, nh * dv), lambda i, h, sl, t: (i, 0, h)),
        scratch_shapes=[
            pltpu.VMEM((nh, 2 * s, LANES), MXU_DTYPE),
            pltpu.VMEM((nh, 1, 2 * tq), F32),
            pltpu.VMEM((nh, 1, 2 * tq), F32),
            pltpu.VMEM((nh, dv, 2 * tq), F32),
            pltpu.VMEM((nh, 2, tq, 2 * tq), F32),
            pltpu.VMEM((nh, 2, tq, 2 * tq), F32),
            pltpu.VMEM((nh, tq, 2 * tq), MXU_DTYPE),
            pltpu.VMEM((nh, 1, 2 * tq), F32),
            pltpu.VMEM((nh, 1, 2 * tq), F32),
        ],
    )
    return pl.pallas_call(
        functools.partial(_diff_kernel, tq=tq, n_items=n_items, lambda_init=lambda_init),
        grid_spec=grid_spec,
        out_shape=jax.ShapeDtypeStruct((b, s, DIFF_HEADS * dv), MXU_DTYPE),
        compiler_params=_params("parallel", "parallel"),
        name="diff_attention",
    )(slopes, jnp.asarray(table), proj, proj, v_t, lam_rows.astype(F32), subln_gain.astype(F32).reshape(dv, 1))


def _diff_kernel_old(slope_ref, q_ref, k_ref, vt_ref, lam_ref, gain_ref, o_ref, m_ref, l_ref, acc_ref, base_ref,
                     *, tq, lambda_init):
    hp = pl.program_id(1)
    qi = pl.program_id(2)
    tk = tq
    nh = DIFF_HEADS_PER_STEP
    dv = LANES
    lane = lax.broadcasted_iota(I32, (1, LANES), 1)
    scale2 = DIFF_HEAD_DIM ** -0.5 * LOG2E
    slope2, qs = [], []
    for hh in range(nh):
        slope2.append(slope_ref[hp * nh + hh] * LOG2E)
        q2 = q_ref[0, :, hh * LANES:(hh + 1) * LANES]
        zero = jnp.zeros_like(q2)
        stacked = jnp.concatenate([jnp.where(lane < DIFF_HEAD_DIM, q2, zero),
                                   jnp.where(lane >= DIFF_HEAD_DIM, q2, zero)], axis=0)
        qs.append((stacked.astype(F32) * scale2).astype(MXU_DTYPE))
        base_ref[hh] = slope2[hh] * lax.broadcasted_iota(I32, (tk, LANES), 0).astype(F32)
    m_ref[...] = jnp.full(m_ref.shape, -jnp.inf, F32)
    l_ref[...] = jnp.zeros_like(l_ref)
    acc_ref[...] = jnp.zeros_like(acc_ref)

    def key_block(j, diagonal):
        start = pl.multiple_of(j * tk, tk)
        for hh in range(nh):
            kj = k_ref[0, pl.ds(start, tk), hh * LANES:(hh + 1) * LANES]
            vtj = vt_ref[0, hh * dv:(hh + 1) * dv, pl.ds(start, tk)]
            z = _dot_nt(kj, qs[hh])
            if diagonal:
                key = lax.broadcasted_iota(I32, (tk, tq), 0)
                qry = lax.broadcasted_iota(I32, (tk, tq), 1)
                visible = key // CHUNK <= qry // CHUNK
                bias_d = slope2[hh] * (qry - jnp.abs(qry - key)).astype(F32)
                z = z + jnp.concatenate([bias_d, bias_d], axis=1)
                z = jnp.where(jnp.concatenate([visible, visible], axis=1), z, -jnp.inf)
                shift = 0.0
            else:
                z = z + jnp.concatenate([base_ref[hh]] * (2 * tq // LANES), axis=1)
                shift = slope2[hh] * ((j - qi) * tq).astype(F32)
            m_old = m_ref[hh]
            m_new = jnp.maximum(m_old, jnp.max(z, axis=0, keepdims=True) + shift)
            p = jnp.exp2(z - (m_new - shift))
            a = jnp.exp2(m_old - m_new)
            l_ref[hh] = a * l_ref[hh] + jnp.sum(p, axis=0, keepdims=True)
            acc_ref[hh] = a * acc_ref[hh] + _dot(vtj, p.astype(MXU_DTYPE))
            m_ref[hh] = m_new

    key_block(qi, True)

    def body(j, _):
        key_block(j, False)
        return 0

    lax.fori_loop(0, qi, body, 0)

    lam_v = lam_ref[...]
    lam = (jnp.exp(jnp.sum(lam_v[0:1] * lam_v[1:2], axis=-1, keepdims=True))
           - jnp.exp(jnp.sum(lam_v[2:3] * lam_v[3:4], axis=-1, keepdims=True)) + lambda_init)
    for hh in range(nh):
        o = acc_ref[hh] * (1.0 / l_ref[hh])
        o = o[:, :tq] - lam * o[:, tq:]
        o = o * lax.rsqrt(jnp.mean(o * o, axis=0, keepdims=True) + LN_EPS)
        o_ref[0, :, hh * dv:(hh + 1) * dv] = (o * gain_ref[...] * (1.0 - lambda_init)).T.astype(o_ref.dtype)


def _diff_attention_old(proj, v_t, lam_rows, subln_gain, lambda_init):
    b, s, _ = proj.shape
    tq = min(256, s)
    dv = subln_gain.shape[-1]
    nh = DIFF_HEADS_PER_STEP
    slopes = jnp.exp2(-8.0 / DIFF_HEADS * (jnp.arange(DIFF_HEADS, dtype=F32) + 1.0))
    kb = DIFF_QK // (nh * LANES)
    grid_spec = pltpu.PrefetchScalarGridSpec(
        num_scalar_prefetch=1,
        grid=(b, DIFF_HEADS // nh, s // tq),
        in_specs=[
            pl.BlockSpec((1, tq, nh * LANES), lambda i, h, j, sl: (i, j, h)),
            pl.BlockSpec((1, s, nh * LANES), lambda i, h, j, sl: (i, 0, kb + h)),
            pl.BlockSpec((1, nh * dv, s), lambda i, h, j, sl: (i, h, 0)),
            pl.BlockSpec((4, DIFF_HEAD_DIM), lambda i, h, j, sl: (0, 0)),
            pl.BlockSpec((dv, 1), lambda i, h, j, sl: (0, 0)),
        ],
        out_specs=pl.BlockSpec((1, tq, nh * dv), lambda i, h, j, sl: (i, j, h)),
        scratch_shapes=[pltpu.VMEM((nh, 1, 2 * tq), F32), pltpu.VMEM((nh, 1, 2 * tq), F32),
                        pltpu.VMEM((nh, dv, 2 * tq), F32), pltpu.VMEM((nh, tq, LANES), F32)],
    )
    return pl.pallas_call(
        functools.partial(_diff_kernel_old, tq=tq, lambda_init=lambda_init),
        grid_spec=grid_spec,
        out_shape=jax.ShapeDtypeStruct((b, s, DIFF_HEADS * dv), MXU_DTYPE),
        compiler_params=_params("parallel", "parallel", "parallel"),
        name="diff_attention_old",
    )(slopes, proj, proj, v_t, lam_rows.astype(F32), subln_gain.astype(F32).reshape(dv, 1))


def _layer_norm(y, g, b):
    mu = jnp.mean(y, axis=-1, keepdims=True)
    yc = y - mu
    var = jnp.mean(yc * yc, axis=-1, keepdims=True)
    return yc * lax.rsqrt(var + LN_EPS) * g + b


def _outproj_kernel(*refs, n_in, alpha):
    a_refs, w_refs = refs[:n_in], refs[n_in:2 * n_in]
    x_ref, gate_ref, lng_ref, lnb_ref, sc_ref, sh_ref, wr_ref, br_ref, xo_ref, u_ref, lg_ref = refs[2 * n_in:]
    mix = _dot(a_refs[0][0], w_refs[0][...])
    for a_ref, w_ref in zip(a_refs[1:], w_refs[1:]):
        mix += _dot(a_ref[0], w_ref[...])
    xn = _layer_norm(alpha * x_ref[0] + gate_ref[0] * mix, lng_ref[...], lnb_ref[...])
    xo_ref[0] = xn
    u = xn * sc_ref[0] + sh_ref[0]
    u_ref[0] = u
    lg_ref[...] = lax.dot_general(wr_ref[...], u, (((1,), (1,)), ((), ())), preferred_element_type=F32,
                                  precision=HIGHEST) + br_ref[...]


def _outproj(parts, weights, x, gate1p, ln_g, ln_b, scale1p, shift, w_router_t, b_router, alpha):
    b, s, d = x.shape
    tm = min(256, s)
    n_in = len(parts)
    vec = pl.BlockSpec((1, 1, d), lambda i, j: (i, 0, 0))
    row = pl.BlockSpec((1, d), lambda i, j: (0, 0))
    in_specs = [pl.BlockSpec((1, tm, p.shape[-1]), lambda i, j: (i, j, 0)) for p in parts]
    in_specs += [pl.BlockSpec(w.shape, lambda i, j: (0, 0)) for w in weights]
    in_specs += [pl.BlockSpec((1, tm, d), lambda i, j: (i, j, 0)), vec, row, row, vec, vec,
                 pl.BlockSpec((ROUTER_ROWS, d), lambda i, j: (0, 0)),
                 pl.BlockSpec((ROUTER_ROWS, 1), lambda i, j: (0, 0))]
    nb = s // tm
    return pl.pallas_call(
        functools.partial(_outproj_kernel, n_in=n_in, alpha=alpha),
        grid=(b, nb),
        in_specs=in_specs,
        out_specs=[pl.BlockSpec((1, tm, d), lambda i, j: (i, j, 0)),
                   pl.BlockSpec((1, tm, d), lambda i, j: (i, j, 0)),
                   pl.BlockSpec((ROUTER_ROWS, tm), lambda i, j: (0, i * nb + j))],
        out_shape=[jax.ShapeDtypeStruct((b, s, d), F32), jax.ShapeDtypeStruct((b, s, d), F32),
                   jax.ShapeDtypeStruct((ROUTER_ROWS, b * s), F32)],
        compiler_params=_params("parallel", "parallel"),
        name="out_proj_norm",
    )(*parts, *[w.astype(MXU_DTYPE) for w in weights], x, gate1p, ln_g.reshape(1, d), ln_b.reshape(1, d),
      scale1p, shift, w_router_t, b_router)


def _route_kernel(lg_ref, ids_ref, gates_ref, cnt_ref, run_ref, *, tm):
    @pl.when(pl.program_id(0) == 0)
    def _():
        run_ref[...] = jnp.zeros_like(run_ref)

    lg = lg_ref[...]
    g0 = N_EXPERTS
    g_max = lg[g0:g0 + 1]
    grp = jnp.zeros((1, tm), I32)
    for i in range(1, N_GROUPS):
        gi = lg[g0 + i:g0 + i + 1]
        better = gi > g_max
        grp = jnp.where(better, i, grp)
        g_max = jnp.where(better, gi, g_max)
    den = jnp.exp(lg[g0:g0 + 1] - g_max)
    for i in range(1, N_GROUPS):
        den += jnp.exp(lg[g0 + i:g0 + i + 1] - g_max)
    p_grp = 1.0 / den

    cand = lg[0:EXPERTS_PER_GROUP]
    for g in range(1, N_GROUPS):
        cand = jnp.where(grp == g, lg[g * EXPERTS_PER_GROUP:(g + 1) * EXPERTS_PER_GROUP], cand)
    ridx = lax.broadcasted_iota(I32, (EXPERTS_PER_GROUP, tm), 0).astype(F32)
    none = float(EXPERTS_PER_GROUP)
    v1 = jnp.max(cand, axis=0, keepdims=True)
    i1 = jnp.min(jnp.where(cand == v1, ridx, none), axis=0, keepdims=True)
    rest = jnp.where(ridx == i1, -jnp.inf, cand)
    v2 = jnp.max(rest, axis=0, keepdims=True)
    i2 = jnp.min(jnp.where(rest == v2, ridx, none), axis=0, keepdims=True)
    e21 = jnp.exp(v2 - v1)
    gate1 = p_grp / (1.0 + e21)
    gate2 = p_grp * e21 / (1.0 + e21)
    ex1 = grp * EXPERTS_PER_GROUP + i1.astype(I32)
    ex2 = grp * EXPERTS_PER_GROUP + i2.astype(I32)

    eidx = lax.broadcasted_iota(I32, (N_EXPERTS, tm), 0)
    oh1 = jnp.where(eidx == ex1, 1.0, 0.0)
    oh2 = jnp.where(eidx == ex2, 1.0, 0.0)
    oh = (oh1 + oh2).astype(MXU_DTYPE)
    earlier = jnp.where(lax.broadcasted_iota(I32, (tm, tm), 0) < lax.broadcasted_iota(I32, (tm, tm), 1),
                        1.0, 0.0).astype(MXU_DTYPE)
    run = run_ref[...]
    before = _dot(oh, earlier) + jnp.concatenate([run] * (tm // LANES), axis=1)
    rank1 = jnp.sum(oh1 * before, axis=0, keepdims=True).astype(I32)
    rank2 = jnp.sum(oh2 * before, axis=0, keepdims=True).astype(I32)
    run = run + _dot(oh, jnp.ones((tm, LANES), MXU_DTYPE))
    run_ref[...] = run
    cnt_ref[...] = run
    ids_ref[...] = jnp.concatenate([ex1, ex2, rank1, rank2, jnp.zeros((4, tm), I32)], axis=0)
    gates_ref[...] = jnp.concatenate([gate1, gate2, jnp.zeros((6, tm), F32)], axis=0)


def _route(logits_t):
    t = logits_t.shape[1]
    tm = min(512, t)
    return pl.pallas_call(
        functools.partial(_route_kernel, tm=tm),
        grid=(t // tm,),
        in_specs=[pl.BlockSpec((ROUTER_ROWS, tm), lambda i: (0, i))],
        out_specs=[pl.BlockSpec((8, tm), lambda i: (0, i)), pl.BlockSpec((8, tm), lambda i: (0, i)),
                   pl.BlockSpec((N_EXPERTS, LANES), lambda i: (0, 0))],
        out_shape=[jax.ShapeDtypeStruct((8, t), I32), jax.ShapeDtypeStruct((8, t), F32),
                   jax.ShapeDtypeStruct((N_EXPERTS, LANES), F32)],
        scratch_shapes=[pltpu.VMEM((N_EXPERTS, LANES), F32)],
        compiler_params=_params("arbitrary"),
        name="route",
    )(logits_t)


def _gather_rows(idx_ref, src_hbm, dst, sem, n):
    def body(r, _):
        pltpu.make_async_copy(src_hbm.at[pl.ds(idx_ref[0, 0, r], 1), :], dst.at[pl.ds(r, 1), :], sem).start()
        return 0

    lax.fori_loop(0, n, body, 0)


def _wait_rows(src_hbm, dst, sem, n):
    pltpu.make_async_copy(src_hbm.at[pl.ds(0, n), :], dst, sem).wait()


def _expert_kernel(be_ref, nu_ref, idx_ref, idx_next_ref, x_hbm, w1_ref, w3_ref, w2_ref, y_ref, xbuf, sem, *, tm):
    i = pl.program_id(0)
    n_used = nu_ref[0]
    slot = i % 2

    @pl.when(i == 0)
    def _():
        _gather_rows(idx_ref, x_hbm, xbuf.at[0], sem.at[0], tm)

    @pl.when(i + 1 < n_used)
    def _():
        _gather_rows(idx_next_ref, x_hbm, xbuf.at[1 - slot], sem.at[1 - slot], tm)

    @pl.when(i < n_used)
    def _():
        _wait_rows(x_hbm, xbuf.at[slot], sem.at[slot], tm)
        xb = xbuf[slot].astype(MXU_DTYPE)
        hidden = _silu(_dot(xb, w1_ref[0])) * _dot(xb, w3_ref[0])
        y_ref[...] = _dot(hidden.astype(MXU_DTYPE), w2_ref[0])

    @pl.when(i >= n_used)
    def _():
        y_ref[...] = jnp.zeros_like(y_ref)


def _experts(u_flat, slot_tok, blk_expert, n_used, w1, w3, w2, tm):
    t, d = u_flat.shape
    n_blk = slot_tok.shape[0] // tm
    ff = w1.shape[-1]
    idx3 = slot_tok.reshape(n_blk, 1, tm)
    last = n_blk - 1
    grid_spec = pltpu.PrefetchScalarGridSpec(
        num_scalar_prefetch=2,
        grid=(n_blk,),
        in_specs=[
            pl.BlockSpec((1, 1, tm), lambda i, be, nu: (i, 0, 0), memory_space=pltpu.SMEM),
            pl.BlockSpec((1, 1, tm), lambda i, be, nu: (jnp.minimum(i + 1, last), 0, 0), memory_space=pltpu.SMEM),
            pl.BlockSpec(memory_space=pl.ANY),
            pl.BlockSpec((1, d, ff), lambda i, be, nu: (be[i], 0, 0)),
            pl.BlockSpec((1, d, ff), lambda i, be, nu: (be[i], 0, 0)),
            pl.BlockSpec((1, ff, d), lambda i, be, nu: (be[i], 0, 0)),
        ],
        out_specs=pl.BlockSpec((tm, d), lambda i, be, nu: (i, 0)),
        scratch_shapes=[pltpu.VMEM((2, tm, d), F32), pltpu.SemaphoreType.DMA((2,))],
    )
    return pl.pallas_call(
        functools.partial(_expert_kernel, tm=tm),
        grid_spec=grid_spec,
        out_shape=jax.ShapeDtypeStruct((n_blk * tm, d), F32),
        compiler_params=_params("arbitrary"),
        name="experts",
    )(blk_expert, n_used, idx3, idx3, u_flat, w1.astype(MXU_DTYPE), w3.astype(MXU_DTYPE), w2.astype(MXU_DTYPE))


def _combine_kernel(d1_ref, d2_ref, d1n_ref, d2n_ref, y_hbm, gates_ref, x_ref, gate_ref, lng_ref, lnb_ref,
                    o_ref, ybuf, sem, *, tm, alpha):
    i = pl.program_id(0)
    n = pl.num_programs(0)
    slot = i % 2

    def start(a_ref, b_ref, s):
        _gather_rows(a_ref, y_hbm, ybuf.at[s, 0], sem.at[s, 0], tm)
        _gather_rows(b_ref, y_hbm, ybuf.at[s, 1], sem.at[s, 1], tm)

    @pl.when(i == 0)
    def _():
        start(d1_ref, d2_ref, 0)

    @pl.when(i + 1 < n)
    def _():
        start(d1n_ref, d2n_ref, 1 - slot)

    _wait_rows(y_hbm, ybuf.at[slot, 0], sem.at[slot, 0], tm)
    _wait_rows(y_hbm, ybuf.at[slot, 1], sem.at[slot, 1], tm)
    gates = gates_ref[...]
    f = gates[:, 0:1] * ybuf[slot, 0] + gates[:, 1:2] * ybuf[slot, 1]
    o_ref[...] = _layer_norm(alpha * x_ref[...] + gate_ref[0] * f, lng_ref[...], lnb_ref[...])


def _combine(y_slots, dest1, dest2, gates, x_flat, gate1p, ln_g, ln_b, seq, alpha):
    t, d = x_flat.shape
    tm = min(256, seq)
    n_blk = t // tm
    per_seq = seq // tm
    d1 = dest1.reshape(n_blk, 1, tm)
    d2 = dest2.reshape(n_blk, 1, tm)
    cur = pl.BlockSpec((1, 1, tm), lambda i: (i, 0, 0), memory_space=pltpu.SMEM)
    nxt = pl.BlockSpec((1, 1, tm), lambda i: (jnp.minimum(i + 1, n_blk - 1), 0, 0), memory_space=pltpu.SMEM)
    row = pl.BlockSpec((1, d), lambda i: (0, 0))
    return pl.pallas_call(
        functools.partial(_combine_kernel, tm=tm, alpha=alpha),
        grid=(n_blk,),
        in_specs=[cur, cur, nxt, nxt, pl.BlockSpec(memory_space=pl.ANY),
                  pl.BlockSpec((tm, TOP_K), lambda i: (i, 0)),
                  pl.BlockSpec((tm, d), lambda i: (i, 0)),
                  pl.BlockSpec((1, 1, d), lambda i: (i // per_seq, 0, 0)), row, row],
        out_specs=pl.BlockSpec((tm, d), lambda i: (i, 0)),
        out_shape=jax.ShapeDtypeStruct((t, d), F32),
        scratch_shapes=[pltpu.VMEM((2, 2, tm, d), F32), pltpu.SemaphoreType.DMA((2, 2))],
        compiler_params=_params("arbitrary"),
        name="combine_norm",
    )(d1, d2, d1, d2, y_slots, gates, x_flat, gate1p, ln_g.reshape(1, d), ln_b.reshape(1, d))


EXPERT_TILE = 256


def _moe(u, logits_t, x, gate2p, ln_g, ln_b, w1, w3, w2, alpha):
    b, s, d = x.shape
    t = b * s
    ids, gates8, cnt = _route(logits_t)
    counts = cnt[:, 0].astype(I32)
    padded = (counts + EXPERT_TILE - 1) // EXPERT_TILE * EXPERT_TILE
    pad_end = jnp.cumsum(padded)
    pad_start = pad_end - padded
    dest1 = pad_start[ids[0]] + ids[2]
    dest2 = pad_start[ids[1]] + ids[3]
    n_slots = t * TOP_K + N_EXPERTS * EXPERT_TILE
    n_blk = n_slots // EXPERT_TILE
    tok = jnp.arange(t, dtype=I32)
    slot_tok = jnp.zeros((n_slots,), I32).at[dest1].set(tok).at[dest2].set(tok)
    blk_expert = jnp.minimum(
        jnp.searchsorted(pad_end, jnp.arange(n_blk, dtype=I32) * EXPERT_TILE, side='right'),
        N_EXPERTS - 1).astype(I32)
    n_used = (pad_end[-1:] // EXPERT_TILE).astype(I32)
    y_slots = _experts(u.reshape(t, d), slot_tok, blk_expert, n_used, w1, w3, w2, EXPERT_TILE)
    out = _combine(y_slots, dest1, dest2, gates8[:TOP_K].T, x.reshape(t, d), gate2p, ln_g, ln_b, s, alpha)
    return out.reshape(b, s, d)


def kernel(x, c, ln1_g, ln1_b, ln2_g, ln2_b, w_ada, b_ada, even_w_in, even_w_out, ret_gn_g, odd_w_in, odd_w_out, lambda_q1, lambda_k1, lambda_q2, lambda_k2, diff_subln_g, moe_w_group, moe_b_group, moe_w_router, moe_b_router, moe_w1, moe_w3, moe_w2):
    b, s, d = x.shape
    depth = w_ada.shape[0]
    alpha = (2.0 * depth) ** 0.25
    mod = _ada(c, w_ada, b_ada)
    for l in range(depth):
        sh1, sc1, g1, sh2, sc2, g2 = [m[:, None, :] for m in jnp.split(mod[l], 6, axis=-1)]
        i = l // 2
        if l % 2 == 0:
            w_in = even_w_in[i]
            w_main = jnp.concatenate([w_in[:, :2 * SB_WIDTH], w_in[:, 3 * SB_WIDTH:]], axis=1)
            proj, v_t = _inproj(x, 1.0 + sc1, sh1, w_main, w_in[:, 2 * SB_WIDTH:3 * SB_WIDTH])
            parts = [_sb_attention(proj, v_t), _retention(proj, ret_gn_g[i], 2 * SB_WIDTH // LANES)]
            w_out = even_w_out[i]
            weights = [w_out[:SB_WIDTH], w_out[SB_WIDTH:]]
        else:
            w_in = odd_w_in[i]
            proj, v_t = _inproj(x, 1.0 + sc1, sh1, w_in[:, :2 * DIFF_QK], w_in[:, 2 * DIFF_QK:])
            lambda_init = 0.8 - 0.6 * math.exp(-0.3 * l)
            lam_rows = jnp.stack([lambda_q1[i], lambda_k1[i], lambda_q2[i], lambda_k2[i]])
            parts = [_diff_attention(proj, v_t, lam_rows, diff_subln_g[i], lambda_init)]
            weights = [odd_w_out[i]]
        w_router_t = jnp.zeros((ROUTER_ROWS, d), F32).at[:N_EXPERTS].set(moe_w_router[l].T.astype(F32))
        w_router_t = w_router_t.at[N_EXPERTS:N_EXPERTS + N_GROUPS].set(moe_w_group[l].T.astype(F32))
        b_router = jnp.zeros((ROUTER_ROWS, 1), F32).at[:N_EXPERTS, 0].set(moe_b_router[l].astype(F32))
        b_router = b_router.at[N_EXPERTS:N_EXPERTS + N_GROUPS, 0].set(moe_b_group[l].astype(F32))
        x, u, logits_t = _outproj(parts, weights, x, 1.0 + g1, ln1_g[l], ln1_b[l], 1.0 + sc2, sh2,
                                  w_router_t, b_router, alpha)
        x = _moe(u, logits_t, x, 1.0 + g2, ln2_g[l], ln2_b[l], moe_w1[l], moe_w3[l], moe_w2[l], alpha)
    return x
```

```python
import functools
import math

import jax
import jax.numpy as jnp
import numpy as np
from jax import lax
from jax.experimental import pallas as pl
from jax.experimental.pallas import tpu as pltpu

F32 = jnp.float32
I32 = jnp.int32
MXU_DTYPE = jnp.bfloat16
HIGHEST = lax.Precision.HIGHEST
LOG2E = math.log2(math.e)

LN_EPS = 1e-5
CHUNK = 64
LANES = 128
SB_HEADS, SB_HEAD_DIM = 8, 64
RET_HEADS, RET_HEAD_DIM = 4, 128
DIFF_HEADS, DIFF_HEAD_DIM = 8, 64
SB_WIDTH = SB_HEADS * SB_HEAD_DIM
RET_WIDTH = RET_HEADS * RET_HEAD_DIM
DIFF_QK = DIFF_HEADS * 2 * DIFF_HEAD_DIM
N_GROUPS, EXPERTS_PER_GROUP = 4, 8
N_EXPERTS = N_GROUPS * EXPERTS_PER_GROUP
TOP_K = 2
ROUTER_ROWS = 40

VMEM_LIMIT = 56 * 1024 * 1024


def _params(*sem):
    return pltpu.CompilerParams(dimension_semantics=sem, vmem_limit_bytes=VMEM_LIMIT)


def _dot(a, b):
    return jnp.dot(a, b, preferred_element_type=F32)


def _dot_nt(a, b):
    return lax.dot_general(a, b, (((1,), (1,)), ((), ())), preferred_element_type=F32)


def _dot_tn(a, b):
    return lax.dot_general(a, b, (((0,), (0,)), ((), ())), preferred_element_type=F32)


def _silu(x):
    return x * (1.0 / (1.0 + jnp.exp(-x)))


def _ada_kernel(c_ref, w_ref, b_ref, o_ref):
    o_ref[0] = jnp.dot(_silu(c_ref[...]), w_ref[0], preferred_element_type=F32, precision=HIGHEST) + b_ref[0]


def _ada(c, w_ada, b_ada):
    depth, d, n = w_ada.shape
    bp = 8
    cp = jnp.zeros((bp, d), F32).at[: c.shape[0]].set(c)
    tn = 1536
    out = pl.pallas_call(
        _ada_kernel,
        grid=(depth, n // tn),
        in_specs=[
            pl.BlockSpec((bp, d), lambda l, j: (0, 0)),
            pl.BlockSpec((1, d, tn), lambda l, j: (l, 0, j)),
            pl.BlockSpec((1, 1, tn), lambda l, j: (l, 0, j)),
        ],
        out_specs=pl.BlockSpec((1, bp, tn), lambda l, j: (l, 0, j)),
        out_shape=jax.ShapeDtypeStruct((depth, bp, n), F32),
        compiler_params=_params("parallel", "parallel"),
        name="ada_mod",
    )(cp, w_ada, b_ada.reshape(depth, 1, n))
    return out[:, : c.shape[0]]


def _inproj_kernel(x_ref, sc_ref, sh_ref, w_ref, wvt_ref, o_ref, vt_ref, *, tn):
    u = (x_ref[0] * sc_ref[0] + sh_ref[0]).astype(MXU_DTYPE)
    for j in range(o_ref.shape[2] // tn):
        o_ref[0, :, j * tn:(j + 1) * tn] = _dot(u, w_ref[:, j * tn:(j + 1) * tn]).astype(o_ref.dtype)
    for j in range(vt_ref.shape[1] // tn):
        vt_ref[0, j * tn:(j + 1) * tn, :] = _dot_nt(wvt_ref[j * tn:(j + 1) * tn, :], u).astype(vt_ref.dtype)


def _inproj(x, scale1p, shift, w, w_v):
    b, s, d = x.shape
    n, n_v = w.shape[1], w_v.shape[1]
    tm = min(512, s)
    return pl.pallas_call(
        functools.partial(_inproj_kernel, tn=512),
        grid=(b, s // tm),
        in_specs=[
            pl.BlockSpec((1, tm, d), lambda i, j: (i, j, 0)),
            pl.BlockSpec((1, 1, d), lambda i, j: (i, 0, 0)),
            pl.BlockSpec((1, 1, d), lambda i, j: (i, 0, 0)),
            pl.BlockSpec((d, n), lambda i, j: (0, 0)),
            pl.BlockSpec((n_v, d), lambda i, j: (0, 0)),
        ],
        out_specs=[pl.BlockSpec((1, tm, n), lambda i, j: (i, j, 0)),
                   pl.BlockSpec((1, n_v, tm), lambda i, j: (i, 0, j))],
        out_shape=[jax.ShapeDtypeStruct((b, s, n), MXU_DTYPE), jax.ShapeDtypeStruct((b, n_v, s), MXU_DTYPE)],
        compiler_params=_params("parallel", "parallel"),
        name="in_proj",
    )(x, scale1p, shift, w.astype(MXU_DTYPE), w_v.T.astype(MXU_DTYPE))


MASKED = -float("inf")
ITEM_PLAIN, ITEM_DIAGONAL, ITEM_NULL = 0, 1, 2
FLAG_FIRST, FLAG_LAST = 1, 2


def _triangle_items(n_q, pad, diagonal_first):
    items = []
    for qb in range(n_q):
        order = range(qb, -1, -1) if diagonal_first else range(qb + 1)
        for n, kb in enumerate(order):
            flags = (FLAG_FIRST if n == 0 else 0) | (FLAG_LAST if n == qb else 0)
            items.append((qb, kb, ITEM_DIAGONAL if kb == qb else ITEM_PLAIN, flags))
    null = (0, 0, ITEM_NULL, 0)
    return np.asarray([null] * pad + items + [null] * (pad + 1), np.int32).T.copy(), len(items)


SB_STAGES = 3


def _sb_kernel(tab_ref, q_ref, k_ref, vt_ref, o_ref, qh_s, acc_ref, carry_ref, mask_ref, y_s, yms_s, sums_s,
               *, tq, n_items):
    tk = tq
    extra = 16
    heads = range(2)
    lane = lax.broadcasted_iota(I32, (1, LANES), 1)
    key = lax.broadcasted_iota(I32, (tk, tq), 0)
    qry = lax.broadcasted_iota(I32, (tk, tq), 1)
    r = lax.broadcasted_iota(I32, (tk + extra, tk), 0)
    c = lax.broadcasted_iota(I32, (tk + extra, tk), 1)
    neg_later = jnp.where(r >= tk, -1.0, jnp.where(c > r, -1.0, 0.0)).astype(MXU_DTYPE)
    scale2 = SB_HEAD_DIM ** -0.5 * LOG2E

    def prepare_queries(blk, _):
        rows = pl.ds(pl.multiple_of(blk * tq, tq), tq)
        q2 = q_ref[0, rows, :]
        for h in heads:
            qh_s[h, rows, :] = (jnp.where((lane // SB_HEAD_DIM) == h, q2, jnp.zeros_like(q2)).astype(F32)
                                * scale2).astype(MXU_DTYPE)
        return 0

    lax.fori_loop(0, q_ref.shape[1] // tq, prepare_queries, 0)
    acc_ref[...] = jnp.zeros_like(acc_ref)
    carry_ref[...] = jnp.zeros_like(carry_ref)
    mask_ref[ITEM_PLAIN] = jnp.zeros((tk, tq), F32)
    mask_ref[ITEM_DIAGONAL] = jnp.where(key < qry, 0.0, MASKED)
    mask_ref[ITEM_NULL] = jnp.full((tk, tq), MASKED, F32)
    y_s[...] = jnp.zeros_like(y_s)
    yms_s[...] = jnp.full(yms_s.shape, MASKED, F32)
    sums_s[...] = jnp.zeros_like(sums_s)

    def rows_of(block, size):
        return pl.ds(pl.multiple_of(block * size, size), size)

    def trip(it, parity):
        col_x, col_y, col_c = it + 2, it + 1, it
        kj = k_ref[0, rows_of(tab_ref[1, col_x], tk), :]
        qrows = rows_of(tab_ref[0, col_x], tq)
        for h in heads:
            y_s[h, parity] = _dot_nt(kj, qh_s[h, qrows, :])
        mask = mask_ref[tab_ref[2, col_y]]
        for h in heads:
            ym = y_s[h, 1 - parity] + mask
            sp = jnp.maximum(ym, jnp.log2(1.0 + jnp.exp2(jnp.minimum(ym, 126.0))))
            yms_s[h, parity] = ym - sp
            sums_s[h, parity] = _dot(neg_later, sp.astype(MXU_DTYPE))
        vtj = vt_ref[0, :, rows_of(tab_ref[1, col_c], tk)]
        keep = jnp.where((tab_ref[3, col_c] & FLAG_FIRST) != 0, 0.0, 1.0)
        for h in heads:
            sums = sums_s[h, 1 - parity]
            carry = carry_ref[h] * keep
            w = jnp.exp2(yms_s[h, 1 - parity] + sums[:tk] + carry).astype(MXU_DTYPE)
            acc_ref[h] = acc_ref[h] * keep + _dot(vtj, w)
            carry_ref[h] = carry + sums[tk:tk + 1]

        @pl.when((tab_ref[3, col_c] & FLAG_LAST) != 0)
        def _():
            sub = lax.broadcasted_iota(I32, (LANES, 1), 0)
            o_ref[0, rows_of(tab_ref[0, col_c], tq), :] = jnp.where(
                sub < SB_HEAD_DIM, acc_ref[0], acc_ref[1]).T.astype(o_ref.dtype)

    def trip_pair(i, _):
        trip(2 * i, 0)
        trip(2 * i + 1, 1)
        return 0

    lax.fori_loop(0, pl.cdiv(n_items + SB_STAGES - 1, 2), trip_pair, 0)


def _sb_attention(proj, v_t):
    b, s, _ = proj.shape
    tq = min(256, s)
    n_pairs = SB_WIDTH // LANES
    table, n_items = _triangle_items(s // tq, SB_STAGES - 1, diagonal_first=True)
    grid_spec = pltpu.PrefetchScalarGridSpec(
        num_scalar_prefetch=1,
        grid=(b, n_pairs),
        in_specs=[
            pl.BlockSpec((1, s, LANES), lambda i, p, t: (i, 0, p)),
            pl.BlockSpec((1, s, LANES), lambda i, p, t: (i, 0, n_pairs + p)),
            pl.BlockSpec((1, LANES, s), lambda i, p, t: (i, p, 0)),
        ],
        out_specs=pl.BlockSpec((1, s, LANES), lambda i, p, t: (i, 0, p)),
        scratch_shapes=[
            pltpu.VMEM((2, s, LANES), MXU_DTYPE),
            pltpu.VMEM((2, LANES, tq), F32),
            pltpu.VMEM((2, 1, tq), F32),
            pltpu.VMEM((3, tq, tq), F32),
            pltpu.VMEM((2, 2, tq, tq), F32),
            pltpu.VMEM((2, 2, tq, tq), F32),
            pltpu.VMEM((2, 2, tq + 16, tq), F32),
        ],
    )
    return pl.pallas_call(
        functools.partial(_sb_kernel, tq=tq, n_items=n_items),
        grid_spec=grid_spec,
        out_shape=jax.ShapeDtypeStruct((b, s, SB_WIDTH), MXU_DTYPE),
        compiler_params=_params("parallel", "parallel"),
        name="sb_attention",
    )(jnp.asarray(table), proj, proj, v_t)


def _sb_kernel_old(q_ref, k_ref, vt_ref, o_ref, acc_ref, carry_ref, *, tq):
    qi = pl.program_id(2)
    tk = tq
    extra = 16
    lane = lax.broadcasted_iota(I32, (1, LANES), 1)
    key = lax.broadcasted_iota(I32, (tk, tq), 0)
    qry = lax.broadcasted_iota(I32, (tk, tq), 1)
    causal = key < qry
    r = lax.broadcasted_iota(I32, (tk + extra, tk), 0)
    c = lax.broadcasted_iota(I32, (tk + extra, tk), 1)
    neg_later = jnp.where(r >= tk, -1.0, jnp.where(c > r, -1.0, 0.0)).astype(MXU_DTYPE)
    q2 = q_ref[0]
    scale2 = SB_HEAD_DIM ** -0.5 * LOG2E

    qh = [(jnp.where((lane // SB_HEAD_DIM) == h, q2, jnp.zeros_like(q2)).astype(F32) * scale2).astype(MXU_DTYPE)
          for h in range(2)]
    acc_ref[...] = jnp.zeros_like(acc_ref)
    carry_ref[...] = jnp.zeros_like(carry_ref)

    def key_block(j, diagonal):
        start = pl.multiple_of(j * tk, tk)
        kj = k_ref[0, pl.ds(start, tk), :]
        vtj = vt_ref[0, :, pl.ds(start, tk)]
        for h in range(2):
            y = _dot_nt(kj, qh[h])
            sp = jnp.maximum(y, 0.0) + jnp.log2(1.0 + jnp.exp2(-jnp.abs(y)))
            if diagonal:
                sp = jnp.where(causal, sp, 0.0)
            sums = _dot(neg_later, sp.astype(MXU_DTYPE))
            carry = carry_ref[h]
            w = jnp.exp2(y - sp + sums[:tk] + carry)
            if diagonal:
                w = jnp.where(causal, w, 0.0)
            acc_ref[h] += _dot(vtj, w.astype(MXU_DTYPE))
            carry_ref[h] = carry + sums[tk:tk + 1]

    key_block(qi, True)

    def body(it, _):
        key_block(qi - 1 - it, False)
        return 0

    lax.fori_loop(0, qi, body, 0)
    sub = lax.broadcasted_iota(I32, (LANES, 1), 0)
    o_ref[0] = jnp.where(sub < SB_HEAD_DIM, acc_ref[0], acc_ref[1]).T.astype(o_ref.dtype)


def _sb_attention_old(proj, v_t):
    b, s, _ = proj.shape
    tq = min(256, s)
    n_pairs = SB_WIDTH // LANES
    return pl.pallas_call(
        functools.partial(_sb_kernel_old, tq=tq),
        grid=(b, n_pairs, s // tq),
        in_specs=[
            pl.BlockSpec((1, tq, LANES), lambda i, p, j: (i, j, p)),
            pl.BlockSpec((1, s, LANES), lambda i, p, j: (i, 0, n_pairs + p)),
            pl.BlockSpec((1, LANES, s), lambda i, p, j: (i, p, 0)),
        ],
        out_specs=pl.BlockSpec((1, tq, LANES), lambda i, p, j: (i, j, p)),
        out_shape=jax.ShapeDtypeStruct((b, s, SB_WIDTH), MXU_DTYPE),
        scratch_shapes=[pltpu.VMEM((2, LANES, tq), F32), pltpu.VMEM((2, 1, tq), F32)],
        compiler_params=_params("parallel", "parallel", "parallel"),
        name="sb_attention_old",
    )(proj, proj, v_t)


def _ret_kernel(lg_ref, q_ref, k_ref, v_ref, g_ref, gn_ref, o_ref, state_ref, *, tb):
    h = pl.program_id(1)
    blk = pl.program_id(2)

    @pl.when(blk == 0)
    def _():
        state_ref[...] = jnp.zeros_like(state_ref)

    lg = lg_ref[h]
    scale = RET_HEAD_DIM ** -0.5
    q = q_ref[0].astype(F32)
    k = k_ref[0].astype(F32)
    v = v_ref[0]
    row = lax.broadcasted_iota(I32, (tb, tb), 0)
    col = lax.broadcasted_iota(I32, (tb, tb), 1)
    dist = jnp.abs(row - col).astype(F32)
    decay = jnp.where(col // CHUNK <= row // CHUNK, jnp.exp(lg * dist) * scale, 0.0)
    pos = lax.broadcasted_iota(I32, (tb, 1), 0).astype(F32)
    scores = _dot_nt(q.astype(MXU_DTYPE), k.astype(MXU_DTYPE)) * decay
    intra = _dot(scores.astype(MXU_DTYPE), v)
    state = state_ref[...]
    q_in = (q * jnp.exp(lg * (pos + 1.0))).astype(MXU_DTYPE)
    inter = _dot(q_in, state.astype(MXU_DTYPE))
    k_out = (k * (jnp.exp(lg * (tb - 1.0 - pos)) * scale)).astype(MXU_DTYPE)
    block_decay = jnp.exp(lg * jnp.full((1, RET_HEAD_DIM), float(tb), F32))
    state_ref[...] = block_decay * state + _dot_tn(k_out, v)
    o = intra + inter
    mu = jnp.mean(o, axis=-1, keepdims=True)
    oc = o - mu
    var = jnp.mean(oc * oc, axis=-1, keepdims=True)
    o = oc * lax.rsqrt(var + LN_EPS) * gn_ref[0] * _silu(g_ref[0].astype(F32))
    o_ref[0] = o.astype(o_ref.dtype)


def _retention(proj, gn_gain, first_col_block):
    b, s, _ = proj.shape
    tb = min(256, s)
    log_gamma = jnp.log1p(-jnp.exp2(-5.0 - jnp.arange(RET_HEADS, dtype=F32)))

    def col(which):
        return lambda i, h, j, lg: (i, j, first_col_block + which * RET_HEADS + h)

    grid_spec = pltpu.PrefetchScalarGridSpec(
        num_scalar_prefetch=1,
        grid=(b, RET_HEADS, s // tb),
        in_specs=[pl.BlockSpec((1, tb, LANES), col(w)) for w in range(4)]
        + [pl.BlockSpec((1, 1, LANES), lambda i, h, j, lg: (h, 0, 0))],
        out_specs=pl.BlockSpec((1, tb, LANES), lambda i, h, j, lg: (i, j, h)),
        scratch_shapes=[pltpu.VMEM((RET_HEAD_DIM, RET_HEAD_DIM), F32)],
    )
    return pl.pallas_call(
        functools.partial(_ret_kernel, tb=tb),
        grid_spec=grid_spec,
        out_shape=jax.ShapeDtypeStruct((b, s, RET_WIDTH), MXU_DTYPE),
        compiler_params=_params("parallel", "parallel", "arbitrary"),
        name="retention",
    )(log_gamma, proj, proj, proj, proj, gn_gain.astype(F32).reshape(RET_HEADS, 1, RET_HEAD_DIM))


DIFF_STAGES = 3
DIFF_HEADS_PER_STEP = 2


def _diff_kernel(slope_ref, tab_ref, q_ref, k_ref, vt_ref, lam_ref, gain_ref, o_ref, qs_s, m_ref, l_ref, acc_ref,
                 bias_ref, z_s, p_s, a_s, lfin_s, *, tq, n_items, lambda_init):
    hp = pl.program_id(1)
    tk = tq
    nh = DIFF_HEADS_PER_STEP
    dv = LANES
    heads = range(nh)
    lane = lax.broadcasted_iota(I32, (1, LANES), 1)
    scale2 = DIFF_HEAD_DIM ** -0.5 * LOG2E
    slope2 = [slope_ref[hp * nh + hh] * LOG2E for hh in heads]

    def rows_of(block, size):
        return pl.ds(pl.multiple_of(block * size, size), size)

    def prepare_queries(blk, _):
        for hh in heads:
            q2 = q_ref[0, rows_of(blk, tq), hh * LANES:(hh + 1) * LANES]
            zero = jnp.zeros_like(q2)
            stacked = jnp.concatenate([jnp.where(lane < DIFF_HEAD_DIM, q2, zero),
                                       jnp.where(lane >= DIFF_HEAD_DIM, q2, zero)], axis=0)
            qs_s[hh, rows_of(blk, 2 * tq), :] = (stacked.astype(F32) * scale2).astype(MXU_DTYPE)
        return 0

    lax.fori_loop(0, q_ref.shape[1] // tq, prepare_queries, 0)
    key = lax.broadcasted_iota(I32, (tk, tq), 0)
    qry = lax.broadcasted_iota(I32, (tk, tq), 1)
    visible = key // CHUNK <= qry // CHUNK
    for hh in heads:
        plain = slope2[hh] * key.astype(F32)
        diag = jnp.where(visible, slope2[hh] * (qry - jnp.abs(qry - key)).astype(F32), MASKED)
        bias_ref[hh, ITEM_PLAIN] = jnp.concatenate([plain, plain], axis=1)
        bias_ref[hh, ITEM_DIAGONAL] = jnp.concatenate([diag, diag], axis=1)
    m_ref[...] = jnp.zeros_like(m_ref)
    l_ref[...] = jnp.zeros_like(l_ref)
    acc_ref[...] = jnp.zeros_like(acc_ref)
    z_s[...] = jnp.zeros_like(z_s)
    p_s[...] = jnp.zeros_like(p_s)
    a_s[...] = jnp.ones_like(a_s)
    lfin_s[...] = jnp.ones_like(lfin_s)

    def trip(it, parity):
        col_a, col_a1, col_c = it + 2, it + 1, it
        krows = rows_of(tab_ref[1, col_a], tk)
        qrows = rows_of(tab_ref[0, col_a], 2 * tq)
        for hh in heads:
            z_s[hh, parity] = _dot_nt(k_ref[0, krows, hh * LANES:(hh + 1) * LANES], qs_s[hh, qrows, :])
        vrows = rows_of(tab_ref[1, col_c], tk)
        for hh in heads:
            acc_ref[hh] = a_s[hh] * acc_ref[hh] + _dot(vt_ref[0, hh * dv:(hh + 1) * dv, vrows], p_s[hh])
        lfin = [lfin_s[hh] for hh in heads]
        kind = tab_ref[2, col_a1]
        first = (tab_ref[3, col_a1] & FLAG_FIRST) != 0
        offset = ((tab_ref[1, col_a1] - tab_ref[0, col_a1]) * tq).astype(F32)
        for hh in heads:
            shift = jnp.where(kind == ITEM_PLAIN, slope2[hh] * offset, jnp.where(kind == ITEM_DIAGONAL, 0.0, MASKED))
            z = z_s[hh, 1 - parity] + bias_ref[hh, jnp.minimum(kind, ITEM_DIAGONAL)]
            m_old = jnp.where(first, MASKED, m_ref[hh])
            m_new = jnp.maximum(m_old, jnp.max(z, axis=0, keepdims=True) + shift)
            p = jnp.exp2(z - (m_new - shift))
            a = jnp.exp2(m_old - m_new)
            l_new = a * l_ref[hh] + jnp.sum(p, axis=0, keepdims=True)
            p_s[hh] = p.astype(MXU_DTYPE)
            a_s[hh] = a
            lfin_s[hh] = l_new
            l_ref[hh] = l_new
            m_ref[hh] = m_new

        @pl.when((tab_ref[3, col_c] & FLAG_LAST) != 0)
        def _():
            lam_v = lam_ref[...]
            lam = (jnp.exp(jnp.sum(lam_v[0:1] * lam_v[1:2], axis=-1, keepdims=True))
                   - jnp.exp(jnp.sum(lam_v[2:3] * lam_v[3:4], axis=-1, keepdims=True)) + lambda_init)
            orows = rows_of(tab_ref[0, col_c], tq)
            for hh in heads:
                o = acc_ref[hh] * (1.0 / lfin[hh])
                o = o[:, :tq] - lam * o[:, tq:]
                o = o * lax.rsqrt(jnp.mean(o * o, axis=0, keepdims=True) + LN_EPS)
                o_ref[0, orows, hh * dv:(hh + 1) * dv] = (o * gain_ref[...] * (1.0 - lambda_init)).T.astype(
                    o_ref.dtype)

    def trip_pair(i, _):
        trip(2 * i, 0)
        trip(2 * i + 1, 1)
        return 0

    lax.fori_loop(0, pl.cdiv(n_items + DIFF_STAGES - 1, 2), trip_pair, 0)


def _diff_attention(proj, v_t, lam_rows, subln_gain, lambda_init):
    b, s, _ = proj.shape
    tq = min(256, s)
    dv = subln_gain.shape[-1]
    nh = DIFF_HEADS_PER_STEP
    slopes = jnp.exp2(-8.0 / DIFF_HEADS * (jnp.arange(DIFF_HEADS, dtype=F32) + 1.0))
    kb = DIFF_QK // (nh * LANES)
    table, n_items = _triangle_items(s // tq, DIFF_STAGES - 1, diagonal_first=False)
    grid_spec = pltpu.PrefetchScalarGridSpec(
        num_scalar_prefetch=2,
        grid=(b, DIFF_HEADS // nh),
        in_specs=[
            pl.BlockSpec((1, s, nh * LANES), lambda i, h, sl, t: (i, 0, h)),
            pl.BlockSpec((1, s, nh * LANES), lambda i, h, sl, t: (i, 0, kb + h)),
            pl.BlockSpec((1, nh * dv, s), lambda i, h, sl, t: (i, h, 0)),
            pl.BlockSpec((4, DIFF_HEAD_DIM), lambda i, h, sl, t: (0, 0)),
            pl.BlockSpec((dv, 1), lambda i, h, sl, t: (0, 0)),
        ],
        out_specs=pl.BlockSpec((1, s, nh * dv), lambda i, h, sl, t: (i, 0, h)),
        scratch_shapes=[
            pltpu.VMEM((nh, 2 * s, LANES), MXU_DTYPE),
            pltpu.VMEM((nh, 1, 2 * tq), F32),
            pltpu.VMEM((nh, 1, 2 * tq), F32),
            pltpu.VMEM((nh, dv, 2 * tq), F32),
            pltpu.VMEM((nh, 2, tq, 2 * tq), F32),
            pltpu.VMEM((nh, 2, tq, 2 * tq), F32),
            pltpu.VMEM((nh, tq, 2 * tq), MXU_DTYPE),
            pltpu.VMEM((nh, 1, 2 * tq), F32),
            pltpu.VMEM((nh, 1, 2 * tq), F32),
        ],
    )
    return pl.pallas_call(
        functools.partial(_diff_kernel, tq=tq, n_items=n_items, lambda_init=lambda_init),
        grid_spec=grid_spec,
        out_shape=jax.ShapeDtypeStruct((b, s, DIFF_HEADS * dv), MXU_DTYPE),
        compiler_params=_params("parallel", "parallel"),
        name="diff_attention",
    )(slopes, jnp.asarray(table), proj, proj, v_t, lam_rows.astype(F32), subln_gain.astype(F32).reshape(dv, 1))


def _diff_kernel_old(slope_ref, q_ref, k_ref, vt_ref, lam_ref, gain_ref, o_ref, m_ref, l_ref, acc_ref, base_ref,
                     *, tq, lambda_init):
    hp = pl.program_id(1)
    qi = pl.program_id(2)
    tk = tq
    nh = DIFF_HEADS_PER_STEP
    dv = LANES
    lane = lax.broadcasted_iota(I32, (1, LANES), 1)
    scale2 = DIFF_HEAD_DIM ** -0.5 * LOG2E
    slope2, qs = [], []
    for hh in range(nh):
        slope2.append(slope_ref[hp * nh + hh] * LOG2E)
        q2 = q_ref[0, :, hh * LANES:(hh + 1) * LANES]
        zero = jnp.zeros_like(q2)
        stacked = jnp.concatenate([jnp.where(lane < DIFF_HEAD_DIM, q2, zero),
                                   jnp.where(lane >= DIFF_HEAD_DIM, q2, zero)], axis=0)
        qs.append((stacked.astype(F32) * scale2).astype(MXU_DTYPE))
        base_ref[hh] = slope2[hh] * lax.broadcasted_iota(I32, (tk, LANES), 0).astype(F32)
    m_ref[...] = jnp.full(m_ref.shape, -jnp.inf, F32)
    l_ref[...] = jnp.zeros_like(l_ref)
    acc_ref[...] = jnp.zeros_like(acc_ref)

    def key_block(j, diagonal):
        start = pl.multiple_of(j * tk, tk)
        for hh in range(nh):
            kj = k_ref[0, pl.ds(start, tk), hh * LANES:(hh + 1) * LANES]
            vtj = vt_ref[0, hh * dv:(hh + 1) * dv, pl.ds(start, tk)]
            z = _dot_nt(kj, qs[hh])
            if diagonal:
                key = lax.broadcasted_iota(I32, (tk, tq), 0)
                qry = lax.broadcasted_iota(I32, (tk, tq), 1)
                visible = key // CHUNK <= qry // CHUNK
                bias_d = slope2[hh] * (qry - jnp.abs(qry - key)).astype(F32)
                z = z + jnp.concatenate([bias_d, bias_d], axis=1)
                z = jnp.where(jnp.concatenate([visible, visible], axis=1), z, -jnp.inf)
                shift = 0.0
            else:
                z = z + jnp.concatenate([base_ref[hh]] * (2 * tq // LANES), axis=1)
                shift = slope2[hh] * ((j - qi) * tq).astype(F32)
            m_old = m_ref[hh]
            m_new = jnp.maximum(m_old, jnp.max(z, axis=0, keepdims=True) + shift)
            p = jnp.exp2(z - (m_new - shift))
            a = jnp.exp2(m_old - m_new)
            l_ref[hh] = a * l_ref[hh] + jnp.sum(p, axis=0, keepdims=True)
            acc_ref[hh] = a * acc_ref[hh] + _dot(vtj, p.astype(MXU_DTYPE))
            m_ref[hh] = m_new

    key_block(qi, True)

    def body(j, _):
        key_block(j, False)
        return 0

    lax.fori_loop(0, qi, body, 0)

    lam_v = lam_ref[...]
    lam = (jnp.exp(jnp.sum(lam_v[0:1] * lam_v[1:2], axis=-1, keepdims=True))
           - jnp.exp(jnp.sum(lam_v[2:3] * lam_v[3:4], axis=-1, keepdims=True)) + lambda_init)
    for hh in range(nh):
        o = acc_ref[hh] * (1.0 / l_ref[hh])
        o = o[:, :tq] - lam * o[:, tq:]
        o = o * lax.rsqrt(jnp.mean(o * o, axis=0, keepdims=True) + LN_EPS)
        o_ref[0, :, hh * dv:(hh + 1) * dv] = (o * gain_ref[...] * (1.0 - lambda_init)).T.astype(o_ref.dtype)


def _diff_attention_old(proj, v_t, lam_rows, subln_gain, lambda_init):
    b, s, _ = proj.shape
    tq = min(256, s)
    dv = subln_gain.shape[-1]
    nh = DIFF_HEADS_PER_STEP
    slopes = jnp.exp2(-8.0 / DIFF_HEADS * (jnp.arange(DIFF_HEADS, dtype=F32) + 1.0))
    kb = DIFF_QK // (nh * LANES)
    grid_spec = pltpu.PrefetchScalarGridSpec(
        num_scalar_prefetch=1,
        grid=(b, DIFF_HEADS // nh, s // tq),
        in_specs=[
            pl.BlockSpec((1, tq, nh * LANES), lambda i, h, j, sl: (i, j, h)),
            pl.BlockSpec((1, s, nh * LANES), lambda i, h, j, sl: (i, 0, kb + h)),
            pl.BlockSpec((1, nh * dv, s), lambda i, h, j, sl: (i, h, 0)),
            pl.BlockSpec((4, DIFF_HEAD_DIM), lambda i, h, j, sl: (0, 0)),
            pl.BlockSpec((dv, 1), lambda i, h, j, sl: (0, 0)),
        ],
        out_specs=pl.BlockSpec((1, tq, nh * dv), lambda i, h, j, sl: (i, j, h)),
        scratch_shapes=[pltpu.VMEM((nh, 1, 2 * tq), F32), pltpu.VMEM((nh, 1, 2 * tq), F32),
                        pltpu.VMEM((nh, dv, 2 * tq), F32), pltpu.VMEM((nh, tq, LANES), F32)],
    )
    return pl.pallas_call(
        functools.partial(_diff_kernel_old, tq=tq, lambda_init=lambda_init),
        grid_spec=grid_spec,
        out_shape=jax.ShapeDtypeStruct((b, s, DIFF_HEADS * dv), MXU_DTYPE),
        compiler_params=_params("parallel", "parallel", "parallel"),
        name="diff_attention_old",
    )(slopes, proj, proj, v_t, lam_rows.astype(F32), subln_gain.astype(F32).reshape(dv, 1))


def _layer_norm(y, g, b):
    mu = jnp.mean(y, axis=-1, keepdims=True)
    yc = y - mu
    var = jnp.mean(yc * yc, axis=-1, keepdims=True)
    return yc * lax.rsqrt(var + LN_EPS) * g + b


def _outproj_kernel(*refs, n_in, alpha):
    a_refs, w_refs = refs[:n_in], refs[n_in:2 * n_in]
    x_ref, gate_ref, lng_ref, lnb_ref, sc_ref, sh_ref, wr_ref, br_ref, xo_ref, u_ref, lg_ref = refs[2 * n_in:]
    mix = _dot(a_refs[0][0], w_refs[0][...])
    for a_ref, w_ref in zip(a_refs[1:], w_refs[1:]):
        mix += _dot(a_ref[0], w_ref[...])
    xn = _layer_norm(alpha * x_ref[0] + gate_ref[0] * mix, lng_ref[...], lnb_ref[...])
    xo_ref[0] = xn
    u = xn * sc_ref[0] + sh_ref[0]
    u_ref[0] = u
    lg_ref[...] = lax.dot_general(wr_ref[...], u, (((1,), (1,)), ((), ())), preferred_element_type=F32,
                                  precision=HIGHEST) + br_ref[...]


def _outproj(parts, weights, x, gate1p, ln_g, ln_b, scale1p, shift, w_router_t, b_router, alpha):
    b, s, d = x.shape
    tm = min(256, s)
    n_in = len(parts)
    vec = pl.BlockSpec((1, 1, d), lambda i, j: (i, 0, 0))
    row = pl.BlockSpec((1, d), lambda i, j: (0, 0))
    in_specs = [pl.BlockSpec((1, tm, p.shape[-1]), lambda i, j: (i, j, 0)) for p in parts]
    in_specs += [pl.BlockSpec(w.shape, lambda i, j: (0, 0)) for w in weights]
    in_specs += [pl.BlockSpec((1, tm, d), lambda i, j: (i, j, 0)), vec, row, row, vec, vec,
                 pl.BlockSpec((ROUTER_ROWS, d), lambda i, j: (0, 0)),
                 pl.BlockSpec((ROUTER_ROWS, 1), lambda i, j: (0, 0))]
    nb = s // tm
    return pl.pallas_call(
        functools.partial(_outproj_kernel, n_in=n_in, alpha=alpha),
        grid=(b, nb),
        in_specs=in_specs,
        out_specs=[pl.BlockSpec((1, tm, d), lambda i, j: (i, j, 0)),
                   pl.BlockSpec((1, tm, d), lambda i, j: (i, j, 0)),
                   pl.BlockSpec((ROUTER_ROWS, tm), lambda i, j: (0, i * nb + j))],
        out_shape=[jax.ShapeDtypeStruct((b, s, d), F32), jax.ShapeDtypeStruct((b, s, d), F32),
                   jax.ShapeDtypeStruct((ROUTER_ROWS, b * s), F32)],
        compiler_params=_params("parallel", "parallel"),
        name="out_proj_norm",
    )(*parts, *[w.astype(MXU_DTYPE) for w in weights], x, gate1p, ln_g.reshape(1, d), ln_b.reshape(1, d),
      scale1p, shift, w_router_t, b_router)


def _route_kernel(lg_ref, ids_ref, gates_ref, cnt_ref, run_ref, *, tm):
    @pl.when(pl.program_id(0) == 0)
    def _():
        run_ref[...] = jnp.zeros_like(run_ref)

    lg = lg_ref[...]
    g0 = N_EXPERTS
    g_max = lg[g0:g0 + 1]
    grp = jnp.zeros((1, tm), I32)
    for i in range(1, N_GROUPS):
        gi = lg[g0 + i:g0 + i + 1]
        better = gi > g_max
        grp = jnp.where(better, i, grp)
        g_max = jnp.where(better, gi, g_max)
    den = jnp.exp(lg[g0:g0 + 1] - g_max)
    for i in range(1, N_GROUPS):
        den += jnp.exp(lg[g0 + i:g0 + i + 1] - g_max)
    p_grp = 1.0 / den

    cand = lg[0:EXPERTS_PER_GROUP]
    for g in range(1, N_GROUPS):
        cand = jnp.where(grp == g, lg[g * EXPERTS_PER_GROUP:(g + 1) * EXPERTS_PER_GROUP], cand)
    ridx = lax.broadcasted_iota(I32, (EXPERTS_PER_GROUP, tm), 0).astype(F32)
    none = float(EXPERTS_PER_GROUP)
    v1 = jnp.max(cand, axis=0, keepdims=True)
    i1 = jnp.min(jnp.where(cand == v1, ridx, none), axis=0, keepdims=True)
    rest = jnp.where(ridx == i1, -jnp.inf, cand)
    v2 = jnp.max(rest, axis=0, keepdims=True)
    i2 = jnp.min(jnp.where(rest == v2, ridx, none), axis=0, keepdims=True)
    e21 = jnp.exp(v2 - v1)
    gate1 = p_grp / (1.0 + e21)
    gate2 = p_grp * e21 / (1.0 + e21)
    ex1 = grp * EXPERTS_PER_GROUP + i1.astype(I32)
    ex2 = grp * EXPERTS_PER_GROUP + i2.astype(I32)

    eidx = lax.broadcasted_iota(I32, (N_EXPERTS, tm), 0)
    oh1 = jnp.where(eidx == ex1, 1.0, 0.0)
    oh2 = jnp.where(eidx == ex2, 1.0, 0.0)
    oh = (oh1 + oh2).astype(MXU_DTYPE)
    earlier = jnp.where(lax.broadcasted_iota(I32, (tm, tm), 0) < lax.broadcasted_iota(I32, (tm, tm), 1),
                        1.0, 0.0).astype(MXU_DTYPE)
    run = run_ref[...]
    before = _dot(oh, earlier) + jnp.concatenate([run] * (tm // LANES), axis=1)
    rank1 = jnp.sum(oh1 * before, axis=0, keepdims=True).astype(I32)
    rank2 = jnp.sum(oh2 * before, axis=0, keepdims=True).astype(I32)
    run = run + _dot(oh, jnp.ones((tm, LANES), MXU_DTYPE))
    run_ref[...] = run
    cnt_ref[...] = run
    ids_ref[...] = jnp.concatenate([ex1, ex2, rank1, rank2, jnp.zeros((4, tm), I32)], axis=0)
    gates_ref[...] = jnp.concatenate([gate1, gate2, jnp.zeros((6, tm), F32)], axis=0)


def _route(logits_t):
    t = logits_t.shape[1]
    tm = min(512, t)
    return pl.pallas_call(
        functools.partial(_route_kernel, tm=tm),
        grid=(t // tm,),
        in_specs=[pl.BlockSpec((ROUTER_ROWS, tm), lambda i: (0, i))],
        out_specs=[pl.BlockSpec((8, tm), lambda i: (0, i)), pl.BlockSpec((8, tm), lambda i: (0, i)),
                   pl.BlockSpec((N_EXPERTS, LANES), lambda i: (0, 0))],
        out_shape=[jax.ShapeDtypeStruct((8, t), I32), jax.ShapeDtypeStruct((8, t), F32),
                   jax.ShapeDtypeStruct((N_EXPERTS, LANES), F32)],
        scratch_shapes=[pltpu.VMEM((N_EXPERTS, LANES), F32)],
        compiler_params=_params("arbitrary"),
        name="route",
    )(logits_t)


def _row_copy(idx_ref, src_hbm, dst, sem, r):
    return pltpu.make_async_copy(src_hbm.at[pl.ds(idx_ref[0, 0, r], 1), :], dst.at[pl.ds(r, 1), :], sem)


def _gather_rows(idx_ref, src_hbm, dst, sem, n):
    def body(r, _):
        _row_copy(idx_ref, src_hbm, dst, sem, r).start()
        return 0

    lax.fori_loop(0, n, body, 0)


def _gather_rows_inline(idx_ref, src_hbm, dst, sem, n):
    for r in range(n):
        _row_copy(idx_ref, src_hbm, dst, sem, r).start()


def _wait_rows(src_hbm, dst, sem, n):
    pltpu.make_async_copy(src_hbm.at[pl.ds(0, n), :], dst, sem).wait()


def _expert_kernel(be_ref, idx_a_ref, idx_b_ref, idx_next_ref, x_hbm, w1a_ref, w3a_ref, w2a_ref, w1b_ref, w3b_ref,
                   w2b_ref, y_ref, xbuf_a, xbuf_b, sem, *, tm):
    i = pl.program_id(0)
    ff = w1a_ref.shape[2]
    n_chunks = ff // EXPERT_FF_CHUNK
    per_dot = pl.cdiv(tm, 3 * n_chunks)

    @pl.when(i == 0)
    def _():
        _gather_rows(idx_a_ref, x_hbm, xbuf_a, sem.at[0], tm)

    def block(cur, cur_sem, w1_ref, w3_ref, w2_ref, out_rows, nxt_idx_ref, nxt, nxt_sem):
        _wait_rows(x_hbm, cur, cur_sem, tm)
        starts = iter(range(tm))

        def start_some():
            for _, r in zip(range(per_dot), starts):
                _row_copy(nxt_idx_ref, x_hbm, nxt, nxt_sem, r).start()

        xb = cur[...].astype(MXU_DTYPE)
        y = None
        for c in range(n_chunks):
            cols = slice(c * EXPERT_FF_CHUNK, (c + 1) * EXPERT_FF_CHUNK)
            start_some()
            h1 = _dot(xb, w1_ref[0, :, cols])
            start_some()
            h3 = _dot(xb, w3_ref[0, :, cols])
            hidden = (_silu(h1) * h3).astype(MXU_DTYPE)
            start_some()
            part = _dot(hidden, w2_ref[0, cols, :])
            y = part if y is None else y + part
        y_ref[out_rows, :] = y

    block(xbuf_a, sem.at[0], w1a_ref, w3a_ref, w2a_ref, slice(0, tm), idx_b_ref, xbuf_b, sem.at[1])
    block(xbuf_b, sem.at[1], w1b_ref, w3b_ref, w2b_ref, slice(tm, 2 * tm), idx_next_ref, xbuf_a, sem.at[0])

    @pl.when(i == pl.num_programs(0) - 1)
    def _():
        _wait_rows(x_hbm, xbuf_a, sem.at[0], tm)


def _experts(u_flat, slot_tok, blk_expert, w1, w3, w2, tm):
    t, d = u_flat.shape
    n_blk = slot_tok.shape[0] // tm
    assert n_blk % 2 == 0
    ff = w1.shape[-1]
    idx3 = slot_tok.reshape(n_blk, 1, tm)
    last = n_blk - 1

    def idx_spec(block_of):
        return pl.BlockSpec((1, 1, tm), lambda i, be: (block_of(i), 0, 0), memory_space=pltpu.SMEM)

    def weight_specs(which):
        return [pl.BlockSpec((1, d, ff), lambda i, be: (be[2 * i + which], 0, 0)),
                pl.BlockSpec((1, d, ff), lambda i, be: (be[2 * i + which], 0, 0)),
                pl.BlockSpec((1, ff, d), lambda i, be: (be[2 * i + which], 0, 0))]

    grid_spec = pltpu.PrefetchScalarGridSpec(
        num_scalar_prefetch=1,
        grid=(n_blk // 2,),
        in_specs=[idx_spec(lambda i: 2 * i), idx_spec(lambda i: 2 * i + 1),
                  idx_spec(lambda i: jnp.minimum(2 * i + 2, last)), pl.BlockSpec(memory_space=pl.ANY)]
        + weight_specs(0) + weight_specs(1),
        out_specs=pl.BlockSpec((2 * tm, d), lambda i, be: (i, 0)),
        scratch_shapes=[pltpu.VMEM((tm, d), F32), pltpu.VMEM((tm, d), F32), pltpu.SemaphoreType.DMA((2,))],
    )
    w1, w3, w2 = w1.astype(MXU_DTYPE), w3.astype(MXU_DTYPE), w2.astype(MXU_DTYPE)
    return pl.pallas_call(
        functools.partial(_expert_kernel, tm=tm),
        grid_spec=grid_spec,
        out_shape=jax.ShapeDtypeStruct((n_blk * tm, d), F32),
        compiler_params=_params("arbitrary"),
        name="experts",
    )(blk_expert, idx3, idx3, idx3, u_flat, w1, w3, w2, w1, w3, w2)


def _combine_kernel(d1_ref, d2_ref, d1n_ref, d2n_ref, y_hbm, gates_ref, x_ref, gate_ref, lng_ref, lnb_ref,
                    o_ref, ybuf, sem, *, tm, alpha):
    i = pl.program_id(0)
    n = pl.num_programs(0)
    slot = i % 2

    def wait(s):
        _wait_rows(y_hbm, ybuf.at[s, 0], sem.at[s, 0], tm)
        _wait_rows(y_hbm, ybuf.at[s, 1], sem.at[s, 1], tm)

    @pl.when(i == 0)
    def _():
        _gather_rows(d1_ref, y_hbm, ybuf.at[0, 0], sem.at[0, 0], tm)
        _gather_rows(d2_ref, y_hbm, ybuf.at[0, 1], sem.at[0, 1], tm)

    wait(slot)
    _gather_rows_inline(d1n_ref, y_hbm, ybuf.at[1 - slot, 0], sem.at[1 - slot, 0], tm)
    _gather_rows_inline(d2n_ref, y_hbm, ybuf.at[1 - slot, 1], sem.at[1 - slot, 1], tm)
    gates = gates_ref[...]
    f = gates[:, 0:1] * ybuf[slot, 0] + gates[:, 1:2] * ybuf[slot, 1]
    o_ref[...] = _layer_norm(alpha * x_ref[...] + gate_ref[0] * f, lng_ref[...], lnb_ref[...])

    @pl.when(i == n - 1)
    def _():
        wait(1 - slot)


def _combine(y_slots, dest1, dest2, gates, x_flat, gate1p, ln_g, ln_b, seq, alpha):
    t, d = x_flat.shape
    tm = min(256, seq)
    n_blk = t // tm
    per_seq = seq // tm
    d1 = dest1.reshape(n_blk, 1, tm)
    d2 = dest2.reshape(n_blk, 1, tm)
    cur = pl.BlockSpec((1, 1, tm), lambda i: (i, 0, 0), memory_space=pltpu.SMEM)
    nxt = pl.BlockSpec((1, 1, tm), lambda i: (jnp.minimum(i + 1, n_blk - 1), 0, 0), memory_space=pltpu.SMEM)
    row = pl.BlockSpec((1, d), lambda i: (0, 0))
    return pl.pallas_call(
        functools.partial(_combine_kernel, tm=tm, alpha=alpha),
        grid=(n_blk,),
        in_specs=[cur, cur, nxt, nxt, pl.BlockSpec(memory_space=pl.ANY),
                  pl.BlockSpec((tm, TOP_K), lambda i: (i, 0)),
                  pl.BlockSpec((tm, d), lambda i: (i, 0)),
                  pl.BlockSpec((1, 1, d), lambda i: (i // per_seq, 0, 0)), row, row],
        out_specs=pl.BlockSpec((tm, d), lambda i: (i, 0)),
        out_shape=jax.ShapeDtypeStruct((t, d), F32),
        scratch_shapes=[pltpu.VMEM((2, 2, tm, d), F32), pltpu.SemaphoreType.DMA((2, 2))],
        compiler_params=_params("arbitrary"),
        name="combine_norm",
    )(d1, d2, d1, d2, y_slots, gates, x_flat, gate1p, ln_g.reshape(1, d), ln_b.reshape(1, d))


EXPERT_TILE = 256
EXPERT_FF_CHUNK = 256


def _moe(u, logits_t, x, gate2p, ln_g, ln_b, w1, w3, w2, alpha):
    b, s, d = x.shape
    t = b * s
    ids, gates8, cnt = _route(logits_t)
    counts = cnt[:, 0].astype(I32)
    padded = (counts + EXPERT_TILE - 1) // EXPERT_TILE * EXPERT_TILE
    pad_end = jnp.cumsum(padded)
    pad_start = pad_end - padded
    dest1 = pad_start[ids[0]] + ids[2]
    dest2 = pad_start[ids[1]] + ids[3]
    n_slots = t * TOP_K + N_EXPERTS * EXPERT_TILE
    n_blk = n_slots // EXPERT_TILE
    tok = jnp.arange(t, dtype=I32)
    slot_tok = jnp.zeros((n_slots,), I32).at[jnp.concatenate([dest1, dest2])].set(jnp.concatenate([tok, tok]))
    blk_start = jnp.arange(n_blk, dtype=I32) * EXPERT_TILE
    blk_expert = jnp.minimum(jnp.sum((pad_end[None, :] <= blk_start[:, None]).astype(I32), axis=1), N_EXPERTS - 1)
    y_slots = _experts(u.reshape(t, d), slot_tok, blk_expert.astype(I32), w1, w3, w2, EXPERT_TILE)
    out = _combine(y_slots, dest1, dest2, gates8[:TOP_K].T, x.reshape(t, d), gate2p, ln_g, ln_b, s, alpha)
    return out.reshape(b, s, d)


def kernel(x, c, ln1_g, ln1_b, ln2_g, ln2_b, w_ada, b_ada, even_w_in, even_w_out, ret_gn_g, odd_w_in, odd_w_out, lambda_q1, lambda_k1, lambda_q2, lambda_k2, diff_subln_g, moe_w_group, moe_b_group, moe_w_router, moe_b_router, moe_w1, moe_w3, moe_w2):
    b, s, d = x.shape
    depth = w_ada.shape[0]
    alpha = (2.0 * depth) ** 0.25
    mod = _ada(c, w_ada, b_ada)
    for l in range(depth):
        sh1, sc1, g1, sh2, sc2, g2 = [m[:, None, :] for m in jnp.split(mod[l], 6, axis=-1)]
        i = l // 2
        if l % 2 == 0:
            w_in = even_w_in[i]
            w_main = jnp.concatenate([w_in[:, :2 * SB_WIDTH], w_in[:, 3 * SB_WIDTH:]], axis=1)
            proj, v_t = _inproj(x, 1.0 + sc1, sh1, w_main, w_in[:, 2 * SB_WIDTH:3 * SB_WIDTH])
            parts = [_sb_attention(proj, v_t), _retention(proj, ret_gn_g[i], 2 * SB_WIDTH // LANES)]
            w_out = even_w_out[i]
            weights = [w_out[:SB_WIDTH], w_out[SB_WIDTH:]]
        else:
            w_in = odd_w_in[i]
            proj, v_t = _inproj(x, 1.0 + sc1, sh1, w_in[:, :2 * DIFF_QK], w_in[:, 2 * DIFF_QK:])
            lambda_init = 0.8 - 0.6 * math.exp(-0.3 * l)
            lam_rows = jnp.stack([lambda_q1[i], lambda_k1[i], lambda_q2[i], lambda_k2[i]])
            parts = [_diff_attention(proj, v_t, lam_rows, diff_subln_g[i], lambda_init)]
            weights = [odd_w_out[i]]
        w_router_t = jnp.zeros((ROUTER_ROWS, d), F32).at[:N_EXPERTS].set(moe_w_router[l].T.astype(F32))
        w_router_t = w_router_t.at[N_EXPERTS:N_EXPERTS + N_GROUPS].set(moe_w_group[l].T.astype(F32))
        b_router = jnp.zeros((ROUTER_ROWS, 1), F32).at[:N_EXPERTS, 0].set(moe_b_router[l].astype(F32))
        b_router = b_router.at[N_EXPERTS:N_EXPERTS + N_GROUPS, 0].set(moe_b_group[l].astype(F32))
        x, u, logits_t = _outproj(parts, weights, x, 1.0 + g1, ln1_g[l], ln1_b[l], 1.0 + sc2, sh2,
                                  w_router_t, b_router, alpha)
        x = _moe(u, logits_t, x, 1.0 + g2, ln2_g[l], ln2_b[l], moe_w1[l], moe_w3[l], moe_w2[l], alpha)
    return x
```

```python
import functools
import math

import jax
import jax.numpy as jnp
import numpy as np
from jax import lax
from jax.experimental import pallas as pl
from jax.experimental.pallas import tpu as pltpu

F32 = jnp.float32
I32 = jnp.int32
MXU_DTYPE = jnp.bfloat16
HIGHEST = lax.Precision.HIGHEST
LOG2E = math.log2(math.e)

LN_EPS = 1e-5
CHUNK = 64
LANES = 128
SB_HEADS, SB_HEAD_DIM = 8, 64
RET_HEADS, RET_HEAD_DIM = 4, 128
DIFF_HEADS, DIFF_HEAD_DIM = 8, 64
SB_WIDTH = SB_HEADS * SB_HEAD_DIM
RET_WIDTH = RET_HEADS * RET_HEAD_DIM
DIFF_QK = DIFF_HEADS * 2 * DIFF_HEAD_DIM
N_GROUPS, EXPERTS_PER_GROUP = 4, 8
N_EXPERTS = N_GROUPS * EXPERTS_PER_GROUP
TOP_K = 2
ROUTER_ROWS = 40

VMEM_LIMIT = 56 * 1024 * 1024


def _params(*sem):
    return pltpu.CompilerParams(dimension_semantics=sem, vmem_limit_bytes=VMEM_LIMIT)


def _dot(a, b):
    return jnp.dot(a, b, preferred_element_type=F32)


def _dot_nt(a, b):
    return lax.dot_general(a, b, (((1,), (1,)), ((), ())), preferred_element_type=F32)


def _dot_tn(a, b):
    return lax.dot_general(a, b, (((0,), (0,)), ((), ())), preferred_element_type=F32)


def _silu(x):
    return x * (1.0 / (1.0 + jnp.exp(-x)))


def _ada_kernel(c_ref, w_ref, b_ref, o_ref):
    o_ref[0] = jnp.dot(_silu(c_ref[...]), w_ref[0], preferred_element_type=F32, precision=HIGHEST) + b_ref[0]


def _ada(c, w_ada, b_ada):
    depth, d, n = w_ada.shape
    bp = 8
    cp = jnp.zeros((bp, d), F32).at[: c.shape[0]].set(c)
    tn = 1536
    out = pl.pallas_call(
        _ada_kernel,
        grid=(depth, n // tn),
        in_specs=[
            pl.BlockSpec((bp, d), lambda l, j: (0, 0)),
            pl.BlockSpec((1, d, tn), lambda l, j: (l, 0, j)),
            pl.BlockSpec((1, 1, tn), lambda l, j: (l, 0, j)),
        ],
        out_specs=pl.BlockSpec((1, bp, tn), lambda l, j: (l, 0, j)),
        out_shape=jax.ShapeDtypeStruct((depth, bp, n), F32),
        compiler_params=_params("parallel", "parallel"),
        name="ada_mod",
    )(cp, w_ada, b_ada.reshape(depth, 1, n))
    return out[:, : c.shape[0]]


def _inproj_kernel(x_ref, sc_ref, sh_ref, w_ref, wvt_ref, o_ref, vt_ref, *, tn):
    u = (x_ref[0] * sc_ref[0] + sh_ref[0]).astype(MXU_DTYPE)
    for j in range(o_ref.shape[2] // tn):
        o_ref[0, :, j * tn:(j + 1) * tn] = _dot(u, w_ref[:, j * tn:(j + 1) * tn]).astype(o_ref.dtype)
    for j in range(vt_ref.shape[1] // tn):
        vt_ref[0, j * tn:(j + 1) * tn, :] = _dot_nt(wvt_ref[j * tn:(j + 1) * tn, :], u).astype(vt_ref.dtype)


def _inproj(x, scale1p, shift, w, w_v):
    b, s, d = x.shape
    n, n_v = w.shape[1], w_v.shape[1]
    tm = min(512, s)
    return pl.pallas_call(
        functools.partial(_inproj_kernel, tn=512),
        grid=(b, s // tm),
        in_specs=[
            pl.BlockSpec((1, tm, d), lambda i, j: (i, j, 0)),
            pl.BlockSpec((1, 1, d), lambda i, j: (i, 0, 0)),
            pl.BlockSpec((1, 1, d), lambda i, j: (i, 0, 0)),
            pl.BlockSpec((d, n), lambda i, j: (0, 0)),
            pl.BlockSpec((n_v, d), lambda i, j: (0, 0)),
        ],
        out_specs=[pl.BlockSpec((1, tm, n), lambda i, j: (i, j, 0)),
                   pl.BlockSpec((1, n_v, tm), lambda i, j: (i, 0, j))],
        out_shape=[jax.ShapeDtypeStruct((b, s, n), MXU_DTYPE), jax.ShapeDtypeStruct((b, n_v, s), MXU_DTYPE)],
        compiler_params=_params("parallel", "parallel"),
        name="in_proj",
    )(x, scale1p, shift, w.astype(MXU_DTYPE), w_v.T.astype(MXU_DTYPE))


MASKED = -float("inf")
ITEM_PLAIN, ITEM_DIAGONAL, ITEM_NULL = 0, 1, 2
FLAG_FIRST, FLAG_LAST = 1, 2


def _triangle_items(n_q, pad, diagonal_first):
    items = []
    for qb in range(n_q):
        order = range(qb, -1, -1) if diagonal_first else range(qb + 1)
        for n, kb in enumerate(order):
            flags = (FLAG_FIRST if n == 0 else 0) | (FLAG_LAST if n == qb else 0)
            items.append((qb, kb, ITEM_DIAGONAL if kb == qb else ITEM_PLAIN, flags))
    null = (0, 0, ITEM_NULL, 0)
    return np.asarray([null] * pad + items + [null] * (pad + 1), np.int32).T.copy(), len(items)


SB_STAGES = 3


def _sb_kernel(tab_ref, q_ref, k_ref, vt_ref, o_ref, qh_s, acc_ref, carry_ref, mask_ref, y_s, yms_s, sums_s,
               *, tq, n_items):
    tk = tq
    extra = 16
    heads = range(2)
    lane = lax.broadcasted_iota(I32, (1, LANES), 1)
    key = lax.broadcasted_iota(I32, (tk, tq), 0)
    qry = lax.broadcasted_iota(I32, (tk, tq), 1)
    r = lax.broadcasted_iota(I32, (tk + extra, tk), 0)
    c = lax.broadcasted_iota(I32, (tk + extra, tk), 1)
    neg_later = jnp.where(r >= tk, -1.0, jnp.where(c > r, -1.0, 0.0)).astype(MXU_DTYPE)
    scale2 = SB_HEAD_DIM ** -0.5 * LOG2E

    def prepare_queries(blk, _):
        rows = pl.ds(pl.multiple_of(blk * tq, tq), tq)
        q2 = q_ref[0, rows, :]
        for h in heads:
            qh_s[h, rows, :] = (jnp.where((lane // SB_HEAD_DIM) == h, q2, jnp.zeros_like(q2)).astype(F32)
                                * scale2).astype(MXU_DTYPE)
        return 0

    lax.fori_loop(0, q_ref.shape[1] // tq, prepare_queries, 0)
    acc_ref[...] = jnp.zeros_like(acc_ref)
    carry_ref[...] = jnp.zeros_like(carry_ref)
    mask_ref[ITEM_PLAIN] = jnp.zeros((tk, tq), F32)
    mask_ref[ITEM_DIAGONAL] = jnp.where(key < qry, 0.0, MASKED)
    mask_ref[ITEM_NULL] = jnp.full((tk, tq), MASKED, F32)
    y_s[...] = jnp.zeros_like(y_s)
    yms_s[...] = jnp.full(yms_s.shape, MASKED, F32)
    sums_s[...] = jnp.zeros_like(sums_s)

    def rows_of(block, size):
        return pl.ds(pl.multiple_of(block * size, size), size)

    def trip(it, parity):
        col_x, col_y, col_c = it + 2, it + 1, it
        kj = k_ref[0, rows_of(tab_ref[1, col_x], tk), :]
        qrows = rows_of(tab_ref[0, col_x], tq)
        for h in heads:
            y_s[h, parity] = _dot_nt(kj, qh_s[h, qrows, :])
        mask = mask_ref[tab_ref[2, col_y]]
        for h in heads:
            ym = y_s[h, 1 - parity] + mask
            sp = jnp.maximum(ym, jnp.log2(1.0 + jnp.exp2(jnp.minimum(ym, 126.0))))
            yms_s[h, parity] = ym - sp
            sums_s[h, parity] = _dot(neg_later, sp.astype(MXU_DTYPE))
        vtj = vt_ref[0, :, rows_of(tab_ref[1, col_c], tk)]
        keep = jnp.where((tab_ref[3, col_c] & FLAG_FIRST) != 0, 0.0, 1.0)
        for h in heads:
            sums = sums_s[h, 1 - parity]
            carry = carry_ref[h] * keep
            w = jnp.exp2(yms_s[h, 1 - parity] + sums[:tk] + carry).astype(MXU_DTYPE)
            acc_ref[h] = acc_ref[h] * keep + _dot(vtj, w)
            carry_ref[h] = carry + sums[tk:tk + 1]

        @pl.when((tab_ref[3, col_c] & FLAG_LAST) != 0)
        def _():
            sub = lax.broadcasted_iota(I32, (LANES, 1), 0)
            o_ref[0, rows_of(tab_ref[0, col_c], tq), :] = jnp.where(
                sub < SB_HEAD_DIM, acc_ref[0], acc_ref[1]).T.astype(o_ref.dtype)

    def trip_pair(i, _):
        trip(2 * i, 0)
        trip(2 * i + 1, 1)
        return 0

    lax.fori_loop(0, pl.cdiv(n_items + SB_STAGES - 1, 2), trip_pair, 0)


def _sb_attention(proj, v_t):
    b, s, _ = proj.shape
    tq = min(256, s)
    n_pairs = SB_WIDTH // LANES
    table, n_items = _triangle_items(s // tq, SB_STAGES - 1, diagonal_first=True)
    grid_spec = pltpu.PrefetchScalarGridSpec(
        num_scalar_prefetch=1,
        grid=(b, n_pairs),
        in_specs=[
            pl.BlockSpec((1, s, LANES), lambda i, p, t: (i, 0, p)),
            pl.BlockSpec((1, s, LANES), lambda i, p, t: (i, 0, n_pairs + p)),
            pl.BlockSpec((1, LANES, s), lambda i, p, t: (i, p, 0)),
        ],
        out_specs=pl.BlockSpec((1, s, LANES), lambda i, p, t: (i, 0, p)),
        scratch_shapes=[
            pltpu.VMEM((2, s, LANES), MXU_DTYPE),
            pltpu.VMEM((2, LANES, tq), F32),
            pltpu.VMEM((2, 1, tq), F32),
            pltpu.VMEM((3, tq, tq), F32),
            pltpu.VMEM((2, 2, tq, tq), F32),
            pltpu.VMEM((2, 2, tq, tq), F32),
            pltpu.VMEM((2, 2, tq + 16, tq), F32),
        ],
    )
    return pl.pallas_call(
        functools.partial(_sb_kernel, tq=tq, n_items=n_items),
        grid_spec=grid_spec,
        out_shape=jax.ShapeDtypeStruct((b, s, SB_WIDTH), MXU_DTYPE),
        compiler_params=_params("parallel", "parallel"),
        name="sb_attention",
    )(jnp.asarray(table), proj, proj, v_t)


def _ret_kernel(lg_ref, q_ref, k_ref, v_ref, g_ref, gn_ref, o_ref, state_ref, *, tb):
    h = pl.program_id(1)
    blk = pl.program_id(2)

    @pl.when(blk == 0)
    def _():
        state_ref[...] = jnp.zeros_like(state_ref)

    lg = lg_ref[h]
    scale = RET_HEAD_DIM ** -0.5
    q = q_ref[0].astype(F32)
    k = k_ref[0].astype(F32)
    v = v_ref[0]
    row = lax.broadcasted_iota(I32, (tb, tb), 0)
    col = lax.broadcasted_iota(I32, (tb, tb), 1)
    dist = jnp.abs(row - col).astype(F32)
    decay = jnp.where(col // CHUNK <= row // CHUNK, jnp.exp(lg * dist) * scale, 0.0)
    pos = lax.broadcasted_iota(I32, (tb, 1), 0).astype(F32)
    scores = _dot_nt(q.astype(MXU_DTYPE), k.astype(MXU_DTYPE)) * decay
    intra = _dot(scores.astype(MXU_DTYPE), v)
    state = state_ref[...]
    q_in = (q * jnp.exp(lg * (pos + 1.0))).astype(MXU_DTYPE)
    inter = _dot(q_in, state.astype(MXU_DTYPE))
    k_out = (k * (jnp.exp(lg * (tb - 1.0 - pos)) * scale)).astype(MXU_DTYPE)
    block_decay = jnp.exp(lg * jnp.full((1, RET_HEAD_DIM), float(tb), F32))
    state_ref[...] = block_decay * state + _dot_tn(k_out, v)
    o = intra + inter
    mu = jnp.mean(o, axis=-1, keepdims=True)
    oc = o - mu
    var = jnp.mean(oc * oc, axis=-1, keepdims=True)
    o = oc * lax.rsqrt(var + LN_EPS) * gn_ref[0] * _silu(g_ref[0].astype(F32))
    o_ref[0] = o.astype(o_ref.dtype)


def _retention(proj, gn_gain, first_col_block):
    b, s, _ = proj.shape
    tb = min(256, s)
    log_gamma = jnp.log1p(-jnp.exp2(-5.0 - jnp.arange(RET_HEADS, dtype=F32)))

    def col(which):
        return lambda i, h, j, lg: (i, j, first_col_block + which * RET_HEADS + h)

    grid_spec = pltpu.PrefetchScalarGridSpec(
        num_scalar_prefetch=1,
        grid=(b, RET_HEADS, s // tb),
        in_specs=[pl.BlockSpec((1, tb, LANES), col(w)) for w in range(4)]
        + [pl.BlockSpec((1, 1, LANES), lambda i, h, j, lg: (h, 0, 0))],
        out_specs=pl.BlockSpec((1, tb, LANES), lambda i, h, j, lg: (i, j, h)),
        scratch_shapes=[pltpu.VMEM((RET_HEAD_DIM, RET_HEAD_DIM), F32)],
    )
    return pl.pallas_call(
        functools.partial(_ret_kernel, tb=tb),
        grid_spec=grid_spec,
        out_shape=jax.ShapeDtypeStruct((b, s, RET_WIDTH), MXU_DTYPE),
        compiler_params=_params("parallel", "parallel", "arbitrary"),
        name="retention",
    )(log_gamma, proj, proj, proj, proj, gn_gain.astype(F32).reshape(RET_HEADS, 1, RET_HEAD_DIM))


DIFF_STAGES = 3
DIFF_HEADS_PER_STEP = 2


def _diff_kernel(slope_ref, tab_ref, q_ref, k_ref, vt_ref, lam_ref, gain_ref, o_ref, qs_s, m_ref, l_ref, acc_ref,
                 bias_ref, z_s, p_s, a_s, lfin_s, *, tq, n_items, lambda_init):
    hp = pl.program_id(1)
    tk = tq
    nh = DIFF_HEADS_PER_STEP
    dv = LANES
    heads = range(nh)
    lane = lax.broadcasted_iota(I32, (1, LANES), 1)
    scale2 = DIFF_HEAD_DIM ** -0.5 * LOG2E
    slope2 = [slope_ref[hp * nh + hh] * LOG2E for hh in heads]

    def rows_of(block, size):
        return pl.ds(pl.multiple_of(block * size, size), size)

    def prepare_queries(blk, _):
        for hh in heads:
            q2 = q_ref[0, rows_of(blk, tq), hh * LANES:(hh + 1) * LANES]
            zero = jnp.zeros_like(q2)
            stacked = jnp.concatenate([jnp.where(lane < DIFF_HEAD_DIM, q2, zero),
                                       jnp.where(lane >= DIFF_HEAD_DIM, q2, zero)], axis=0)
            qs_s[hh, rows_of(blk, 2 * tq), :] = (stacked.astype(F32) * scale2).astype(MXU_DTYPE)
        return 0

    lax.fori_loop(0, q_ref.shape[1] // tq, prepare_queries, 0)
    key = lax.broadcasted_iota(I32, (tk, tq), 0)
    qry = lax.broadcasted_iota(I32, (tk, tq), 1)
    visible = key // CHUNK <= qry // CHUNK
    for hh in heads:
        plain = slope2[hh] * key.astype(F32)
        diag = jnp.where(visible, slope2[hh] * (qry - jnp.abs(qry - key)).astype(F32), MASKED)
        bias_ref[hh, ITEM_PLAIN] = jnp.concatenate([plain, plain], axis=1)
        bias_ref[hh, ITEM_DIAGONAL] = jnp.concatenate([diag, diag], axis=1)
    m_ref[...] = jnp.zeros_like(m_ref)
    l_ref[...] = jnp.zeros_like(l_ref)
    acc_ref[...] = jnp.zeros_like(acc_ref)
    z_s[...] = jnp.zeros_like(z_s)
    p_s[...] = jnp.zeros_like(p_s)
    a_s[...] = jnp.ones_like(a_s)
    lfin_s[...] = jnp.ones_like(lfin_s)

    def trip(it, parity):
        col_a, col_a1, col_c = it + 2, it + 1, it
        krows = rows_of(tab_ref[1, col_a], tk)
        qrows = rows_of(tab_ref[0, col_a], 2 * tq)
        for hh in heads:
            z_s[hh, parity] = _dot_nt(k_ref[0, krows, hh * LANES:(hh + 1) * LANES], qs_s[hh, qrows, :])
        vrows = rows_of(tab_ref[1, col_c], tk)
        for hh in heads:
            acc_ref[hh] = a_s[hh] * acc_ref[hh] + _dot(vt_ref[0, hh * dv:(hh + 1) * dv, vrows], p_s[hh])
        lfin = [lfin_s[hh] for hh in heads]
        kind = tab_ref[2, col_a1]
        first = (tab_ref[3, col_a1] & FLAG_FIRST) != 0
        offset = ((tab_ref[1, col_a1] - tab_ref[0, col_a1]) * tq).astype(F32)
        for hh in heads:
            shift = jnp.where(kind == ITEM_PLAIN, slope2[hh] * offset, jnp.where(kind == ITEM_DIAGONAL, 0.0, MASKED))
            z = z_s[hh, 1 - parity] + bias_ref[hh, jnp.minimum(kind, ITEM_DIAGONAL)]
            m_old = jnp.where(first, MASKED, m_ref[hh])
            m_new = jnp.maximum(m_old, jnp.max(z, axis=0, keepdims=True) + shift)
            p = jnp.exp2(z - (m_new - shift))
            a = jnp.exp2(m_old - m_new)
            l_new = a * l_ref[hh] + jnp.sum(p, axis=0, keepdims=True)
            p_s[hh] = p.astype(MXU_DTYPE)
            a_s[hh] = a
            lfin_s[hh] = l_new
            l_ref[hh] = l_new
            m_ref[hh] = m_new

        @pl.when((tab_ref[3, col_c] & FLAG_LAST) != 0)
        def _():
            lam_v = lam_ref[...]
            lam = (jnp.exp(jnp.sum(lam_v[0:1] * lam_v[1:2], axis=-1, keepdims=True))
                   - jnp.exp(jnp.sum(lam_v[2:3] * lam_v[3:4], axis=-1, keepdims=True)) + lambda_init)
            orows = rows_of(tab_ref[0, col_c], tq)
            for hh in heads:
                o = acc_ref[hh] * (1.0 / lfin[hh])
                o = o[:, :tq] - lam * o[:, tq:]
                o = o * lax.rsqrt(jnp.mean(o * o, axis=0, keepdims=True) + LN_EPS)
                o_ref[0, orows, hh * dv:(hh + 1) * dv] = (o * gain_ref[...] * (1.0 - lambda_init)).T.astype(
                    o_ref.dtype)

    def trip_pair(i, _):
        trip(2 * i, 0)
        trip(2 * i + 1, 1)
        return 0

    lax.fori_loop(0, pl.cdiv(n_items + DIFF_STAGES - 1, 2), trip_pair, 0)


def _diff_attention(proj, v_t, lam_rows, subln_gain, lambda_init):
    b, s, _ = proj.shape
    tq = min(256, s)
    dv = subln_gain.shape[-1]
    nh = DIFF_HEADS_PER_STEP
    slopes = jnp.exp2(-8.0 / DIFF_HEADS * (jnp.arange(DIFF_HEADS, dtype=F32) + 1.0))
    kb = DIFF_QK // (nh * LANES)
    table, n_items = _triangle_items(s // tq, DIFF_STAGES - 1, diagonal_first=False)
    grid_spec = pltpu.PrefetchScalarGridSpec(
        num_scalar_prefetch=2,
        grid=(b, DIFF_HEADS // nh),
        in_specs=[
            pl.BlockSpec((1, s, nh * LANES), lambda i, h, sl, t: (i, 0, h)),
            pl.BlockSpec((1, s, nh * LANES), lambda i, h, sl, t: (i, 0, kb + h)),
            pl.BlockSpec((1, nh * dv, s), lambda i, h, sl, t: (i, h, 0)),
            pl.BlockSpec((4, DIFF_HEAD_DIM), lambda i, h, sl, t: (0, 0)),
            pl.BlockSpec((dv, 1), lambda i, h, sl, t: (0, 0)),
        ],
        out_specs=pl.BlockSpec((1, s, nh * dv), lambda i, h, sl, t: (i, 0, h)),
        scratch_shapes=[
            pltpu.VMEM((nh, 2 * s, LANES), MXU_DTYPE),
            pltpu.VMEM((nh, 1, 2 * tq), F32),
            pltpu.VMEM((nh, 1, 2 * tq), F32),
            pltpu.VMEM((nh, dv, 2 * tq), F32),
            pltpu.VMEM((nh, 2, tq, 2 * tq), F32),
            pltpu.VMEM((nh, 2, tq, 2 * tq), F32),
            pltpu.VMEM((nh, tq, 2 * tq), MXU_DTYPE),
            pltpu.VMEM((nh, 1, 2 * tq), F32),
            pltpu.VMEM((nh, 1, 2 * tq), F32),
        ],
    )
    return pl.pallas_call(
        functools.partial(_diff_kernel, tq=tq, n_items=n_items, lambda_init=lambda_init),
        grid_spec=grid_spec,
        out_shape=jax.ShapeDtypeStruct((b, s, DIFF_HEADS * dv), MXU_DTYPE),
        compiler_params=_params("parallel", "parallel"),
        name="diff_attention",
    )(slopes, jnp.asarray(table), proj, proj, v_t, lam_rows.astype(F32), subln_gain.astype(F32).reshape(dv, 1))


def _layer_norm(y, g, b):
    mu = jnp.mean(y, axis=-1, keepdims=True)
    yc = y - mu
    var = jnp.mean(yc * yc, axis=-1, keepdims=True)
    return yc * lax.rsqrt(var + LN_EPS) * g + b


def _outproj_kernel(*refs, n_in, alpha):
    a_refs, w_refs = refs[:n_in], refs[n_in:2 * n_in]
    x_ref, gate_ref, lng_ref, lnb_ref, sc_ref, sh_ref, wr_ref, br_ref, xo_ref, u_ref, lg_ref = refs[2 * n_in:]
    mix = _dot(a_refs[0][0], w_refs[0][...])
    for a_ref, w_ref in zip(a_refs[1:], w_refs[1:]):
        mix += _dot(a_ref[0], w_ref[...])
    xn = _layer_norm(alpha * x_ref[0] + gate_ref[0] * mix, lng_ref[...], lnb_ref[...])
    xo_ref[0] = xn
    u = xn * sc_ref[0] + sh_ref[0]
    u_ref[0] = u
    lg_ref[...] = lax.dot_general(wr_ref[...], u, (((1,), (1,)), ((), ())), preferred_element_type=F32,
                                  precision=HIGHEST) + br_ref[...]


def _outproj(parts, weights, x, gate1p, ln_g, ln_b, scale1p, shift, w_router_t, b_router, alpha):
    b, s, d = x.shape
    tm = min(256, s)
    n_in = len(parts)
    vec = pl.BlockSpec((1, 1, d), lambda i, j: (i, 0, 0))
    row = pl.BlockSpec((1, d), lambda i, j: (0, 0))
    in_specs = [pl.BlockSpec((1, tm, p.shape[-1]), lambda i, j: (i, j, 0)) for p in parts]
    in_specs += [pl.BlockSpec(w.shape, lambda i, j: (0, 0)) for w in weights]
    in_specs += [pl.BlockSpec((1, tm, d), lambda i, j: (i, j, 0)), vec, row, row, vec, vec,
                 pl.BlockSpec((ROUTER_ROWS, d), lambda i, j: (0, 0)),
                 pl.BlockSpec((ROUTER_ROWS, 1), lambda i, j: (0, 0))]
    nb = s // tm
    return pl.pallas_call(
        functools.partial(_outproj_kernel, n_in=n_in, alpha=alpha),
        grid=(b, nb),
        in_specs=in_specs,
        out_specs=[pl.BlockSpec((1, tm, d), lambda i, j: (i, j, 0)),
                   pl.BlockSpec((1, tm, d), lambda i, j: (i, j, 0)),
                   pl.BlockSpec((ROUTER_ROWS, tm), lambda i, j: (0, i * nb + j))],
        out_shape=[jax.ShapeDtypeStruct((b, s, d), F32), jax.ShapeDtypeStruct((b, s, d), F32),
                   jax.ShapeDtypeStruct((ROUTER_ROWS, b * s), F32)],
        compiler_params=_params("parallel", "parallel"),
        name="out_proj_norm",
    )(*parts, *[w.astype(MXU_DTYPE) for w in weights], x, gate1p, ln_g.reshape(1, d), ln_b.reshape(1, d),
      scale1p, shift, w_router_t, b_router)


def _route_kernel(lg_ref, ids_ref, gates_ref, cnt_ref, run_ref, *, tm):
    @pl.when(pl.program_id(0) == 0)
    def _():
        run_ref[...] = jnp.zeros_like(run_ref)

    lg = lg_ref[...]
    g0 = N_EXPERTS
    g_max = lg[g0:g0 + 1]
    grp = jnp.zeros((1, tm), I32)
    for i in range(1, N_GROUPS):
        gi = lg[g0 + i:g0 + i + 1]
        better = gi > g_max
        grp = jnp.where(better, i, grp)
        g_max = jnp.where(better, gi, g_max)
    den = jnp.exp(lg[g0:g0 + 1] - g_max)
    for i in range(1, N_GROUPS):
        den += jnp.exp(lg[g0 + i:g0 + i + 1] - g_max)
    p_grp = 1.0 / den

    cand = lg[0:EXPERTS_PER_GROUP]
    for g in range(1, N_GROUPS):
        cand = jnp.where(grp == g, lg[g * EXPERTS_PER_GROUP:(g + 1) * EXPERTS_PER_GROUP], cand)
    ridx = lax.broadcasted_iota(I32, (EXPERTS_PER_GROUP, tm), 0).astype(F32)
    none = float(EXPERTS_PER_GROUP)
    v1 = jnp.max(cand, axis=0, keepdims=True)
    i1 = jnp.min(jnp.where(cand == v1, ridx, none), axis=0, keepdims=True)
    rest = jnp.where(ridx == i1, -jnp.inf, cand)
    v2 = jnp.max(rest, axis=0, keepdims=True)
    i2 = jnp.min(jnp.where(rest == v2, ridx, none), axis=0, keepdims=True)
    e21 = jnp.exp(v2 - v1)
    gate1 = p_grp / (1.0 + e21)
    gate2 = p_grp * e21 / (1.0 + e21)
    ex1 = grp * EXPERTS_PER_GROUP + i1.astype(I32)
    ex2 = grp * EXPERTS_PER_GROUP + i2.astype(I32)

    eidx = lax.broadcasted_iota(I32, (N_EXPERTS, tm), 0)
    oh1 = jnp.where(eidx == ex1, 1.0, 0.0)
    oh2 = jnp.where(eidx == ex2, 1.0, 0.0)
    oh = (oh1 + oh2).astype(MXU_DTYPE)
    earlier = jnp.where(lax.broadcasted_iota(I32, (tm, tm), 0) < lax.broadcasted_iota(I32, (tm, tm), 1),
                        1.0, 0.0).astype(MXU_DTYPE)
    run = run_ref[...]
    before = _dot(oh, earlier) + jnp.concatenate([run] * (tm // LANES), axis=1)
    rank1 = jnp.sum(oh1 * before, axis=0, keepdims=True).astype(I32)
    rank2 = jnp.sum(oh2 * before, axis=0, keepdims=True).astype(I32)
    run = run + _dot(oh, jnp.ones((tm, LANES), MXU_DTYPE))
    run_ref[...] = run
    cnt_ref[...] = run
    ids_ref[...] = jnp.concatenate([ex1, ex2, rank1, rank2, jnp.zeros((4, tm), I32)], axis=0)
    gates_ref[...] = jnp.concatenate([gate1, gate2, jnp.zeros((6, tm), F32)], axis=0)


def _route(logits_t):
    t = logits_t.shape[1]
    tm = min(512, t)
    return pl.pallas_call(
        functools.partial(_route_kernel, tm=tm),
        grid=(t // tm,),
        in_specs=[pl.BlockSpec((ROUTER_ROWS, tm), lambda i: (0, i))],
        out_specs=[pl.BlockSpec((8, tm), lambda i: (0, i)), pl.BlockSpec((8, tm), lambda i: (0, i)),
                   pl.BlockSpec((N_EXPERTS, LANES), lambda i: (0, 0))],
        out_shape=[jax.ShapeDtypeStruct((8, t), I32), jax.ShapeDtypeStruct((8, t), F32),
                   jax.ShapeDtypeStruct((N_EXPERTS, LANES), F32)],
        scratch_shapes=[pltpu.VMEM((N_EXPERTS, LANES), F32)],
        compiler_params=_params("arbitrary"),
        name="route",
    )(logits_t)


def _row_copy(idx_ref, src_hbm, dst, sem, r):
    return pltpu.make_async_copy(src_hbm.at[pl.ds(idx_ref[0, 0, r], 1), :], dst.at[pl.ds(r, 1), :], sem)


def _gather_rows(idx_ref, src_hbm, dst, sem, n):
    def body(r, _):
        _row_copy(idx_ref, src_hbm, dst, sem, r).start()
        return 0

    lax.fori_loop(0, n, body, 0)


def _gather_rows_inline(idx_ref, src_hbm, dst, sem, n):
    for r in range(n):
        _row_copy(idx_ref, src_hbm, dst, sem, r).start()


def _wait_rows(src_hbm, dst, sem, n):
    pltpu.make_async_copy(src_hbm.at[pl.ds(0, n), :], dst, sem).wait()


def _expert_kernel(be_ref, idx_a_ref, idx_b_ref, idx_next_ref, x_hbm, w1a_ref, w3a_ref, w2a_ref, w1b_ref, w3b_ref,
                   w2b_ref, y_ref, xbuf_a, xbuf_b, sem, *, tm):
    i = pl.program_id(0)

    @pl.when(i == 0)
    def _():
        _gather_rows(idx_a_ref, x_hbm, xbuf_a, sem.at[0], tm)

    def block(cur, cur_sem, w1_ref, w3_ref, w2_ref, out_rows, nxt_idx_ref, nxt, nxt_sem):
        _wait_rows(x_hbm, cur, cur_sem, tm)
        _gather_rows_inline(nxt_idx_ref, x_hbm, nxt, nxt_sem, tm)
        xb = cur[...].astype(MXU_DTYPE)
        hidden = _silu(_dot(xb, w1_ref[0])) * _dot(xb, w3_ref[0])
        y_ref[out_rows, :] = _dot(hidden.astype(MXU_DTYPE), w2_ref[0])

    block(xbuf_a, sem.at[0], w1a_ref, w3a_ref, w2a_ref, slice(0, tm), idx_b_ref, xbuf_b, sem.at[1])
    block(xbuf_b, sem.at[1], w1b_ref, w3b_ref, w2b_ref, slice(tm, 2 * tm), idx_next_ref, xbuf_a, sem.at[0])

    @pl.when(i == pl.num_programs(0) - 1)
    def _():
        _wait_rows(x_hbm, xbuf_a, sem.at[0], tm)


def _experts(u_flat, slot_tok, blk_expert, w1, w3, w2, tm):
    t, d = u_flat.shape
    n_blk = slot_tok.shape[0] // tm
    assert n_blk % 2 == 0
    ff = w1.shape[-1]
    idx3 = slot_tok.reshape(n_blk, 1, tm)
    last = n_blk - 1

    def idx_spec(block_of):
        return pl.BlockSpec((1, 1, tm), lambda i, be: (block_of(i), 0, 0), memory_space=pltpu.SMEM)

    def weight_specs(which):
        return [pl.BlockSpec((1, d, ff), lambda i, be: (be[2 * i + which], 0, 0)),
                pl.BlockSpec((1, d, ff), lambda i, be: (be[2 * i + which], 0, 0)),
                pl.BlockSpec((1, ff, d), lambda i, be: (be[2 * i + which], 0, 0))]

    grid_spec = pltpu.PrefetchScalarGridSpec(
        num_scalar_prefetch=1,
        grid=(n_blk // 2,),
        in_specs=[idx_spec(lambda i: 2 * i), idx_spec(lambda i: 2 * i + 1),
                  idx_spec(lambda i: jnp.minimum(2 * i + 2, last)), pl.BlockSpec(memory_space=pl.ANY)]
        + weight_specs(0) + weight_specs(1),
        out_specs=pl.BlockSpec((2 * tm, d), lambda i, be: (i, 0)),
        scratch_shapes=[pltpu.VMEM((tm, d), F32), pltpu.VMEM((tm, d), F32), pltpu.SemaphoreType.DMA((2,))],
    )
    w1, w3, w2 = w1.astype(MXU_DTYPE), w3.astype(MXU_DTYPE), w2.astype(MXU_DTYPE)
    return pl.pallas_call(
        functools.partial(_expert_kernel, tm=tm),
        grid_spec=grid_spec,
        out_shape=jax.ShapeDtypeStruct((n_blk * tm, d), F32),
        compiler_params=_params("arbitrary"),
        name="experts",
    )(blk_expert, idx3, idx3, idx3, u_flat, w1, w3, w2, w1, w3, w2)


def _combine_kernel(d1_ref, d2_ref, d1n_ref, d2n_ref, y_hbm, gates_ref, x_ref, gate_ref, lng_ref, lnb_ref,
                    o_ref, ybuf, sem, *, tm, alpha):
    i = pl.program_id(0)
    n = pl.num_programs(0)
    slot = i % 2

    def wait(s):
        _wait_rows(y_hbm, ybuf.at[s, 0], sem.at[s, 0], tm)
        _wait_rows(y_hbm, ybuf.at[s, 1], sem.at[s, 1], tm)

    @pl.when(i == 0)
    def _():
        _gather_rows(d1_ref, y_hbm, ybuf.at[0, 0], sem.at[0, 0], tm)
        _gather_rows(d2_ref, y_hbm, ybuf.at[0, 1], sem.at[0, 1], tm)

    wait(slot)
    _gather_rows_inline(d1n_ref, y_hbm, ybuf.at[1 - slot, 0], sem.at[1 - slot, 0], tm)
    _gather_rows_inline(d2n_ref, y_hbm, ybuf.at[1 - slot, 1], sem.at[1 - slot, 1], tm)
    gates = gates_ref[...]
    f = gates[:, 0:1] * ybuf[slot, 0] + gates[:, 1:2] * ybuf[slot, 1]
    o_ref[...] = _layer_norm(alpha * x_ref[...] + gate_ref[0] * f, lng_ref[...], lnb_ref[...])

    @pl.when(i == n - 1)
    def _():
        wait(1 - slot)


def _combine(y_slots, dest1, dest2, gates, x_flat, gate1p, ln_g, ln_b, seq, alpha):
    t, d = x_flat.shape
    tm = min(256, seq)
    n_blk = t // tm
    per_seq = seq // tm
    d1 = dest1.reshape(n_blk, 1, tm)
    d2 = dest2.reshape(n_blk, 1, tm)
    cur = pl.BlockSpec((1, 1, tm), lambda i: (i, 0, 0), memory_space=pltpu.SMEM)
    nxt = pl.BlockSpec((1, 1, tm), lambda i: (jnp.minimum(i + 1, n_blk - 1), 0, 0), memory_space=pltpu.SMEM)
    row = pl.BlockSpec((1, d), lambda i: (0, 0))
    return pl.pallas_call(
        functools.partial(_combine_kernel, tm=tm, alpha=alpha),
        grid=(n_blk,),
        in_specs=[cur, cur, nxt, nxt, pl.BlockSpec(memory_space=pl.ANY),
                  pl.BlockSpec((tm, TOP_K), lambda i: (i, 0)),
                  pl.BlockSpec((tm, d), lambda i: (i, 0)),
                  pl.BlockSpec((1, 1, d), lambda i: (i // per_seq, 0, 0)), row, row],
        out_specs=pl.BlockSpec((tm, d), lambda i: (i, 0)),
        out_shape=jax.ShapeDtypeStruct((t, d), F32),
        scratch_shapes=[pltpu.VMEM((2, 2, tm, d), F32), pltpu.SemaphoreType.DMA((2, 2))],
        compiler_params=_params("arbitrary"),
        name="combine_norm",
    )(d1, d2, d1, d2, y_slots, gates, x_flat, gate1p, ln_g.reshape(1, d), ln_b.reshape(1, d))


EXPERT_TILE = 256

def _moe(u, logits_t, x, gate2p, ln_g, ln_b, w1, w3, w2, alpha):
    b, s, d = x.shape
    t = b * s
    ids, gates8, cnt = _route(logits_t)
    counts = cnt[:, 0].astype(I32)
    padded = (counts + EXPERT_TILE - 1) // EXPERT_TILE * EXPERT_TILE
    pad_end = jnp.cumsum(padded)
    pad_start = pad_end - padded
    dest1 = pad_start[ids[0]] + ids[2]
    dest2 = pad_start[ids[1]] + ids[3]
    n_slots = t * TOP_K + N_EXPERTS * EXPERT_TILE
    n_blk = n_slots // EXPERT_TILE
    tok = jnp.arange(t, dtype=I32)
    slot_tok = jnp.zeros((n_slots,), I32).at[jnp.concatenate([dest1, dest2])].set(jnp.concatenate([tok, tok]))
    blk_start = jnp.arange(n_blk, dtype=I32) * EXPERT_TILE
    blk_expert = jnp.minimum(jnp.sum((pad_end[None, :] <= blk_start[:, None]).astype(I32), axis=1), N_EXPERTS - 1)
    y_slots = _experts(u.reshape(t, d), slot_tok, blk_expert.astype(I32), w1, w3, w2, EXPERT_TILE)
    out = _combine(y_slots, dest1, dest2, gates8[:TOP_K].T, x.reshape(t, d), gate2p, ln_g, ln_b, s, alpha)
    return out.reshape(b, s, d)


def kernel(x, c, ln1_g, ln1_b, ln2_g, ln2_b, w_ada, b_ada, even_w_in, even_w_out, ret_gn_g, odd_w_in, odd_w_out, lambda_q1, lambda_k1, lambda_q2, lambda_k2, diff_subln_g, moe_w_group, moe_b_group, moe_w_router, moe_b_router, moe_w1, moe_w3, moe_w2):
    b, s, d = x.shape
    depth = w_ada.shape[0]
    alpha = (2.0 * depth) ** 0.25
    mod = _ada(c, w_ada, b_ada)
    for l in range(depth):
        sh1, sc1, g1, sh2, sc2, g2 = [m[:, None, :] for m in jnp.split(mod[l], 6, axis=-1)]
        i = l // 2
        if l % 2 == 0:
            w_in = even_w_in[i]
            w_main = jnp.concatenate([w_in[:, :2 * SB_WIDTH], w_in[:, 3 * SB_WIDTH:]], axis=1)
            proj, v_t = _inproj(x, 1.0 + sc1, sh1, w_main, w_in[:, 2 * SB_WIDTH:3 * SB_WIDTH])
            parts = [_sb_attention(proj, v_t), _retention(proj, ret_gn_g[i], 2 * SB_WIDTH // LANES)]
            w_out = even_w_out[i]
            weights = [w_out[:SB_WIDTH], w_out[SB_WIDTH:]]
        else:
            w_in = odd_w_in[i]
            proj, v_t = _inproj(x, 1.0 + sc1, sh1, w_in[:, :2 * DIFF_QK], w_in[:, 2 * DIFF_QK:])
            lambda_init = 0.8 - 0.6 * math.exp(-0.3 * l)
            lam_rows = jnp.stack([lambda_q1[i], lambda_k1[i], lambda_q2[i], lambda_k2[i]])
            parts = [_diff_attention(proj, v_t, lam_rows, diff_subln_g[i], lambda_init)]
            weights = [odd_w_out[i]]
        w_router_t = jnp.zeros((ROUTER_ROWS, d), F32).at[:N_EXPERTS].set(moe_w_router[l].T.astype(F32))
        w_router_t = w_router_t.at[N_EXPERTS:N_EXPERTS + N_GROUPS].set(moe_w_group[l].T.astype(F32))
        b_router = jnp.zeros((ROUTER_ROWS, 1), F32).at[:N_EXPERTS, 0].set(moe_b_router[l].astype(F32))
        b_router = b_router.at[N_EXPERTS:N_EXPERTS + N_GROUPS, 0].set(moe_b_group[l].astype(F32))
        x, u, logits_t = _outproj(parts, weights, x, 1.0 + g1, ln1_g[l], ln1_b[l], 1.0 + sc2, sh2,
                                  w_router_t, b_router, alpha)
        x = _moe(u, logits_t, x, 1.0 + g2, ln2_g[l], ln2_b[l], moe_w1[l], moe_w3[l], moe_w2[l], alpha)
    return x
```

```python
import functools
import math

import jax
import jax.numpy as jnp
import numpy as np
from jax import lax
from jax.experimental import pallas as pl
from jax.experimental.pallas import tpu as pltpu

F32 = jnp.float32
I32 = jnp.int32
MXU_DTYPE = jnp.bfloat16
HIGHEST = lax.Precision.HIGHEST
LOG2E = math.log2(math.e)

LN_EPS = 1e-5
CHUNK = 64
LANES = 128
SB_HEADS, SB_HEAD_DIM = 8, 64
RET_HEADS, RET_HEAD_DIM = 4, 128
DIFF_HEADS, DIFF_HEAD_DIM = 8, 64
SB_WIDTH = SB_HEADS * SB_HEAD_DIM
RET_WIDTH = RET_HEADS * RET_HEAD_DIM
DIFF_QK = DIFF_HEADS * 2 * DIFF_HEAD_DIM
N_GROUPS, EXPERTS_PER_GROUP = 4, 8
N_EXPERTS = N_GROUPS * EXPERTS_PER_GROUP
TOP_K = 2
ROUTER_ROWS = 40

VMEM_LIMIT = 56 * 1024 * 1024


def _params(*sem):
    return pltpu.CompilerParams(dimension_semantics=sem, vmem_limit_bytes=VMEM_LIMIT)


def _dot(a, b):
    return jnp.dot(a, b, preferred_element_type=F32)


def _dot_nt(a, b):
    return lax.dot_general(a, b, (((1,), (1,)), ((), ())), preferred_element_type=F32)


def _dot_tn(a, b):
    return lax.dot_general(a, b, (((0,), (0,)), ((), ())), preferred_element_type=F32)


def _silu(x):
    return x * (1.0 / (1.0 + jnp.exp(-x)))


def _ada_kernel(c_ref, w_ref, b_ref, o_ref):
    o_ref[0] = jnp.dot(_silu(c_ref[...]), w_ref[0], preferred_element_type=F32, precision=HIGHEST) + b_ref[0]


def _ada(c, w_ada, b_ada):
    depth, d, n = w_ada.shape
    bp = 8
    cp = jnp.zeros((bp, d), F32).at[: c.shape[0]].set(c)
    tn = 1536
    out = pl.pallas_call(
        _ada_kernel,
        grid=(depth, n // tn),
        in_specs=[
            pl.BlockSpec((bp, d), lambda l, j: (0, 0)),
            pl.BlockSpec((1, d, tn), lambda l, j: (l, 0, j)),
            pl.BlockSpec((1, 1, tn), lambda l, j: (l, 0, j)),
        ],
        out_specs=pl.BlockSpec((1, bp, tn), lambda l, j: (l, 0, j)),
        out_shape=jax.ShapeDtypeStruct((depth, bp, n), F32),
        compiler_params=_params("parallel", "parallel"),
        name="ada_mod",
    )(cp, w_ada, b_ada.reshape(depth, 1, n))
    return out[:, : c.shape[0]]


def _inproj_kernel(x_ref, sc_ref, sh_ref, w_ref, wvt_ref, o_ref, vt_ref, *, tn):
    u = (x_ref[0] * sc_ref[0] + sh_ref[0]).astype(MXU_DTYPE)
    for j in range(o_ref.shape[2] // tn):
        o_ref[0, :, j * tn:(j + 1) * tn] = _dot(u, w_ref[:, j * tn:(j + 1) * tn]).astype(o_ref.dtype)
    for j in range(vt_ref.shape[1] // tn):
        vt_ref[0, j * tn:(j + 1) * tn, :] = _dot_nt(wvt_ref[j * tn:(j + 1) * tn, :], u).astype(vt_ref.dtype)


def _inproj(x, scale1p, shift, w, w_v):
    b, s, d = x.shape
    n, n_v = w.shape[1], w_v.shape[1]
    tm = min(512, s)
    return pl.pallas_call(
        functools.partial(_inproj_kernel, tn=512),
        grid=(b, s // tm),
        in_specs=[
            pl.BlockSpec((1, tm, d), lambda i, j: (i, j, 0)),
            pl.BlockSpec((1, 1, d), lambda i, j: (i, 0, 0)),
            pl.BlockSpec((1, 1, d), lambda i, j: (i, 0, 0)),
            pl.BlockSpec((d, n), lambda i, j: (0, 0)),
            pl.BlockSpec((n_v, d), lambda i, j: (0, 0)),
        ],
        out_specs=[pl.BlockSpec((1, tm, n), lambda i, j: (i, j, 0)),
                   pl.BlockSpec((1, n_v, tm), lambda i, j: (i, 0, j))],
        out_shape=[jax.ShapeDtypeStruct((b, s, n), MXU_DTYPE), jax.ShapeDtypeStruct((b, n_v, s), MXU_DTYPE)],
        compiler_params=_params("parallel", "parallel"),
        name="in_proj",
    )(x, scale1p, shift, w.astype(MXU_DTYPE), w_v.T.astype(MXU_DTYPE))


MASKED = -float("inf")
ITEM_PLAIN, ITEM_DIAGONAL, ITEM_NULL = 0, 1, 2
FLAG_FIRST, FLAG_LAST = 1, 2


def _triangle_items(n_q, pad, diagonal_first, n_streams):
    streams = []
    for s in range(n_streams):
        items = []
        for qb in range(s, n_q, n_streams):
            order = range(qb, -1, -1) if diagonal_first else range(qb + 1)
            for n, kb in enumerate(order):
                flags = (FLAG_FIRST if n == 0 else 0) | (FLAG_LAST if n == qb else 0)
                items.append((qb, kb, ITEM_DIAGONAL if kb == qb else ITEM_PLAIN, flags))
        streams.append(items)
    n_items = max(len(items) for items in streams)
    null = (0, 0, ITEM_NULL, 0)
    rows = [np.asarray([null] * pad + items + [null] * (n_items - len(items) + pad + 1), np.int32).T
            for items in streams]
    return np.concatenate(rows, axis=0).copy(), n_items


SB_STAGES = 3
SB_STREAMS = 2


def _sb_kernel(tab_ref, q_ref, k_ref, vt_ref, o_ref, qh_s, acc_ref, carry_ref, mask_ref, y_s, yms_s, sums_s,
               *, tq, n_items):
    tk = tq
    extra = 16
    heads = range(2)
    streams = range(SB_STREAMS)
    lane = lax.broadcasted_iota(I32, (1, LANES), 1)
    key = lax.broadcasted_iota(I32, (tk, tq), 0)
    qry = lax.broadcasted_iota(I32, (tk, tq), 1)
    r = lax.broadcasted_iota(I32, (tk + extra, tk), 0)
    c = lax.broadcasted_iota(I32, (tk + extra, tk), 1)
    neg_later = jnp.where(r >= tk, -1.0, jnp.where(c > r, -1.0, 0.0)).astype(MXU_DTYPE)
    scale2 = SB_HEAD_DIM ** -0.5 * LOG2E

    def prepare_queries(blk, _):
        rows = pl.ds(pl.multiple_of(blk * tq, tq), tq)
        q2 = q_ref[0, rows, :]
        for h in heads:
            qh_s[h, rows, :] = (jnp.where((lane // SB_HEAD_DIM) == h, q2, jnp.zeros_like(q2)).astype(F32)
                                * scale2).astype(MXU_DTYPE)
        return 0

    lax.fori_loop(0, q_ref.shape[1] // tq, prepare_queries, 0)
    acc_ref[...] = jnp.zeros_like(acc_ref)
    carry_ref[...] = jnp.zeros_like(carry_ref)
    mask_ref[ITEM_PLAIN] = jnp.zeros((tk, tq), F32)
    mask_ref[ITEM_DIAGONAL] = jnp.where(key < qry, 0.0, MASKED)
    mask_ref[ITEM_NULL] = jnp.full((tk, tq), MASKED, F32)
    y_s[...] = jnp.zeros_like(y_s)
    yms_s[...] = jnp.full(yms_s.shape, MASKED, F32)
    sums_s[...] = jnp.zeros_like(sums_s)

    def rows_of(block, size):
        return pl.ds(pl.multiple_of(block * size, size), size)

    def trip(it, parity):
        col_x, col_y, col_c = it + 2, it + 1, it
        for s in streams:
            kj = k_ref[0, rows_of(tab_ref[4 * s + 1, col_x], tk), :]
            qrows = rows_of(tab_ref[4 * s, col_x], tq)
            for h in heads:
                y_s[2 * s + h, parity] = _dot_nt(kj, qh_s[h, qrows, :])
        for s in streams:
            mask = mask_ref[tab_ref[4 * s + 2, col_y]]
            for h in heads:
                ym = y_s[2 * s + h, 1 - parity] + mask
                sp = jnp.maximum(ym, jnp.log2(1.0 + jnp.exp2(jnp.minimum(ym, 126.0))))
                yms_s[2 * s + h, parity] = ym - sp
                sums_s[2 * s + h, parity] = _dot(neg_later, sp.astype(MXU_DTYPE))
        for s in streams:
            vtj = vt_ref[0, :, rows_of(tab_ref[4 * s + 1, col_c], tk)]
            keep = jnp.where((tab_ref[4 * s + 3, col_c] & FLAG_FIRST) != 0, 0.0, 1.0)
            for h in heads:
                sums = sums_s[2 * s + h, 1 - parity]
                carry = carry_ref[2 * s + h] * keep
                w = jnp.exp2(yms_s[2 * s + h, 1 - parity] + sums[:tk] + carry).astype(MXU_DTYPE)
                acc_ref[2 * s + h] = acc_ref[2 * s + h] * keep + _dot(vtj, w)
                carry_ref[2 * s + h] = carry + sums[tk:tk + 1]

        for s in streams:
            @pl.when((tab_ref[4 * s + 3, col_c] & FLAG_LAST) != 0)
            def _(s=s):
                sub = lax.broadcasted_iota(I32, (LANES, 1), 0)
                o_ref[0, rows_of(tab_ref[4 * s, col_c], tq), :] = jnp.where(
                    sub < SB_HEAD_DIM, acc_ref[2 * s], acc_ref[2 * s + 1]).T.astype(o_ref.dtype)

    def trip_pair(i, _):
        trip(2 * i, 0)
        trip(2 * i + 1, 1)
        return 0

    lax.fori_loop(0, pl.cdiv(n_items + SB_STAGES - 1, 2), trip_pair, 0)


def _sb_attention(proj, v_t):
    b, s, _ = proj.shape
    tq = min(256, s)
    n_pairs = SB_WIDTH // LANES
    table, n_items = _triangle_items(s // tq, SB_STAGES - 1, diagonal_first=True, n_streams=SB_STREAMS)
    n_chains = 2 * SB_STREAMS
    grid_spec = pltpu.PrefetchScalarGridSpec(
        num_scalar_prefetch=1,
        grid=(b, n_pairs),
        in_specs=[
            pl.BlockSpec((1, s, LANES), lambda i, p, t: (i, 0, p)),
            pl.BlockSpec((1, s, LANES), lambda i, p, t: (i, 0, n_pairs + p)),
            pl.BlockSpec((1, LANES, s), lambda i, p, t: (i, p, 0)),
        ],
        out_specs=pl.BlockSpec((1, s, LANES), lambda i, p, t: (i, 0, p)),
        scratch_shapes=[
            pltpu.VMEM((2, s, LANES), MXU_DTYPE),
            pltpu.VMEM((n_chains, LANES, tq), F32),
            pltpu.VMEM((n_chains, 1, tq), F32),
            pltpu.VMEM((3, tq, tq), F32),
            pltpu.VMEM((n_chains, 2, tq, tq), F32),
            pltpu.VMEM((n_chains, 2, tq, tq), F32),
            pltpu.VMEM((n_chains, 2, tq + 16, tq), F32),
        ],
    )
    return pl.pallas_call(
        functools.partial(_sb_kernel, tq=tq, n_items=n_items),
        grid_spec=grid_spec,
        out_shape=jax.ShapeDtypeStruct((b, s, SB_WIDTH), MXU_DTYPE),
        compiler_params=_params("parallel", "parallel"),
        name="sb_attention",
    )(jnp.asarray(table), proj, proj, v_t)


def _ret_kernel(lg_ref, q_ref, k_ref, v_ref, g_ref, gn_ref, o_ref, state_ref, *, tb):
    h = pl.program_id(1)
    blk = pl.program_id(2)

    @pl.when(blk == 0)
    def _():
        state_ref[...] = jnp.zeros_like(state_ref)

    lg = lg_ref[h]
    scale = RET_HEAD_DIM ** -0.5
    q = q_ref[0].astype(F32)
    k = k_ref[0].astype(F32)
    v = v_ref[0]
    row = lax.broadcasted_iota(I32, (tb, tb), 0)
    col = lax.broadcasted_iota(I32, (tb, tb), 1)
    dist = jnp.abs(row - col).astype(F32)
    decay = jnp.where(col // CHUNK <= row // CHUNK, jnp.exp(lg * dist) * scale, 0.0)
    pos = lax.broadcasted_iota(I32, (tb, 1), 0).astype(F32)
    scores = _dot_nt(q.astype(MXU_DTYPE), k.astype(MXU_DTYPE)) * decay
    intra = _dot(scores.astype(MXU_DTYPE), v)
    state = state_ref[...]
    q_in = (q * jnp.exp(lg * (pos + 1.0))).astype(MXU_DTYPE)
    inter = _dot(q_in, state.astype(MXU_DTYPE))
    k_out = (k * (jnp.exp(lg * (tb - 1.0 - pos)) * scale)).astype(MXU_DTYPE)
    block_decay = jnp.exp(lg * jnp.full((1, RET_HEAD_DIM), float(tb), F32))
    state_ref[...] = block_decay * state + _dot_tn(k_out, v)
    o = intra + inter
    mu = jnp.mean(o, axis=-1, keepdims=True)
    oc = o - mu
    var = jnp.mean(oc * oc, axis=-1, keepdims=True)
    o = oc * lax.rsqrt(var + LN_EPS) * gn_ref[0] * _silu(g_ref[0].astype(F32))
    o_ref[0] = o.astype(o_ref.dtype)


def _retention(proj, gn_gain, first_col_block):
    b, s, _ = proj.shape
    tb = min(256, s)
    log_gamma = jnp.log1p(-jnp.exp2(-5.0 - jnp.arange(RET_HEADS, dtype=F32)))

    def col(which):
        return lambda i, h, j, lg: (i, j, first_col_block + which * RET_HEADS + h)

    grid_spec = pltpu.PrefetchScalarGridSpec(
        num_scalar_prefetch=1,
        grid=(b, RET_HEADS, s // tb),
        in_specs=[pl.BlockSpec((1, tb, LANES), col(w)) for w in range(4)]
        + [pl.BlockSpec((1, 1, LANES), lambda i, h, j, lg: (h, 0, 0))],
        out_specs=pl.BlockSpec((1, tb, LANES), lambda i, h, j, lg: (i, j, h)),
        scratch_shapes=[pltpu.VMEM((RET_HEAD_DIM, RET_HEAD_DIM), F32)],
    )
    return pl.pallas_call(
        functools.partial(_ret_kernel, tb=tb),
        grid_spec=grid_spec,
        out_shape=jax.ShapeDtypeStruct((b, s, RET_WIDTH), MXU_DTYPE),
        compiler_params=_params("parallel", "parallel", "arbitrary"),
        name="retention",
    )(log_gamma, proj, proj, proj, proj, gn_gain.astype(F32).reshape(RET_HEADS, 1, RET_HEAD_DIM))


DIFF_STAGES = 3
DIFF_HEADS_PER_STEP = 2


def _diff_kernel(slope_ref, tab_ref, q_ref, k_ref, vt_ref, lam_ref, gain_ref, o_ref, qs_s, m_ref, l_ref, acc_ref,
                 bias_ref, z_s, p_s, a_s, lfin_s, *, tq, n_items, lambda_init):
    hp = pl.program_id(1)
    tk = tq
    nh = DIFF_HEADS_PER_STEP
    dv = LANES
    heads = range(nh)
    lane = lax.broadcasted_iota(I32, (1, LANES), 1)
    scale2 = DIFF_HEAD_DIM ** -0.5 * LOG2E
    slope2 = [slope_ref[hp * nh + hh] * LOG2E for hh in heads]

    def rows_of(block, size):
        return pl.ds(pl.multiple_of(block * size, size), size)

    def prepare_queries(blk, _):
        for hh in heads:
            q2 = q_ref[0, rows_of(blk, tq), hh * LANES:(hh + 1) * LANES]
            zero = jnp.zeros_like(q2)
            stacked = jnp.concatenate([jnp.where(lane < DIFF_HEAD_DIM, q2, zero),
                                       jnp.where(lane >= DIFF_HEAD_DIM, q2, zero)], axis=0)
            qs_s[hh, rows_of(blk, 2 * tq), :] = (stacked.astype(F32) * scale2).astype(MXU_DTYPE)
        return 0

    lax.fori_loop(0, q_ref.shape[1] // tq, prepare_queries, 0)
    key = lax.broadcasted_iota(I32, (tk, tq), 0)
    qry = lax.broadcasted_iota(I32, (tk, tq), 1)
    visible = key // CHUNK <= qry // CHUNK
    for hh in heads:
        plain = slope2[hh] * key.astype(F32)
        diag = jnp.where(visible, slope2[hh] * (qry - jnp.abs(qry - key)).astype(F32), MASKED)
        bias_ref[hh, ITEM_PLAIN] = jnp.concatenate([plain, plain], axis=1)
        bias_ref[hh, ITEM_DIAGONAL] = jnp.concatenate([diag, diag], axis=1)
    m_ref[...] = jnp.zeros_like(m_ref)
    l_ref[...] = jnp.zeros_like(l_ref)
    acc_ref[...] = jnp.zeros_like(acc_ref)
    z_s[...] = jnp.zeros_like(z_s)
    p_s[...] = jnp.zeros_like(p_s)
    a_s[...] = jnp.ones_like(a_s)
    lfin_s[...] = jnp.ones_like(lfin_s)

    def trip(it, parity):
        col_a, col_a1, col_c = it + 2, it + 1, it
        krows = rows_of(tab_ref[1, col_a], tk)
        qrows = rows_of(tab_ref[0, col_a], 2 * tq)
        for hh in heads:
            z_s[hh, parity] = _dot_nt(k_ref[0, krows, hh * LANES:(hh + 1) * LANES], qs_s[hh, qrows, :])
        vrows = rows_of(tab_ref[1, col_c], tk)
        for hh in heads:
            acc_ref[hh] = a_s[hh] * acc_ref[hh] + _dot(vt_ref[0, hh * dv:(hh + 1) * dv, vrows], p_s[hh])
        lfin = [lfin_s[hh] for hh in heads]
        kind = tab_ref[2, col_a1]
        first = (tab_ref[3, col_a1] & FLAG_FIRST) != 0
        offset = ((tab_ref[1, col_a1] - tab_ref[0, col_a1]) * tq).astype(F32)
        for hh in heads:
            shift = jnp.where(kind == ITEM_PLAIN, slope2[hh] * offset, jnp.where(kind == ITEM_DIAGONAL, 0.0, MASKED))
            z = z_s[hh, 1 - parity] + bias_ref[hh, jnp.minimum(kind, ITEM_DIAGONAL)]
            m_old = jnp.where(first, MASKED, m_ref[hh])
            m_new = jnp.maximum(m_old, jnp.max(z, axis=0, keepdims=True) + shift)
            p = jnp.exp2(z - (m_new - shift))
            a = jnp.exp2(m_old - m_new)
            l_new = a * l_ref[hh] + jnp.sum(p, axis=0, keepdims=True)
            p_s[hh] = p.astype(MXU_DTYPE)
            a_s[hh] = a
            lfin_s[hh] = l_new
            l_ref[hh] = l_new
            m_ref[hh] = m_new

        @pl.when((tab_ref[3, col_c] & FLAG_LAST) != 0)
        def _():
            lam_v = lam_ref[...]
            lam = (jnp.exp(jnp.sum(lam_v[0:1] * lam_v[1:2], axis=-1, keepdims=True))
                   - jnp.exp(jnp.sum(lam_v[2:3] * lam_v[3:4], axis=-1, keepdims=True)) + lambda_init)
            orows = rows_of(tab_ref[0, col_c], tq)
            for hh in heads:
                o = acc_ref[hh] * (1.0 / lfin[hh])
                o = o[:, :tq] - lam * o[:, tq:]
                o = o * lax.rsqrt(jnp.mean(o * o, axis=0, keepdims=True) + LN_EPS)
                o_ref[0, orows, hh * dv:(hh + 1) * dv] = (o * gain_ref[...] * (1.0 - lambda_init)).T.astype(
                    o_ref.dtype)

    def trip_pair(i, _):
        trip(2 * i, 0)
        trip(2 * i + 1, 1)
        return 0

    lax.fori_loop(0, pl.cdiv(n_items + DIFF_STAGES - 1, 2), trip_pair, 0)


def _diff_attention(proj, v_t, lam_rows, subln_gain, lambda_init):
    b, s, _ = proj.shape
    tq = min(256, s)
    dv = subln_gain.shape[-1]
    nh = DIFF_HEADS_PER_STEP
    slopes = jnp.exp2(-8.0 / DIFF_HEADS * (jnp.arange(DIFF_HEADS, dtype=F32) + 1.0))
    kb = DIFF_QK // (nh * LANES)
    table, n_items = _triangle_items(s // tq, DIFF_STAGES - 1, diagonal_first=False, n_streams=1)
    grid_spec = pltpu.PrefetchScalarGridSpec(
        num_scalar_prefetch=2,
        grid=(b, DIFF_HEADS // nh),
        in_specs=[
            pl.BlockSpec((1, s, nh * LANES), lambda i, h, sl, t: (i, 0, h)),
            pl.BlockSpec((1, s, nh * LANES), lambda i, h, sl, t: (i, 0, kb + h)),
            pl.BlockSpec((1, nh * dv, s), lambda i, h, sl, t: (i, h, 0)),
            pl.BlockSpec((4, DIFF_HEAD_DIM), lambda i, h, sl, t: (0, 0)),
            pl.BlockSpec((dv, 1), lambda i, h, sl, t: (0, 0)),
        ],
        out_specs=pl.BlockSpec((1, s, nh * dv), lambda i, h, sl, t: (i, 0, h)),
        scratch_shapes=[
            pltpu.VMEM((nh, 2 * s, LANES), MXU_DTYPE),
            pltpu.VMEM((nh, 1, 2 * tq), F32),
            pltpu.VMEM((nh, 1, 2 * tq), F32),
            pltpu.VMEM((nh, dv, 2 * tq), F32),
            pltpu.VMEM((nh, 2, tq, 2 * tq), F32),
            pltpu.VMEM((nh, 2, tq, 2 * tq), F32),
            pltpu.VMEM((nh, tq, 2 * tq), MXU_DTYPE),
            pltpu.VMEM((nh, 1, 2 * tq), F32),
            pltpu.VMEM((nh, 1, 2 * tq), F32),
        ],
    )
    return pl.pallas_call(
        functools.partial(_diff_kernel, tq=tq, n_items=n_items, lambda_init=lambda_init),
        grid_spec=grid_spec,
        out_shape=jax.ShapeDtypeStruct((b, s, DIFF_HEADS * dv), MXU_DTYPE),
        compiler_params=_params("parallel", "parallel"),
        name="diff_attention",
    )(slopes, jnp.asarray(table), proj, proj, v_t, lam_rows.astype(F32), subln_gain.astype(F32).reshape(dv, 1))


def _layer_norm(y, g, b):
    mu = jnp.mean(y, axis=-1, keepdims=True)
    yc = y - mu
    var = jnp.mean(yc * yc, axis=-1, keepdims=True)
    return yc * lax.rsqrt(var + LN_EPS) * g + b


def _outproj_kernel(*refs, n_in, alpha):
    a_refs, w_refs = refs[:n_in], refs[n_in:2 * n_in]
    x_ref, gate_ref, lng_ref, lnb_ref, sc_ref, sh_ref, wr_ref, br_ref, xo_ref, u_ref, lg_ref = refs[2 * n_in:]
    mix = _dot(a_refs[0][0], w_refs[0][...])
    for a_ref, w_ref in zip(a_refs[1:], w_refs[1:]):
        mix += _dot(a_ref[0], w_ref[...])
    xn = _layer_norm(alpha * x_ref[0] + gate_ref[0] * mix, lng_ref[...], lnb_ref[...])
    xo_ref[0] = xn
    u = xn * sc_ref[0] + sh_ref[0]
    u_ref[0] = u
    lg_ref[...] = lax.dot_general(wr_ref[...], u, (((1,), (1,)), ((), ())), preferred_element_type=F32,
                                  precision=HIGHEST) + br_ref[...]


def _outproj(parts, weights, x, gate1p, ln_g, ln_b, scale1p, shift, w_router_t, b_router, alpha):
    b, s, d = x.shape
    tm = min(256, s)
    n_in = len(parts)
    vec = pl.BlockSpec((1, 1, d), lambda i, j: (i, 0, 0))
    row = pl.BlockSpec((1, d), lambda i, j: (0, 0))
    in_specs = [pl.BlockSpec((1, tm, p.shape[-1]), lambda i, j: (i, j, 0)) for p in parts]
    in_specs += [pl.BlockSpec(w.shape, lambda i, j: (0, 0)) for w in weights]
    in_specs += [pl.BlockSpec((1, tm, d), lambda i, j: (i, j, 0)), vec, row, row, vec, vec,
                 pl.BlockSpec((ROUTER_ROWS, d), lambda i, j: (0, 0)),
                 pl.BlockSpec((ROUTER_ROWS, 1), lambda i, j: (0, 0))]
    nb = s // tm
    return pl.pallas_call(
        functools.partial(_outproj_kernel, n_in=n_in, alpha=alpha),
        grid=(b, nb),
        in_specs=in_specs,
        out_specs=[pl.BlockSpec((1, tm, d), lambda i, j: (i, j, 0)),
                   pl.BlockSpec((1, tm, d), lambda i, j: (i, j, 0)),
                   pl.BlockSpec((ROUTER_ROWS, tm), lambda i, j: (0, i * nb + j))],
        out_shape=[jax.ShapeDtypeStruct((b, s, d), F32), jax.ShapeDtypeStruct((b, s, d), F32),
                   jax.ShapeDtypeStruct((ROUTER_ROWS, b * s), F32)],
        compiler_params=_params("parallel", "parallel"),
        name="out_proj_norm",
    )(*parts, *[w.astype(MXU_DTYPE) for w in weights], x, gate1p, ln_g.reshape(1, d), ln_b.reshape(1, d),
      scale1p, shift, w_router_t, b_router)


def _route_kernel(lg_ref, ids_ref, gates_ref, cnt_ref, run_ref, *, tm):
    @pl.when(pl.program_id(0) == 0)
    def _():
        run_ref[...] = jnp.zeros_like(run_ref)

    lg = lg_ref[...]
    g0 = N_EXPERTS
    g_max = lg[g0:g0 + 1]
    grp = jnp.zeros((1, tm), I32)
    for i in range(1, N_GROUPS):
        gi = lg[g0 + i:g0 + i + 1]
        better = gi > g_max
        grp = jnp.where(better, i, grp)
        g_max = jnp.where(better, gi, g_max)
    den = jnp.exp(lg[g0:g0 + 1] - g_max)
    for i in range(1, N_GROUPS):
        den += jnp.exp(lg[g0 + i:g0 + i + 1] - g_max)
    p_grp = 1.0 / den

    cand = lg[0:EXPERTS_PER_GROUP]
    for g in range(1, N_GROUPS):
        cand = jnp.where(grp == g, lg[g * EXPERTS_PER_GROUP:(g + 1) * EXPERTS_PER_GROUP], cand)
    ridx = lax.broadcasted_iota(I32, (EXPERTS_PER_GROUP, tm), 0).astype(F32)
    none = float(EXPERTS_PER_GROUP)
    v1 = jnp.max(cand, axis=0, keepdims=True)
    i1 = jnp.min(jnp.where(cand == v1, ridx, none), axis=0, keepdims=True)
    rest = jnp.where(ridx == i1, -jnp.inf, cand)
    v2 = jnp.max(rest, axis=0, keepdims=True)
    i2 = jnp.min(jnp.where(rest == v2, ridx, none), axis=0, keepdims=True)
    e21 = jnp.exp(v2 - v1)
    gate1 = p_grp / (1.0 + e21)
    gate2 = p_grp * e21 / (1.0 + e21)
    ex1 = grp * EXPERTS_PER_GROUP + i1.astype(I32)
    ex2 = grp * EXPERTS_PER_GROUP + i2.astype(I32)

    eidx = lax.broadcasted_iota(I32, (N_EXPERTS, tm), 0)
    oh1 = jnp.where(eidx == ex1, 1.0, 0.0)
    oh2 = jnp.where(eidx == ex2, 1.0, 0.0)
    oh = (oh1 + oh2).astype(MXU_DTYPE)
    earlier = jnp.where(lax.broadcasted_iota(I32, (tm, tm), 0) < lax.broadcasted_iota(I32, (tm, tm), 1),
                        1.0, 0.0).astype(MXU_DTYPE)
    run = run_ref[...]
    before = _dot(oh, earlier) + jnp.concatenate([run] * (tm // LANES), axis=1)
    rank1 = jnp.sum(oh1 * before, axis=0, keepdims=True).astype(I32)
    rank2 = jnp.sum(oh2 * before, axis=0, keepdims=True).astype(I32)
    run = run + _dot(oh, jnp.ones((tm, LANES), MXU_DTYPE))
    run_ref[...] = run
    cnt_ref[...] = run
    ids_ref[...] = jnp.concatenate([ex1, ex2, rank1, rank2, jnp.zeros((4, tm), I32)], axis=0)
    gates_ref[...] = jnp.concatenate([gate1, gate2, jnp.zeros((6, tm), F32)], axis=0)


def _route(logits_t):
    t = logits_t.shape[1]
    tm = min(512, t)
    return pl.pallas_call(
        functools.partial(_route_kernel, tm=tm),
        grid=(t // tm,),
        in_specs=[pl.BlockSpec((ROUTER_ROWS, tm), lambda i: (0, i))],
        out_specs=[pl.BlockSpec((8, tm), lambda i: (0, i)), pl.BlockSpec((8, tm), lambda i: (0, i)),
                   pl.BlockSpec((N_EXPERTS, LANES), lambda i: (0, 0))],
        out_shape=[jax.ShapeDtypeStruct((8, t), I32), jax.ShapeDtypeStruct((8, t), F32),
                   jax.ShapeDtypeStruct((N_EXPERTS, LANES), F32)],
        scratch_shapes=[pltpu.VMEM((N_EXPERTS, LANES), F32)],
        compiler_params=_params("arbitrary"),
        name="route",
    )(logits_t)


def _row_copy(idx_ref, src_hbm, dst, sem, r):
    return pltpu.make_async_copy(src_hbm.at[pl.ds(idx_ref[0, 0, r], 1), :], dst.at[pl.ds(r, 1), :], sem)


def _gather_rows(idx_ref, src_hbm, dst, sem, n):
    def body(r, _):
        _row_copy(idx_ref, src_hbm, dst, sem, r).start()
        return 0

    lax.fori_loop(0, n, body, 0)


def _gather_rows_inline(idx_ref, src_hbm, dst, sem, n):
    for r in range(n):
        _row_copy(idx_ref, src_hbm, dst, sem, r).start()


def _wait_rows(src_hbm, dst, sem, n):
    pltpu.make_async_copy(src_hbm.at[pl.ds(0, n), :], dst, sem).wait()


def _expert_kernel(be_ref, idx_a_ref, idx_b_ref, idx_next_ref, x_hbm, w1a_ref, w3a_ref, w2a_ref, w1b_ref, w3b_ref,
                   w2b_ref, y_ref, xbuf_a, xbuf_b, sem, *, tm):
    i = pl.program_id(0)

    @pl.when(i == 0)
    def _():
        _gather_rows(idx_a_ref, x_hbm, xbuf_a, sem.at[0], tm)

    def block(cur, cur_sem, w1_ref, w3_ref, w2_ref, out_rows, nxt_idx_ref, nxt, nxt_sem):
        _wait_rows(x_hbm, cur, cur_sem, tm)
        _gather_rows_inline(nxt_idx_ref, x_hbm, nxt, nxt_sem, tm)
        xb = cur[...].astype(MXU_DTYPE)
        hidden = _silu(_dot(xb, w1_ref[0])) * _dot(xb, w3_ref[0])
        y_ref[out_rows, :] = _dot(hidden.astype(MXU_DTYPE), w2_ref[0])

    block(xbuf_a, sem.at[0], w1a_ref, w3a_ref, w2a_ref, slice(0, tm), idx_b_ref, xbuf_b, sem.at[1])
    block(xbuf_b, sem.at[1], w1b_ref, w3b_ref, w2b_ref, slice(tm, 2 * tm), idx_next_ref, xbuf_a, sem.at[0])

    @pl.when(i == pl.num_programs(0) - 1)
    def _():
        _wait_rows(x_hbm, xbuf_a, sem.at[0], tm)


def _experts(u_flat, slot_tok, blk_expert, w1, w3, w2, tm):
    t, d = u_flat.shape
    n_blk = slot_tok.shape[0] // tm
    assert n_blk % 2 == 0
    ff = w1.shape[-1]
    idx3 = slot_tok.reshape(n_blk, 1, tm)
    last = n_blk - 1

    def idx_spec(block_of):
        return pl.BlockSpec((1, 1, tm), lambda i, be: (block_of(i), 0, 0), memory_space=pltpu.SMEM)

    def weight_specs(which):
        return [pl.BlockSpec((1, d, ff), lambda i, be: (be[2 * i + which], 0, 0)),
                pl.BlockSpec((1, d, ff), lambda i, be: (be[2 * i + which], 0, 0)),
                pl.BlockSpec((1, ff, d), lambda i, be: (be[2 * i + which], 0, 0))]

    grid_spec = pltpu.PrefetchScalarGridSpec(
        num_scalar_prefetch=1,
        grid=(n_blk // 2,),
        in_specs=[idx_spec(lambda i: 2 * i), idx_spec(lambda i: 2 * i + 1),
                  idx_spec(lambda i: jnp.minimum(2 * i + 2, last)), pl.BlockSpec(memory_space=pl.ANY)]
        + weight_specs(0) + weight_specs(1),
        out_specs=pl.BlockSpec((2 * tm, d), lambda i, be: (i, 0)),
        scratch_shapes=[pltpu.VMEM((tm, d), F32), pltpu.VMEM((tm, d), F32), pltpu.SemaphoreType.DMA((2,))],
    )
    w1, w3, w2 = w1.astype(MXU_DTYPE), w3.astype(MXU_DTYPE), w2.astype(MXU_DTYPE)
    return pl.pallas_call(
        functools.partial(_expert_kernel, tm=tm),
        grid_spec=grid_spec,
        out_shape=jax.ShapeDtypeStruct((n_blk * tm, d), F32),
        compiler_params=_params("arbitrary"),
        name="experts",
    )(blk_expert, idx3, idx3, idx3, u_flat, w1, w3, w2, w1, w3, w2)


def _combine_kernel(d1_ref, d2_ref, d1n_ref, d2n_ref, y_hbm, gates_ref, x_ref, gate_ref, lng_ref, lnb_ref,
                    o_ref, ybuf, sem, *, tm, alpha):
    i = pl.program_id(0)
    n = pl.num_programs(0)
    slot = i % 2

    def wait(s):
        _wait_rows(y_hbm, ybuf.at[s, 0], sem.at[s, 0], tm)
        _wait_rows(y_hbm, ybuf.at[s, 1], sem.at[s, 1], tm)

    @pl.when(i == 0)
    def _():
        _gather_rows(d1_ref, y_hbm, ybuf.at[0, 0], sem.at[0, 0], tm)
        _gather_rows(d2_ref, y_hbm, ybuf.at[0, 1], sem.at[0, 1], tm)

    wait(slot)
    _gather_rows_inline(d1n_ref, y_hbm, ybuf.at[1 - slot, 0], sem.at[1 - slot, 0], tm)
    _gather_rows_inline(d2n_ref, y_hbm, ybuf.at[1 - slot, 1], sem.at[1 - slot, 1], tm)
    gates = gates_ref[...]
    f = gates[:, 0:1] * ybuf[slot, 0] + gates[:, 1:2] * ybuf[slot, 1]
    o_ref[...] = _layer_norm(alpha * x_ref[...] + gate_ref[0] * f, lng_ref[...], lnb_ref[...])

    @pl.when(i == n - 1)
    def _():
        wait(1 - slot)


def _combine(y_slots, dest1, dest2, gates, x_flat, gate1p, ln_g, ln_b, seq, alpha):
    t, d = x_flat.shape
    tm = min(256, seq)
    n_blk = t // tm
    per_seq = seq // tm
    d1 = dest1.reshape(n_blk, 1, tm)
    d2 = dest2.reshape(n_blk, 1, tm)
    cur = pl.BlockSpec((1, 1, tm), lambda i: (i, 0, 0), memory_space=pltpu.SMEM)
    nxt = pl.BlockSpec((1, 1, tm), lambda i: (jnp.minimum(i + 1, n_blk - 1), 0, 0), memory_space=pltpu.SMEM)
    row = pl.BlockSpec((1, d), lambda i: (0, 0))
    return pl.pallas_call(
        functools.partial(_combine_kernel, tm=tm, alpha=alpha),
        grid=(n_blk,),
        in_specs=[cur, cur, nxt, nxt, pl.BlockSpec(memory_space=pl.ANY),
                  pl.BlockSpec((tm, TOP_K), lambda i: (i, 0)),
                  pl.BlockSpec((tm, d), lambda i: (i, 0)),
                  pl.BlockSpec((1, 1, d), lambda i: (i // per_seq, 0, 0)), row, row],
        out_specs=pl.BlockSpec((tm, d), lambda i: (i, 0)),
        out_shape=jax.ShapeDtypeStruct((t, d), F32),
        scratch_shapes=[pltpu.VMEM((2, 2, tm, d), F32), pltpu.SemaphoreType.DMA((2, 2))],
        compiler_params=_params("arbitrary"),
        name="combine_norm",
    )(d1, d2, d1, d2, y_slots, gates, x_flat, gate1p, ln_g.reshape(1, d), ln_b.reshape(1, d))


EXPERT_TILE = 256
SLOT_STRIDE = 7919

def _moe(u, logits_t, x, gate2p, ln_g, ln_b, w1, w3, w2, alpha):
    b, s, d = x.shape
    t = b * s
    ids, gates8, cnt = _route(logits_t)
    counts = cnt[:, 0].astype(I32)
    padded = (counts + EXPERT_TILE - 1) // EXPERT_TILE * EXPERT_TILE
    pad_end = jnp.cumsum(padded)
    pad_start = pad_end - padded
    def slot(e, rank):
        return pad_start[e] + (rank * SLOT_STRIDE) % padded[e]

    dest1 = slot(ids[0], ids[2])
    dest2 = slot(ids[1], ids[3])
    n_slots = t * TOP_K + N_EXPERTS * EXPERT_TILE
    n_blk = n_slots // EXPERT_TILE
    tok = jnp.arange(t, dtype=I32)
    slot_tok = jnp.zeros((n_slots,), I32).at[jnp.concatenate([dest1, dest2])].set(jnp.concatenate([tok, tok]))
    blk_start = jnp.arange(n_blk, dtype=I32) * EXPERT_TILE
    blk_expert = jnp.minimum(jnp.sum((pad_end[None, :] <= blk_start[:, None]).astype(I32), axis=1), N_EXPERTS - 1)
    y_slots = _experts(u.reshape(t, d), slot_tok, blk_expert.astype(I32), w1, w3, w2, EXPERT_TILE)
    out = _combine(y_slots, dest1, dest2, gates8[:TOP_K].T, x.reshape(t, d), gate2p, ln_g, ln_b, s, alpha)
    return out.reshape(b, s, d)


def kernel(x, c, ln1_g, ln1_b, ln2_g, ln2_b, w_ada, b_ada, even_w_in, even_w_out, ret_gn_g, odd_w_in, odd_w_out, lambda_q1, lambda_k1, lambda_q2, lambda_k2, diff_subln_g, moe_w_group, moe_b_group, moe_w_router, moe_b_router, moe_w1, moe_w3, moe_w2):
    b, s, d = x.shape
    depth = w_ada.shape[0]
    alpha = (2.0 * depth) ** 0.25
    mod = _ada(c, w_ada, b_ada)
    for l in range(depth):
        sh1, sc1, g1, sh2, sc2, g2 = [m[:, None, :] for m in jnp.split(mod[l], 6, axis=-1)]
        i = l // 2
        if l % 2 == 0:
            w_in = even_w_in[i]
            w_main = jnp.concatenate([w_in[:, :2 * SB_WIDTH], w_in[:, 3 * SB_WIDTH:]], axis=1)
            proj, v_t = _inproj(x, 1.0 + sc1, sh1, w_main, w_in[:, 2 * SB_WIDTH:3 * SB_WIDTH])
            parts = [_sb_attention(proj, v_t), _retention(proj, ret_gn_g[i], 2 * SB_WIDTH // LANES)]
            w_out = even_w_out[i]
            weights = [w_out[:SB_WIDTH], w_out[SB_WIDTH:]]
        else:
            w_in = odd_w_in[i]
            proj, v_t = _inproj(x, 1.0 + sc1, sh1, w_in[:, :2 * DIFF_QK], w_in[:, 2 * DIFF_QK:])
            lambda_init = 0.8 - 0.6 * math.exp(-0.3 * l)
            lam_rows = jnp.stack([lambda_q1[i], lambda_k1[i], lambda_q2[i], lambda_k2[i]])
            parts = [_diff_attention(proj, v_t, lam_rows, diff_subln_g[i], lambda_init)]
            weights = [odd_w_out[i]]
        w_router_t = jnp.zeros((ROUTER_ROWS, d), F32).at[:N_EXPERTS].set(moe_w_router[l].T.astype(F32))
        w_router_t = w_router_t.at[N_EXPERTS:N_EXPERTS + N_GROUPS].set(moe_w_group[l].T.astype(F32))
        b_router = jnp.zeros((ROUTER_ROWS, 1), F32).at[:N_EXPERTS, 0].set(moe_b_router[l].astype(F32))
        b_router = b_router.at[N_EXPERTS:N_EXPERTS + N_GROUPS, 0].set(moe_b_group[l].astype(F32))
        x, u, logits_t = _outproj(parts, weights, x, 1.0 + g1, ln1_g[l], ln1_b[l], 1.0 + sc2, sh2,
                                  w_router_t, b_router, alpha)
        x = _moe(u, logits_t, x, 1.0 + g2, ln2_g[l], ln2_b[l], moe_w1[l], moe_w3[l], moe_w2[l], alpha)
    return x
```

```python
import functools
import math

import jax
import jax.numpy as jnp
import numpy as np
from jax import lax
from jax.experimental import pallas as pl
from jax.experimental.pallas import tpu as pltpu

F32 = jnp.float32
I32 = jnp.int32
MXU_DTYPE = jnp.bfloat16
HIGHEST = lax.Precision.HIGHEST
LOG2E = math.log2(math.e)

LN_EPS = 1e-5
CHUNK = 64
LANES = 128
SB_HEADS, SB_HEAD_DIM = 8, 64
RET_HEADS, RET_HEAD_DIM = 4, 128
DIFF_HEADS, DIFF_HEAD_DIM = 8, 64
SB_WIDTH = SB_HEADS * SB_HEAD_DIM
RET_WIDTH = RET_HEADS * RET_HEAD_DIM
DIFF_QK = DIFF_HEADS * 2 * DIFF_HEAD_DIM
N_GROUPS, EXPERTS_PER_GROUP = 4, 8
N_EXPERTS = N_GROUPS * EXPERTS_PER_GROUP
TOP_K = 2
ROUTER_ROWS = 40

VMEM_LIMIT = 56 * 1024 * 1024


def _params(*sem):
    return pltpu.CompilerParams(dimension_semantics=sem, vmem_limit_bytes=VMEM_LIMIT)


def _dot(a, b):
    return jnp.dot(a, b, preferred_element_type=F32)


def _dot_nt(a, b):
    return lax.dot_general(a, b, (((1,), (1,)), ((), ())), preferred_element_type=F32)


def _dot_tn(a, b):
    return lax.dot_general(a, b, (((0,), (0,)), ((), ())), preferred_element_type=F32)


def _silu(x):
    return x * (1.0 / (1.0 + jnp.exp(-x)))


def _ada_kernel(c_ref, w_ref, b_ref, o_ref):
    o_ref[0] = jnp.dot(_silu(c_ref[...]), w_ref[0], preferred_element_type=F32, precision=HIGHEST) + b_ref[0]


def _ada(c, w_ada, b_ada):
    depth, d, n = w_ada.shape
    bp = 8
    cp = jnp.zeros((bp, d), F32).at[: c.shape[0]].set(c)
    tn = 1536
    out = pl.pallas_call(
        _ada_kernel,
        grid=(depth, n // tn),
        in_specs=[
            pl.BlockSpec((bp, d), lambda l, j: (0, 0)),
            pl.BlockSpec((1, d, tn), lambda l, j: (l, 0, j)),
            pl.BlockSpec((1, 1, tn), lambda l, j: (l, 0, j)),
        ],
        out_specs=pl.BlockSpec((1, bp, tn), lambda l, j: (l, 0, j)),
        out_shape=jax.ShapeDtypeStruct((depth, bp, n), F32),
        compiler_params=_params("parallel", "parallel"),
        name="ada_mod",
    )(cp, w_ada, b_ada.reshape(depth, 1, n))
    return out[:, : c.shape[0]]


def _inproj_kernel(x_ref, sc_ref, sh_ref, w_ref, wvt_ref, o_ref, vt_ref, *, tn):
    u = (x_ref[0] * sc_ref[0] + sh_ref[0]).astype(MXU_DTYPE)
    for j in range(o_ref.shape[2] // tn):
        o_ref[0, :, j * tn:(j + 1) * tn] = _dot(u, w_ref[:, j * tn:(j + 1) * tn]).astype(o_ref.dtype)
    for j in range(vt_ref.shape[1] // tn):
        vt_ref[0, j * tn:(j + 1) * tn, :] = _dot_nt(wvt_ref[j * tn:(j + 1) * tn, :], u).astype(vt_ref.dtype)


def _inproj(x, scale1p, shift, w, w_v):
    b, s, d = x.shape
    n, n_v = w.shape[1], w_v.shape[1]
    tm = min(512, s)
    return pl.pallas_call(
        functools.partial(_inproj_kernel, tn=512),
        grid=(b, s // tm),
        in_specs=[
            pl.BlockSpec((1, tm, d), lambda i, j: (i, j, 0)),
            pl.BlockSpec((1, 1, d), lambda i, j: (i, 0, 0)),
            pl.BlockSpec((1, 1, d), lambda i, j: (i, 0, 0)),
            pl.BlockSpec((d, n), lambda i, j: (0, 0)),
            pl.BlockSpec((n_v, d), lambda i, j: (0, 0)),
        ],
        out_specs=[pl.BlockSpec((1, tm, n), lambda i, j: (i, j, 0)),
                   pl.BlockSpec((1, n_v, tm), lambda i, j: (i, 0, j))],
        out_shape=[jax.ShapeDtypeStruct((b, s, n), MXU_DTYPE), jax.ShapeDtypeStruct((b, n_v, s), MXU_DTYPE)],
        compiler_params=_params("parallel", "parallel"),
        name="in_proj",
    )(x, scale1p, shift, w.astype(MXU_DTYPE), w_v.T.astype(MXU_DTYPE))


MASKED = -float("inf")
ITEM_PLAIN, ITEM_DIAGONAL, ITEM_NULL = 0, 1, 2
FLAG_FIRST, FLAG_LAST = 1, 2


def _triangle_items(n_q, pad, diagonal_first, n_streams):
    streams = []
    for s in range(n_streams):
        items = []
        for qb in range(s, n_q, n_streams):
            order = range(qb, -1, -1) if diagonal_first else range(qb + 1)
            for n, kb in enumerate(order):
                flags = (FLAG_FIRST if n == 0 else 0) | (FLAG_LAST if n == qb else 0)
                items.append((qb, kb, ITEM_DIAGONAL if kb == qb else ITEM_PLAIN, flags))
        streams.append(items)
    n_items = max(len(items) for items in streams)
    null = (0, 0, ITEM_NULL, 0)
    rows = [np.asarray([null] * pad + items + [null] * (n_items - len(items) + pad + 1), np.int32).T
            for items in streams]
    return np.concatenate(rows, axis=0).copy(), n_items


SB_STAGES = 3
SB_STREAMS = 2


def _sb_kernel(tab_ref, q_ref, k_ref, vt_ref, o_ref, qh_s, acc_ref, carry_ref, mask_ref, y_s, yms_s, sums_s,
               *, tq, n_items):
    tk = tq
    extra = 16
    heads = range(2)
    streams = range(SB_STREAMS)
    lane = lax.broadcasted_iota(I32, (1, LANES), 1)
    key = lax.broadcasted_iota(I32, (tk, tq), 0)
    qry = lax.broadcasted_iota(I32, (tk, tq), 1)
    r = lax.broadcasted_iota(I32, (tk + extra, tk), 0)
    c = lax.broadcasted_iota(I32, (tk + extra, tk), 1)
    neg_later = jnp.where(r >= tk, -1.0, jnp.where(c > r, -1.0, 0.0)).astype(MXU_DTYPE)
    scale2 = SB_HEAD_DIM ** -0.5 * LOG2E

    def prepare_queries(blk, _):
        rows = pl.ds(pl.multiple_of(blk * tq, tq), tq)
        q2 = q_ref[0, rows, :]
        for h in heads:
            qh_s[h, rows, :] = (jnp.where((lane // SB_HEAD_DIM) == h, q2, jnp.zeros_like(q2)).astype(F32)
                                * scale2).astype(MXU_DTYPE)
        return 0

    lax.fori_loop(0, q_ref.shape[1] // tq, prepare_queries, 0)
    acc_ref[...] = jnp.zeros_like(acc_ref)
    carry_ref[...] = jnp.zeros_like(carry_ref)
    mask_ref[ITEM_PLAIN] = jnp.zeros((tk, tq), F32)
    mask_ref[ITEM_DIAGONAL] = jnp.where(key < qry, 0.0, MASKED)
    mask_ref[ITEM_NULL] = jnp.full((tk, tq), MASKED, F32)
    y_s[...] = jnp.zeros_like(y_s)
    yms_s[...] = jnp.full(yms_s.shape, MASKED, F32)
    sums_s[...] = jnp.zeros_like(sums_s)

    def rows_of(block, size):
        return pl.ds(pl.multiple_of(block * size, size), size)

    def trip(it, parity):
        col_x, col_y, col_c = it + 2, it + 1, it
        for s in streams:
            kj = k_ref[0, rows_of(tab_ref[4 * s + 1, col_x], tk), :]
            qrows = rows_of(tab_ref[4 * s, col_x], tq)
            for h in heads:
                y_s[2 * s + h, parity] = _dot_nt(kj, qh_s[h, qrows, :])
        for s in streams:
            mask = mask_ref[tab_ref[4 * s + 2, col_y]]
            for h in heads:
                ym = y_s[2 * s + h, 1 - parity] + mask
                sp = jnp.maximum(ym, jnp.log2(1.0 + jnp.exp2(jnp.minimum(ym, 126.0))))
                yms_s[2 * s + h, parity] = ym - sp
                sums_s[2 * s + h, parity] = _dot(neg_later, sp.astype(MXU_DTYPE))
        for s in streams:
            vtj = vt_ref[0, :, rows_of(tab_ref[4 * s + 1, col_c], tk)]
            keep = jnp.where((tab_ref[4 * s + 3, col_c] & FLAG_FIRST) != 0, 0.0, 1.0)
            for h in heads:
                sums = sums_s[2 * s + h, 1 - parity]
                carry = carry_ref[2 * s + h] * keep
                w = jnp.exp2(yms_s[2 * s + h, 1 - parity] + sums[:tk] + carry).astype(MXU_DTYPE)
                acc_ref[2 * s + h] = acc_ref[2 * s + h] * keep + _dot(vtj, w)
                carry_ref[2 * s + h] = carry + sums[tk:tk + 1]

        for s in streams:
            @pl.when((tab_ref[4 * s + 3, col_c] & FLAG_LAST) != 0)
            def _(s=s):
                sub = lax.broadcasted_iota(I32, (LANES, 1), 0)
                o_ref[0, rows_of(tab_ref[4 * s, col_c], tq), :] = jnp.where(
                    sub < SB_HEAD_DIM, acc_ref[2 * s], acc_ref[2 * s + 1]).T.astype(o_ref.dtype)

    def trip_pair(i, _):
        trip(2 * i, 0)
        trip(2 * i + 1, 1)
        return 0

    lax.fori_loop(0, pl.cdiv(n_items + SB_STAGES - 1, 2), trip_pair, 0)


def _sb_attention(proj, v_t):
    b, s, _ = proj.shape
    tq = min(256, s)
    n_pairs = SB_WIDTH // LANES
    table, n_items = _triangle_items(s // tq, SB_STAGES - 1, diagonal_first=True, n_streams=SB_STREAMS)
    n_chains = 2 * SB_STREAMS
    grid_spec = pltpu.PrefetchScalarGridSpec(
        num_scalar_prefetch=1,
        grid=(b, n_pairs),
        in_specs=[
            pl.BlockSpec((1, s, LANES), lambda i, p, t: (i, 0, p)),
            pl.BlockSpec((1, s, LANES), lambda i, p, t: (i, 0, n_pairs + p)),
            pl.BlockSpec((1, LANES, s), lambda i, p, t: (i, p, 0)),
        ],
        out_specs=pl.BlockSpec((1, s, LANES), lambda i, p, t: (i, 0, p)),
        scratch_shapes=[
            pltpu.VMEM((2, s, LANES), MXU_DTYPE),
            pltpu.VMEM((n_chains, LANES, tq), F32),
            pltpu.VMEM((n_chains, 1, tq), F32),
            pltpu.VMEM((3, tq, tq), F32),
            pltpu.VMEM((n_chains, 2, tq, tq), F32),
            pltpu.VMEM((n_chains, 2, tq, tq), F32),
            pltpu.VMEM((n_chains, 2, tq + 16, tq), F32),
        ],
    )
    return pl.pallas_call(
        functools.partial(_sb_kernel, tq=tq, n_items=n_items),
        grid_spec=grid_spec,
        out_shape=jax.ShapeDtypeStruct((b, s, SB_WIDTH), MXU_DTYPE),
        compiler_params=_params("parallel", "parallel"),
        name="sb_attention",
    )(jnp.asarray(table), proj, proj, v_t)


def _ret_kernel(lg_ref, q_ref, k_ref, v_ref, g_ref, gn_ref, o_ref, state_ref, *, tb):
    h = pl.program_id(1)
    blk = pl.program_id(2)

    @pl.when(blk == 0)
    def _():
        state_ref[...] = jnp.zeros_like(state_ref)

    lg = lg_ref[h]
    scale = RET_HEAD_DIM ** -0.5
    q = q_ref[0].astype(F32)
    k = k_ref[0].astype(F32)
    v = v_ref[0]
    row = lax.broadcasted_iota(I32, (tb, tb), 0)
    col = lax.broadcasted_iota(I32, (tb, tb), 1)
    dist = jnp.abs(row - col).astype(F32)
    decay = jnp.where(col // CHUNK <= row // CHUNK, jnp.exp(lg * dist) * scale, 0.0)
    pos = lax.broadcasted_iota(I32, (tb, 1), 0).astype(F32)
    scores = _dot_nt(q.astype(MXU_DTYPE), k.astype(MXU_DTYPE)) * decay
    intra = _dot(scores.astype(MXU_DTYPE), v)
    state = state_ref[...]
    q_in = (q * jnp.exp(lg * (pos + 1.0))).astype(MXU_DTYPE)
    inter = _dot(q_in, state.astype(MXU_DTYPE))
    k_out = (k * (jnp.exp(lg * (tb - 1.0 - pos)) * scale)).astype(MXU_DTYPE)
    block_decay = jnp.exp(lg * jnp.full((1, RET_HEAD_DIM), float(tb), F32))
    state_ref[...] = block_decay * state + _dot_tn(k_out, v)
    o = intra + inter
    mu = jnp.mean(o, axis=-1, keepdims=True)
    oc = o - mu
    var = jnp.mean(oc * oc, axis=-1, keepdims=True)
    o = oc * lax.rsqrt(var + LN_EPS) * gn_ref[0] * _silu(g_ref[0].astype(F32))
    o_ref[0] = o.astype(o_ref.dtype)


def _retention(proj, gn_gain, first_col_block):
    b, s, _ = proj.shape
    tb = min(256, s)
    log_gamma = jnp.log1p(-jnp.exp2(-5.0 - jnp.arange(RET_HEADS, dtype=F32)))

    def col(which):
        return lambda i, h, j, lg: (i, j, first_col_block + which * RET_HEADS + h)

    grid_spec = pltpu.PrefetchScalarGridSpec(
        num_scalar_prefetch=1,
        grid=(b, RET_HEADS, s // tb),
        in_specs=[pl.BlockSpec((1, tb, LANES), col(w)) for w in range(4)]
        + [pl.BlockSpec((1, 1, LANES), lambda i, h, j, lg: (h, 0, 0))],
        out_specs=pl.BlockSpec((1, tb, LANES), lambda i, h, j, lg: (i, j, h)),
        scratch_shapes=[pltpu.VMEM((RET_HEAD_DIM, RET_HEAD_DIM), F32)],
    )
    return pl.pallas_call(
        functools.partial(_ret_kernel, tb=tb),
        grid_spec=grid_spec,
        out_shape=jax.ShapeDtypeStruct((b, s, RET_WIDTH), MXU_DTYPE),
        compiler_params=_params("parallel", "parallel", "arbitrary"),
        name="retention",
    )(log_gamma, proj, proj, proj, proj, gn_gain.astype(F32).reshape(RET_HEADS, 1, RET_HEAD_DIM))


DIFF_STAGES = 3
DIFF_HEADS_PER_STEP = 2
DIFF_STREAMS = 1


def _diff_kernel(slope_ref, tab_ref, q_ref, k_ref, vt_ref, lam_ref, gain_ref, o_ref, qs_s, m_ref, l_ref, acc_ref,
                 bias_ref, z_s, p_s, a_s, lfin_s, *, tq, n_items, lambda_init):
    hp = pl.program_id(1)
    tk = tq
    nh = DIFF_HEADS_PER_STEP
    dv = LANES
    heads = range(nh)
    streams = range(DIFF_STREAMS)
    lane = lax.broadcasted_iota(I32, (1, LANES), 1)
    scale2 = DIFF_HEAD_DIM ** -0.5 * LOG2E
    slope2 = [slope_ref[hp * nh + hh] * LOG2E for hh in heads]

    def rows_of(block, size):
        return pl.ds(pl.multiple_of(block * size, size), size)

    def prepare_queries(blk, _):
        for hh in heads:
            q2 = q_ref[0, rows_of(blk, tq), hh * LANES:(hh + 1) * LANES]
            zero = jnp.zeros_like(q2)
            stacked = jnp.concatenate([jnp.where(lane < DIFF_HEAD_DIM, q2, zero),
                                       jnp.where(lane >= DIFF_HEAD_DIM, q2, zero)], axis=0)
            qs_s[hh, rows_of(blk, 2 * tq), :] = (stacked.astype(F32) * scale2).astype(MXU_DTYPE)
        return 0

    lax.fori_loop(0, q_ref.shape[1] // tq, prepare_queries, 0)
    key = lax.broadcasted_iota(I32, (tk, tq), 0)
    qry = lax.broadcasted_iota(I32, (tk, tq), 1)
    visible = key // CHUNK <= qry // CHUNK
    for hh in heads:
        plain = slope2[hh] * key.astype(F32)
        diag = jnp.where(visible, slope2[hh] * (qry - jnp.abs(qry - key)).astype(F32), MASKED)
        bias_ref[hh, ITEM_PLAIN] = jnp.concatenate([plain, plain], axis=1)
        bias_ref[hh, ITEM_DIAGONAL] = jnp.concatenate([diag, diag], axis=1)
    m_ref[...] = jnp.zeros_like(m_ref)
    l_ref[...] = jnp.zeros_like(l_ref)
    acc_ref[...] = jnp.zeros_like(acc_ref)
    z_s[...] = jnp.zeros_like(z_s)
    p_s[...] = jnp.zeros_like(p_s)
    a_s[...] = jnp.ones_like(a_s)
    lfin_s[...] = jnp.ones_like(lfin_s)

    def trip(it, parity):
        col_a, col_a1, col_c = it + 2, it + 1, it
        for s in streams:
            krows = rows_of(tab_ref[4 * s + 1, col_a], tk)
            qrows = rows_of(tab_ref[4 * s, col_a], 2 * tq)
            for hh in heads:
                z_s[nh * s + hh, parity] = _dot_nt(k_ref[0, krows, hh * LANES:(hh + 1) * LANES], qs_s[hh, qrows, :])
        lfin = {}
        for s in streams:
            vrows = rows_of(tab_ref[4 * s + 1, col_c], tk)
            for hh in heads:
                c = nh * s + hh
                acc_ref[c] = a_s[c] * acc_ref[c] + _dot(vt_ref[0, hh * dv:(hh + 1) * dv, vrows], p_s[c])
                lfin[c] = lfin_s[c]
        for s in streams:
            kind = tab_ref[4 * s + 2, col_a1]
            first = (tab_ref[4 * s + 3, col_a1] & FLAG_FIRST) != 0
            offset = ((tab_ref[4 * s + 1, col_a1] - tab_ref[4 * s, col_a1]) * tq).astype(F32)
            for hh in heads:
                c = nh * s + hh
                shift = jnp.where(kind == ITEM_PLAIN, slope2[hh] * offset,
                                  jnp.where(kind == ITEM_DIAGONAL, 0.0, MASKED))
                z = z_s[c, 1 - parity] + bias_ref[hh, jnp.minimum(kind, ITEM_DIAGONAL)]
                m_old = jnp.where(first, MASKED, m_ref[c])
                m_new = jnp.maximum(m_old, jnp.max(z, axis=0, keepdims=True) + shift)
                p = jnp.exp2(z - (m_new - shift))
                a = jnp.exp2(m_old - m_new)
                l_new = a * l_ref[c] + jnp.sum(p, axis=0, keepdims=True)
                p_s[c] = p.astype(MXU_DTYPE)
                a_s[c] = a
                lfin_s[c] = l_new
                l_ref[c] = l_new
                m_ref[c] = m_new

        for s in streams:
            @pl.when((tab_ref[4 * s + 3, col_c] & FLAG_LAST) != 0)
            def _(s=s):
                lam_v = lam_ref[...]
                lam = (jnp.exp(jnp.sum(lam_v[0:1] * lam_v[1:2], axis=-1, keepdims=True))
                       - jnp.exp(jnp.sum(lam_v[2:3] * lam_v[3:4], axis=-1, keepdims=True)) + lambda_init)
                orows = rows_of(tab_ref[4 * s, col_c], tq)
                for hh in heads:
                    c = nh * s + hh
                    o = acc_ref[c] * (1.0 / lfin[c])
                    o = o[:, :tq] - lam * o[:, tq:]
                    o = o * lax.rsqrt(jnp.mean(o * o, axis=0, keepdims=True) + LN_EPS)
                    o_ref[0, orows, hh * dv:(hh + 1) * dv] = (o * gain_ref[...] * (1.0 - lambda_init)).T.astype(
                        o_ref.dtype)

    def trip_pair(i, _):
        trip(2 * i, 0)
        trip(2 * i + 1, 1)
        return 0

    lax.fori_loop(0, pl.cdiv(n_items + DIFF_STAGES - 1, 2), trip_pair, 0)


def _diff_attention(proj, v_t, lam_rows, subln_gain, lambda_init):
    b, s, _ = proj.shape
    tq = min(256, s)
    dv = subln_gain.shape[-1]
    nh = DIFF_HEADS_PER_STEP
    slopes = jnp.exp2(-8.0 / DIFF_HEADS * (jnp.arange(DIFF_HEADS, dtype=F32) + 1.0))
    kb = DIFF_QK // (nh * LANES)
    table, n_items = _triangle_items(s // tq, DIFF_STAGES - 1, diagonal_first=False, n_streams=DIFF_STREAMS)
    n_chains = nh * DIFF_STREAMS
    grid_spec = pltpu.PrefetchScalarGridSpec(
        num_scalar_prefetch=2,
        grid=(b, DIFF_HEADS // nh),
        in_specs=[
            pl.BlockSpec((1, s, nh * LANES), lambda i, h, sl, t: (i, 0, h)),
            pl.BlockSpec((1, s, nh * LANES), lambda i, h, sl, t: (i, 0, kb + h)),
            pl.BlockSpec((1, nh * dv, s), lambda i, h, sl, t: (i, h, 0)),
            pl.BlockSpec((4, DIFF_HEAD_DIM), lambda i, h, sl, t: (0, 0)),
            pl.BlockSpec((dv, 1), lambda i, h, sl, t: (0, 0)),
        ],
        out_specs=pl.BlockSpec((1, s, nh * dv), lambda i, h, sl, t: (i, 0, h)),
        scratch_shapes=[
            pltpu.VMEM((nh, 2 * s, LANES), MXU_DTYPE),
            pltpu.VMEM((n_chains, 1, 2 * tq), F32),
            pltpu.VMEM((n_chains, 1, 2 * tq), F32),
            pltpu.VMEM((n_chains, dv, 2 * tq), F32),
            pltpu.VMEM((nh, 2, tq, 2 * tq), F32),
            pltpu.VMEM((n_chains, 2, tq, 2 * tq), F32),
            pltpu.VMEM((n_chains, tq, 2 * tq), MXU_DTYPE),
            pltpu.VMEM((n_chains, 1, 2 * tq), F32),
            pltpu.VMEM((n_chains, 1, 2 * tq), F32),
        ],
    )
    return pl.pallas_call(
        functools.partial(_diff_kernel, tq=tq, n_items=n_items, lambda_init=lambda_init),
        grid_spec=grid_spec,
        out_shape=jax.ShapeDtypeStruct((b, s, DIFF_HEADS * dv), MXU_DTYPE),
        compiler_params=_params("parallel", "parallel"),
        name="diff_attention",
    )(slopes, jnp.asarray(table), proj, proj, v_t, lam_rows.astype(F32), subln_gain.astype(F32).reshape(dv, 1))


def _layer_norm(y, g, b):
    mu = jnp.mean(y, axis=-1, keepdims=True)
    yc = y - mu
    var = jnp.mean(yc * yc, axis=-1, keepdims=True)
    return yc * lax.rsqrt(var + LN_EPS) * g + b


def _outproj_kernel(*refs, n_in, alpha):
    a_refs, w_refs = refs[:n_in], refs[n_in:2 * n_in]
    x_ref, gate_ref, lng_ref, lnb_ref, sc_ref, sh_ref, wr_ref, br_ref, xo_ref, u_ref, lg_ref = refs[2 * n_in:]
    mix = _dot(a_refs[0][0], w_refs[0][...])
    for a_ref, w_ref in zip(a_refs[1:], w_refs[1:]):
        mix += _dot(a_ref[0], w_ref[...])
    xn = _layer_norm(alpha * x_ref[0] + gate_ref[0] * mix, lng_ref[...], lnb_ref[...])
    xo_ref[0] = xn
    u = xn * sc_ref[0] + sh_ref[0]
    u_ref[0] = u
    lg_ref[...] = lax.dot_general(wr_ref[...], u, (((1,), (1,)), ((), ())), preferred_element_type=F32,
                                  precision=HIGHEST) + br_ref[...]


def _outproj(parts, weights, x, gate1p, ln_g, ln_b, scale1p, shift, w_router_t, b_router, alpha):
    b, s, d = x.shape
    tm = min(256, s)
    n_in = len(parts)
    vec = pl.BlockSpec((1, 1, d), lambda i, j: (i, 0, 0))
    row = pl.BlockSpec((1, d), lambda i, j: (0, 0))
    in_specs = [pl.BlockSpec((1, tm, p.shape[-1]), lambda i, j: (i, j, 0)) for p in parts]
    in_specs += [pl.BlockSpec(w.shape, lambda i, j: (0, 0)) for w in weights]
    in_specs += [pl.BlockSpec((1, tm, d), lambda i, j: (i, j, 0)), vec, row, row, vec, vec,
                 pl.BlockSpec((ROUTER_ROWS, d), lambda i, j: (0, 0)),
                 pl.BlockSpec((ROUTER_ROWS, 1), lambda i, j: (0, 0))]
    nb = s // tm
    return pl.pallas_call(
        functools.partial(_outproj_kernel, n_in=n_in, alpha=alpha),
        grid=(b, nb),
        in_specs=in_specs,
        out_specs=[pl.BlockSpec((1, tm, d), lambda i, j: (i, j, 0)),
                   pl.BlockSpec((1, tm, d), lambda i, j: (i, j, 0)),
                   pl.BlockSpec((ROUTER_ROWS, tm), lambda i, j: (0, i * nb + j))],
        out_shape=[jax.ShapeDtypeStruct((b, s, d), F32), jax.ShapeDtypeStruct((b, s, d), F32),
                   jax.ShapeDtypeStruct((ROUTER_ROWS, b * s), F32)],
        compiler_params=_params("parallel", "parallel"),
        name="out_proj_norm",
    )(*parts, *[w.astype(MXU_DTYPE) for w in weights], x, gate1p, ln_g.reshape(1, d), ln_b.reshape(1, d),
      scale1p, shift, w_router_t, b_router)


def _route_kernel(lg_ref, ids_ref, gates_ref, cnt_ref, run_ref, *, tm):
    @pl.when(pl.program_id(0) == 0)
    def _():
        run_ref[...] = jnp.zeros_like(run_ref)

    lg = lg_ref[...]
    g0 = N_EXPERTS
    g_max = lg[g0:g0 + 1]
    grp = jnp.zeros((1, tm), I32)
    for i in range(1, N_GROUPS):
        gi = lg[g0 + i:g0 + i + 1]
        better = gi > g_max
        grp = jnp.where(better, i, grp)
        g_max = jnp.where(better, gi, g_max)
    den = jnp.exp(lg[g0:g0 + 1] - g_max)
    for i in range(1, N_GROUPS):
        den += jnp.exp(lg[g0 + i:g0 + i + 1] - g_max)
    p_grp = 1.0 / den

    cand = lg[0:EXPERTS_PER_GROUP]
    for g in range(1, N_GROUPS):
        cand = jnp.where(grp == g, lg[g * EXPERTS_PER_GROUP:(g + 1) * EXPERTS_PER_GROUP], cand)
    ridx = lax.broadcasted_iota(I32, (EXPERTS_PER_GROUP, tm), 0).astype(F32)
    none = float(EXPERTS_PER_GROUP)
    v1 = jnp.max(cand, axis=0, keepdims=True)
    i1 = jnp.min(jnp.where(cand == v1, ridx, none), axis=0, keepdims=True)
    rest = jnp.where(ridx == i1, -jnp.inf, cand)
    v2 = jnp.max(rest, axis=0, keepdims=True)
    i2 = jnp.min(jnp.where(rest == v2, ridx, none), axis=0, keepdims=True)
    e21 = jnp.exp(v2 - v1)
    gate1 = p_grp / (1.0 + e21)
    gate2 = p_grp * e21 / (1.0 + e21)
    ex1 = grp * EXPERTS_PER_GROUP + i1.astype(I32)
    ex2 = grp * EXPERTS_PER_GROUP + i2.astype(I32)

    eidx = lax.broadcasted_iota(I32, (N_EXPERTS, tm), 0)
    oh1 = jnp.where(eidx == ex1, 1.0, 0.0)
    oh2 = jnp.where(eidx == ex2, 1.0, 0.0)
    oh = (oh1 + oh2).astype(MXU_DTYPE)
    earlier = jnp.where(lax.broadcasted_iota(I32, (tm, tm), 0) < lax.broadcasted_iota(I32, (tm, tm), 1),
                        1.0, 0.0).astype(MXU_DTYPE)
    run = run_ref[...]
    before = _dot(oh, earlier) + jnp.concatenate([run] * (tm // LANES), axis=1)
    rank1 = jnp.sum(oh1 * before, axis=0, keepdims=True).astype(I32)
    rank2 = jnp.sum(oh2 * before, axis=0, keepdims=True).astype(I32)
    run = run + _dot(oh, jnp.ones((tm, LANES), MXU_DTYPE))
    run_ref[...] = run
    cnt_ref[...] = run
    ids_ref[...] = jnp.concatenate([ex1, ex2, rank1, rank2, jnp.zeros((4, tm), I32)], axis=0)
    gates_ref[...] = jnp.concatenate([gate1, gate2, jnp.zeros((6, tm), F32)], axis=0)


def _route(logits_t):
    t = logits_t.shape[1]
    tm = min(512, t)
    return pl.pallas_call(
        functools.partial(_route_kernel, tm=tm),
        grid=(t // tm,),
        in_specs=[pl.BlockSpec((ROUTER_ROWS, tm), lambda i: (0, i))],
        out_specs=[pl.BlockSpec((8, tm), lambda i: (0, i)), pl.BlockSpec((8, tm), lambda i: (0, i)),
                   pl.BlockSpec((N_EXPERTS, LANES), lambda i: (0, 0))],
        out_shape=[jax.ShapeDtypeStruct((8, t), I32), jax.ShapeDtypeStruct((8, t), F32),
                   jax.ShapeDtypeStruct((N_EXPERTS, LANES), F32)],
        scratch_shapes=[pltpu.VMEM((N_EXPERTS, LANES), F32)],
        compiler_params=_params("arbitrary"),
        name="route",
    )(logits_t)


def _row_copy(idx_ref, src_hbm, dst, sem, r):
    return pltpu.make_async_copy(src_hbm.at[pl.ds(idx_ref[0, 0, r], 1), :], dst.at[pl.ds(r, 1), :], sem)


def _gather_rows(idx_ref, src_hbm, dst, sem, n):
    def body(r, _):
        _row_copy(idx_ref, src_hbm, dst, sem, r).start()
        return 0

    lax.fori_loop(0, n, body, 0)


def _gather_rows_inline(idx_ref, src_hbm, dst, sem, n):
    for r in range(n):
        _row_copy(idx_ref, src_hbm, dst, sem, r).start()


def _wait_rows(src_hbm, dst, sem, n):
    pltpu.make_async_copy(src_hbm.at[pl.ds(0, n), :], dst, sem).wait()


def _expert_kernel(be_ref, idx_a_ref, idx_b_ref, idx_next_ref, x_hbm, w1a_ref, w3a_ref, w2a_ref, w1b_ref, w3b_ref,
                   w2b_ref, y_ref, xbuf_a, xbuf_b, sem, *, tm):
    i = pl.program_id(0)

    @pl.when(i == 0)
    def _():
        _gather_rows(idx_a_ref, x_hbm, xbuf_a, sem.at[0], tm)

    def block(cur, cur_sem, w1_ref, w3_ref, w2_ref, out_rows, nxt_idx_ref, nxt, nxt_sem):
        _gather_rows_inline(nxt_idx_ref, x_hbm, nxt, nxt_sem, tm)
        _wait_rows(x_hbm, cur, cur_sem, tm)
        xb = cur[...].astype(MXU_DTYPE)
        hidden = _silu(_dot(xb, w1_ref[0])) * _dot(xb, w3_ref[0])
        y_ref[out_rows, :] = _dot(hidden.astype(MXU_DTYPE), w2_ref[0])

    block(xbuf_a, sem.at[0], w1a_ref, w3a_ref, w2a_ref, slice(0, tm), idx_b_ref, xbuf_b, sem.at[1])
    block(xbuf_b, sem.at[1], w1b_ref, w3b_ref, w2b_ref, slice(tm, 2 * tm), idx_next_ref, xbuf_a, sem.at[0])

    @pl.when(i == pl.num_programs(0) - 1)
    def _():
        _wait_rows(x_hbm, xbuf_a, sem.at[0], tm)


def _experts(u_flat, slot_tok, blk_expert, w1, w3, w2, tm):
    t, d = u_flat.shape
    n_blk = slot_tok.shape[0] // tm
    assert n_blk % 2 == 0
    ff = w1.shape[-1]
    idx3 = slot_tok.reshape(n_blk, 1, tm)
    last = n_blk - 1

    def idx_spec(block_of):
        return pl.BlockSpec((1, 1, tm), lambda i, be: (block_of(i), 0, 0), memory_space=pltpu.SMEM)

    def weight_specs(which):
        return [pl.BlockSpec((1, d, ff), lambda i, be: (be[2 * i + which], 0, 0)),
                pl.BlockSpec((1, d, ff), lambda i, be: (be[2 * i + which], 0, 0)),
                pl.BlockSpec((1, ff, d), lambda i, be: (be[2 * i + which], 0, 0))]

    grid_spec = pltpu.PrefetchScalarGridSpec(
        num_scalar_prefetch=1,
        grid=(n_blk // 2,),
        in_specs=[idx_spec(lambda i: 2 * i), idx_spec(lambda i: 2 * i + 1),
                  idx_spec(lambda i: jnp.minimum(2 * i + 2, last)), pl.BlockSpec(memory_space=pl.ANY)]
        + weight_specs(0) + weight_specs(1),
        out_specs=pl.BlockSpec((2 * tm, d), lambda i, be: (i, 0)),
        scratch_shapes=[pltpu.VMEM((tm, d), F32), pltpu.VMEM((tm, d), F32), pltpu.SemaphoreType.DMA((2,))],
    )
    w1, w3, w2 = w1.astype(MXU_DTYPE), w3.astype(MXU_DTYPE), w2.astype(MXU_DTYPE)
    return pl.pallas_call(
        functools.partial(_expert_kernel, tm=tm),
        grid_spec=grid_spec,
        out_shape=jax.ShapeDtypeStruct((n_blk * tm, d), F32),
        compiler_params=_params("arbitrary"),
        name="experts",
    )(blk_expert, idx3, idx3, idx3, u_flat, w1, w3, w2, w1, w3, w2)


def _combine_kernel(d1_ref, d2_ref, d1n_ref, d2n_ref, y_hbm, gates_ref, x_ref, gate_ref, lng_ref, lnb_ref,
                    o_ref, ybuf, sem, *, tm, alpha):
    i = pl.program_id(0)
    n = pl.num_programs(0)
    slot = i % 2

    def wait(s):
        _wait_rows(y_hbm, ybuf.at[s, 0], sem.at[s, 0], tm)
        _wait_rows(y_hbm, ybuf.at[s, 1], sem.at[s, 1], tm)

    @pl.when(i == 0)
    def _():
        _gather_rows(d1_ref, y_hbm, ybuf.at[0, 0], sem.at[0, 0], tm)
        _gather_rows(d2_ref, y_hbm, ybuf.at[0, 1], sem.at[0, 1], tm)

    _gather_rows_inline(d1n_ref, y_hbm, ybuf.at[1 - slot, 0], sem.at[1 - slot, 0], tm)
    _gather_rows_inline(d2n_ref, y_hbm, ybuf.at[1 - slot, 1], sem.at[1 - slot, 1], tm)
    wait(slot)
    gates = gates_ref[...]
    f = gates[:, 0:1] * ybuf[slot, 0] + gates[:, 1:2] * ybuf[slot, 1]
    o_ref[...] = _layer_norm(alpha * x_ref[...] + gate_ref[0] * f, lng_ref[...], lnb_ref[...])

    @pl.when(i == n - 1)
    def _():
        wait(1 - slot)


def _combine(y_slots, dest1, dest2, gates, x_flat, gate1p, ln_g, ln_b, seq, alpha):
    t, d = x_flat.shape
    tm = min(256, seq)
    n_blk = t // tm
    per_seq = seq // tm
    d1 = dest1.reshape(n_blk, 1, tm)
    d2 = dest2.reshape(n_blk, 1, tm)
    cur = pl.BlockSpec((1, 1, tm), lambda i: (i, 0, 0), memory_space=pltpu.SMEM)
    nxt = pl.BlockSpec((1, 1, tm), lambda i: (jnp.minimum(i + 1, n_blk - 1), 0, 0), memory_space=pltpu.SMEM)
    row = pl.BlockSpec((1, d), lambda i: (0, 0))
    return pl.pallas_call(
        functools.partial(_combine_kernel, tm=tm, alpha=alpha),
        grid=(n_blk,),
        in_specs=[cur, cur, nxt, nxt, pl.BlockSpec(memory_space=pl.ANY),
                  pl.BlockSpec((tm, TOP_K), lambda i: (i, 0)),
                  pl.BlockSpec((tm, d), lambda i: (i, 0)),
                  pl.BlockSpec((1, 1, d), lambda i: (i // per_seq, 0, 0)), row, row],
        out_specs=pl.BlockSpec((tm, d), lambda i: (i, 0)),
        out_shape=jax.ShapeDtypeStruct((t, d), F32),
        scratch_shapes=[pltpu.VMEM((2, 2, tm, d), F32), pltpu.SemaphoreType.DMA((2, 2))],
        compiler_params=_params("arbitrary"),
        name="combine_norm",
    )(d1, d2, d1, d2, y_slots, gates, x_flat, gate1p, ln_g.reshape(1, d), ln_b.reshape(1, d))


EXPERT_TILE = 256
SLOT_STRIDE = 7919

def _moe(u, logits_t, x, gate2p, ln_g, ln_b, w1, w3, w2, alpha):
    b, s, d = x.shape
    t = b * s
    ids, gates8, cnt = _route(logits_t)
    counts = cnt[:, 0].astype(I32)
    padded = (counts + EXPERT_TILE - 1) // EXPERT_TILE * EXPERT_TILE
    pad_end = jnp.cumsum(padded)
    pad_start = pad_end - padded
    def slot(e, rank):
        return pad_start[e] + (rank * SLOT_STRIDE) % padded[e]

    dest1 = slot(ids[0], ids[2])
    dest2 = slot(ids[1], ids[3])
    n_slots = t * TOP_K + N_EXPERTS * EXPERT_TILE
    n_blk = n_slots // EXPERT_TILE
    tok = jnp.arange(t, dtype=I32)
    slot_tok = jnp.zeros((n_slots,), I32).at[jnp.concatenate([dest1, dest2])].set(jnp.concatenate([tok, tok]))
    blk_start = jnp.arange(n_blk, dtype=I32) * EXPERT_TILE
    blk_expert = jnp.minimum(jnp.sum((pad_end[None, :] <= blk_start[:, None]).astype(I32), axis=1), N_EXPERTS - 1)
    y_slots = _experts(u.reshape(t, d), slot_tok, blk_expert.astype(I32), w1, w3, w2, EXPERT_TILE)
    out = _combine(y_slots, dest1, dest2, gates8[:TOP_K].T, x.reshape(t, d), gate2p, ln_g, ln_b, s, alpha)
    return out.reshape(b, s, d)


def kernel(x, c, ln1_g, ln1_b, ln2_g, ln2_b, w_ada, b_ada, even_w_in, even_w_out, ret_gn_g, odd_w_in, odd_w_out, lambda_q1, lambda_k1, lambda_q2, lambda_k2, diff_subln_g, moe_w_group, moe_b_group, moe_w_router, moe_b_router, moe_w1, moe_w3, moe_w2):
    b, s, d = x.shape
    depth = w_ada.shape[0]
    alpha = (2.0 * depth) ** 0.25
    mod = _ada(c, w_ada, b_ada)
    for l in range(depth):
        sh1, sc1, g1, sh2, sc2, g2 = [m[:, None, :] for m in jnp.split(mod[l], 6, axis=-1)]
        i = l // 2
        if l % 2 == 0:
            w_in = even_w_in[i]
            w_main = jnp.concatenate([w_in[:, :2 * SB_WIDTH], w_in[:, 3 * SB_WIDTH:]], axis=1)
            proj, v_t = _inproj(x, 1.0 + sc1, sh1, w_main, w_in[:, 2 * SB_WIDTH:3 * SB_WIDTH])
            parts = [_sb_attention(proj, v_t), _retention(proj, ret_gn_g[i], 2 * SB_WIDTH // LANES)]
            w_out = even_w_out[i]
            weights = [w_out[:SB_WIDTH], w_out[SB_WIDTH:]]
        else:
            w_in = odd_w_in[i]
            proj, v_t = _inproj(x, 1.0 + sc1, sh1, w_in[:, :2 * DIFF_QK], w_in[:, 2 * DIFF_QK:])
            lambda_init = 0.8 - 0.6 * math.exp(-0.3 * l)
            lam_rows = jnp.stack([lambda_q1[i], lambda_k1[i], lambda_q2[i], lambda_k2[i]])
            parts = [_diff_attention(proj, v_t, lam_rows, diff_subln_g[i], lambda_init)]
            weights = [odd_w_out[i]]
        w_router_t = jnp.zeros((ROUTER_ROWS, d), F32).at[:N_EXPERTS].set(moe_w_router[l].T.astype(F32))
        w_router_t = w_router_t.at[N_EXPERTS:N_EXPERTS + N_GROUPS].set(moe_w_group[l].T.astype(F32))
        b_router = jnp.zeros((ROUTER_ROWS, 1), F32).at[:N_EXPERTS, 0].set(moe_b_router[l].astype(F32))
        b_router = b_router.at[N_EXPERTS:N_EXPERTS + N_GROUPS, 0].set(moe_b_group[l].astype(F32))
        x, u, logits_t = _outproj(parts, weights, x, 1.0 + g1, ln1_g[l], ln1_b[l], 1.0 + sc2, sh2,
                                  w_router_t, b_router, alpha)
        x = _moe(u, logits_t, x, 1.0 + g2, ln2_g[l], ln2_b[l], moe_w1[l], moe_w3[l], moe_w2[l], alpha)
    return x
```

```python
import functools
import math

import jax
import jax.numpy as jnp
import numpy as np
from jax import lax
from jax.experimental import pallas as pl
from jax.experimental.pallas import tpu as pltpu

F32 = jnp.float32
I32 = jnp.int32
MXU_DTYPE = jnp.bfloat16
HIGHEST = lax.Precision.HIGHEST
LOG2E = math.log2(math.e)

LN_EPS = 1e-5
CHUNK = 64
LANES = 128
SB_HEADS, SB_HEAD_DIM = 8, 64
RET_HEADS, RET_HEAD_DIM = 4, 128
DIFF_HEADS, DIFF_HEAD_DIM = 8, 64
SB_WIDTH = SB_HEADS * SB_HEAD_DIM
RET_WIDTH = RET_HEADS * RET_HEAD_DIM
DIFF_QK = DIFF_HEADS * 2 * DIFF_HEAD_DIM
N_GROUPS, EXPERTS_PER_GROUP = 4, 8
N_EXPERTS = N_GROUPS * EXPERTS_PER_GROUP
TOP_K = 2
ROUTER_ROWS = 40

VMEM_LIMIT = 56 * 1024 * 1024


def _params(*sem):
    return pltpu.CompilerParams(dimension_semantics=sem, vmem_limit_bytes=VMEM_LIMIT)


def _dot(a, b):
    return jnp.dot(a, b, preferred_element_type=F32)


def _dot_nt(a, b):
    return lax.dot_general(a, b, (((1,), (1,)), ((), ())), preferred_element_type=F32)


def _dot_tn(a, b):
    return lax.dot_general(a, b, (((0,), (0,)), ((), ())), preferred_element_type=F32)


def _silu(x):
    return x * (1.0 / (1.0 + jnp.exp(-x)))


def _ada_kernel(c_ref, w_ref, b_ref, o_ref):
    o_ref[0] = jnp.dot(_silu(c_ref[...]), w_ref[0], preferred_element_type=F32, precision=HIGHEST) + b_ref[0]


def _ada(c, w_ada, b_ada):
    depth, d, n = w_ada.shape
    bp = 8
    cp = jnp.zeros((bp, d), F32).at[: c.shape[0]].set(c)
    tn = 1536
    out = pl.pallas_call(
        _ada_kernel,
        grid=(depth, n // tn),
        in_specs=[
            pl.BlockSpec((bp, d), lambda l, j: (0, 0)),
            pl.BlockSpec((1, d, tn), lambda l, j: (l, 0, j)),
            pl.BlockSpec((1, 1, tn), lambda l, j: (l, 0, j)),
        ],
        out_specs=pl.BlockSpec((1, bp, tn), lambda l, j: (l, 0, j)),
        out_shape=jax.ShapeDtypeStruct((depth, bp, n), F32),
        compiler_params=_params("parallel", "parallel"),
        name="ada_mod",
    )(cp, w_ada, b_ada.reshape(depth, 1, n))
    return out[:, : c.shape[0]]


def _inproj_kernel(x_ref, sc_ref, sh_ref, w_ref, wvt_ref, o_ref, vt_ref, *, tn):
    u = (x_ref[0] * sc_ref[0] + sh_ref[0]).astype(MXU_DTYPE)
    for j in range(o_ref.shape[2] // tn):
        o_ref[0, :, j * tn:(j + 1) * tn] = _dot(u, w_ref[:, j * tn:(j + 1) * tn]).astype(o_ref.dtype)
    for j in range(vt_ref.shape[1] // tn):
        vt_ref[0, j * tn:(j + 1) * tn, :] = _dot_nt(wvt_ref[j * tn:(j + 1) * tn, :], u).astype(vt_ref.dtype)


def _inproj(x, scale1p, shift, w, w_v):
    b, s, d = x.shape
    n, n_v = w.shape[1], w_v.shape[1]
    tm = min(512, s)
    return pl.pallas_call(
        functools.partial(_inproj_kernel, tn=512),
        grid=(b, s // tm),
        in_specs=[
            pl.BlockSpec((1, tm, d), lambda i, j: (i, j, 0)),
            pl.BlockSpec((1, 1, d), lambda i, j: (i, 0, 0)),
            pl.BlockSpec((1, 1, d), lambda i, j: (i, 0, 0)),
            pl.BlockSpec((d, n), lambda i, j: (0, 0)),
            pl.BlockSpec((n_v, d), lambda i, j: (0, 0)),
        ],
        out_specs=[pl.BlockSpec((1, tm, n), lambda i, j: (i, j, 0)),
                   pl.BlockSpec((1, n_v, tm), lambda i, j: (i, 0, j))],
        out_shape=[jax.ShapeDtypeStruct((b, s, n), MXU_DTYPE), jax.ShapeDtypeStruct((b, n_v, s), MXU_DTYPE)],
        compiler_params=_params("parallel", "parallel"),
        name="in_proj",
    )(x, scale1p, shift, w.astype(MXU_DTYPE), w_v.T.astype(MXU_DTYPE))


MASKED = -float("inf")
ITEM_PLAIN, ITEM_DIAGONAL, ITEM_NULL = 0, 1, 2
FLAG_FIRST, FLAG_LAST = 1, 2


def _triangle_items(n_q, pad, diagonal_first, n_streams):
    streams = []
    for s in range(n_streams):
        items = []
        for qb in range(s, n_q, n_streams):
            order = range(qb, -1, -1) if diagonal_first else range(qb + 1)
            for n, kb in enumerate(order):
                flags = (FLAG_FIRST if n == 0 else 0) | (FLAG_LAST if n == qb else 0)
                items.append((qb, kb, ITEM_DIAGONAL if kb == qb else ITEM_PLAIN, flags))
        streams.append(items)
    n_items = max(len(items) for items in streams)
    null = (0, 0, ITEM_NULL, 0)
    rows = [np.asarray([null] * pad + items + [null] * (n_items - len(items) + pad + 1), np.int32).T
            for items in streams]
    return np.concatenate(rows, axis=0).copy(), n_items


SB_STAGES = 3
SB_STREAMS = 2


def _sb_kernel(tab_ref, q_ref, k_ref, vt_ref, o_ref, qh_s, acc_ref, carry_ref, mask_ref, y_s, yms_s, sums_s,
               *, tq, n_items):
    tk = tq
    extra = 16
    heads = range(2)
    streams = range(SB_STREAMS)
    lane = lax.broadcasted_iota(I32, (1, LANES), 1)
    key = lax.broadcasted_iota(I32, (tk, tq), 0)
    qry = lax.broadcasted_iota(I32, (tk, tq), 1)
    r = lax.broadcasted_iota(I32, (tk + extra, tk), 0)
    c = lax.broadcasted_iota(I32, (tk + extra, tk), 1)
    neg_later = jnp.where(r >= tk, -1.0, jnp.where(c > r, -1.0, 0.0)).astype(MXU_DTYPE)
    scale2 = SB_HEAD_DIM ** -0.5 * LOG2E

    def prepare_queries(blk, _):
        rows = pl.ds(pl.multiple_of(blk * tq, tq), tq)
        q2 = q_ref[0, rows, :]
        for h in heads:
            qh_s[h, rows, :] = (jnp.where((lane // SB_HEAD_DIM) == h, q2, jnp.zeros_like(q2)).astype(F32)
                                * scale2).astype(MXU_DTYPE)
        return 0

    lax.fori_loop(0, q_ref.shape[1] // tq, prepare_queries, 0)
    acc_ref[...] = jnp.zeros_like(acc_ref)
    carry_ref[...] = jnp.zeros_like(carry_ref)
    mask_ref[ITEM_PLAIN] = jnp.zeros((tk, tq), F32)
    mask_ref[ITEM_DIAGONAL] = jnp.where(key < qry, 0.0, MASKED)
    mask_ref[ITEM_NULL] = jnp.full((tk, tq), MASKED, F32)
    y_s[...] = jnp.zeros_like(y_s)
    yms_s[...] = jnp.full(yms_s.shape, MASKED, F32)
    sums_s[...] = jnp.zeros_like(sums_s)

    def rows_of(block, size):
        return pl.ds(pl.multiple_of(block * size, size), size)

    def trip(it, parity):
        col_x, col_y, col_c = it + 2, it + 1, it
        for s in streams:
            kj = k_ref[0, rows_of(tab_ref[4 * s + 1, col_x], tk), :]
            qrows = rows_of(tab_ref[4 * s, col_x], tq)
            for h in heads:
                y_s[2 * s + h, parity] = _dot_nt(kj, qh_s[h, qrows, :])
        for s in streams:
            mask = mask_ref[tab_ref[4 * s + 2, col_y]]
            for h in heads:
                ym = y_s[2 * s + h, 1 - parity] + mask
                sp = jnp.maximum(ym, jnp.log2(1.0 + jnp.exp2(jnp.minimum(ym, 126.0))))
                yms_s[2 * s + h, parity] = ym - sp
                sums_s[2 * s + h, parity] = _dot(neg_later, sp.astype(MXU_DTYPE))
        for s in streams:
            vtj = vt_ref[0, :, rows_of(tab_ref[4 * s + 1, col_c], tk)]
            keep = jnp.where((tab_ref[4 * s + 3, col_c] & FLAG_FIRST) != 0, 0.0, 1.0)
            for h in heads:
                sums = sums_s[2 * s + h, 1 - parity]
                carry = carry_ref[2 * s + h] * keep
                w = jnp.exp2(yms_s[2 * s + h, 1 - parity] + sums[:tk] + carry).astype(MXU_DTYPE)
                acc_ref[2 * s + h] = acc_ref[2 * s + h] * keep + _dot(vtj, w)
                carry_ref[2 * s + h] = carry + sums[tk:tk + 1]

        for s in streams:
            @pl.when((tab_ref[4 * s + 3, col_c] & FLAG_LAST) != 0)
            def _(s=s):
                sub = lax.broadcasted_iota(I32, (LANES, 1), 0)
                o_ref[0, rows_of(tab_ref[4 * s, col_c], tq), :] = jnp.where(
                    sub < SB_HEAD_DIM, acc_ref[2 * s], acc_ref[2 * s + 1]).T.astype(o_ref.dtype)

    def trip_pair(i, _):
        trip(2 * i, 0)
        trip(2 * i + 1, 1)
        return 0

    lax.fori_loop(0, pl.cdiv(n_items + SB_STAGES - 1, 2), trip_pair, 0)


def _sb_attention(proj, v_t):
    b, s, _ = proj.shape
    tq = min(256, s)
    n_pairs = SB_WIDTH // LANES
    table, n_items = _triangle_items(s // tq, SB_STAGES - 1, diagonal_first=True, n_streams=SB_STREAMS)
    n_chains = 2 * SB_STREAMS
    grid_spec = pltpu.PrefetchScalarGridSpec(
        num_scalar_prefetch=1,
        grid=(b, n_pairs),
        in_specs=[
            pl.BlockSpec((1, s, LANES), lambda i, p, t: (i, 0, p)),
            pl.BlockSpec((1, s, LANES), lambda i, p, t: (i, 0, n_pairs + p)),
            pl.BlockSpec((1, LANES, s), lambda i, p, t: (i, p, 0)),
        ],
        out_specs=pl.BlockSpec((1, s, LANES), lambda i, p, t: (i, 0, p)),
        scratch_shapes=[
            pltpu.VMEM((2, s, LANES), MXU_DTYPE),
            pltpu.VMEM((n_chains, LANES, tq), F32),
            pltpu.VMEM((n_chains, 1, tq), F32),
            pltpu.VMEM((3, tq, tq), F32),
            pltpu.VMEM((n_chains, 2, tq, tq), F32),
            pltpu.VMEM((n_chains, 2, tq, tq), F32),
            pltpu.VMEM((n_chains, 2, tq + 16, tq), F32),
        ],
    )
    return pl.pallas_call(
        functools.partial(_sb_kernel, tq=tq, n_items=n_items),
        grid_spec=grid_spec,
        out_shape=jax.ShapeDtypeStruct((b, s, SB_WIDTH), MXU_DTYPE),
        compiler_params=_params("parallel", "parallel"),
        name="sb_attention",
    )(jnp.asarray(table), proj, proj, v_t)


def _ret_kernel(lg_ref, q_ref, k_ref, v_ref, g_ref, gn_ref, o_ref, state_ref, *, tb):
    h = pl.program_id(1)
    blk = pl.program_id(2)

    @pl.when(blk == 0)
    def _():
        state_ref[...] = jnp.zeros_like(state_ref)

    lg = lg_ref[h]
    scale = RET_HEAD_DIM ** -0.5
    q = q_ref[0].astype(F32)
    k = k_ref[0].astype(F32)
    v = v_ref[0]
    row = lax.broadcasted_iota(I32, (tb, tb), 0)
    col = lax.broadcasted_iota(I32, (tb, tb), 1)
    dist = jnp.abs(row - col).astype(F32)
    decay = jnp.where(col // CHUNK <= row // CHUNK, jnp.exp(lg * dist) * scale, 0.0)
    pos = lax.broadcasted_iota(I32, (tb, 1), 0).astype(F32)
    scores = _dot_nt(q.astype(MXU_DTYPE), k.astype(MXU_DTYPE)) * decay
    intra = _dot(scores.astype(MXU_DTYPE), v)
    state = state_ref[...]
    q_in = (q * jnp.exp(lg * (pos + 1.0))).astype(MXU_DTYPE)
    inter = _dot(q_in, state.astype(MXU_DTYPE))
    k_out = (k * (jnp.exp(lg * (tb - 1.0 - pos)) * scale)).astype(MXU_DTYPE)
    block_decay = jnp.exp(lg * jnp.full((1, RET_HEAD_DIM), float(tb), F32))
    state_ref[...] = block_decay * state + _dot_tn(k_out, v)
    o = intra + inter
    mu = jnp.mean(o, axis=-1, keepdims=True)
    oc = o - mu
    var = jnp.mean(oc * oc, axis=-1, keepdims=True)
    o = oc * lax.rsqrt(var + LN_EPS) * gn_ref[0] * _silu(g_ref[0].astype(F32))
    o_ref[0] = o.astype(o_ref.dtype)


def _retention(proj, gn_gain, first_col_block):
    b, s, _ = proj.shape
    tb = min(512, s)
    log_gamma = jnp.log1p(-jnp.exp2(-5.0 - jnp.arange(RET_HEADS, dtype=F32)))

    def col(which):
        return lambda i, h, j, lg: (i, j, first_col_block + which * RET_HEADS + h)

    grid_spec = pltpu.PrefetchScalarGridSpec(
        num_scalar_prefetch=1,
        grid=(b, RET_HEADS, s // tb),
        in_specs=[pl.BlockSpec((1, tb, LANES), col(w)) for w in range(4)]
        + [pl.BlockSpec((1, 1, LANES), lambda i, h, j, lg: (h, 0, 0))],
        out_specs=pl.BlockSpec((1, tb, LANES), lambda i, h, j, lg: (i, j, h)),
        scratch_shapes=[pltpu.VMEM((RET_HEAD_DIM, RET_HEAD_DIM), F32)],
    )
    return pl.pallas_call(
        functools.partial(_ret_kernel, tb=tb),
        grid_spec=grid_spec,
        out_shape=jax.ShapeDtypeStruct((b, s, RET_WIDTH), MXU_DTYPE),
        compiler_params=_params("parallel", "parallel", "arbitrary"),
        name="retention",
    )(log_gamma, proj, proj, proj, proj, gn_gain.astype(F32).reshape(RET_HEADS, 1, RET_HEAD_DIM))


DIFF_STAGES = 3
DIFF_HEADS_PER_STEP = 2
DIFF_STREAMS = 1


def _diff_kernel(slope_ref, tab_ref, q_ref, k_ref, vt_ref, lam_ref, gain_ref, o_ref, qs_s, m_ref, l_ref, acc_ref,
                 bias_ref, z_s, p_s, a_s, lfin_s, *, tq, n_items, lambda_init):
    hp = pl.program_id(1)
    tk = tq
    nh = DIFF_HEADS_PER_STEP
    dv = LANES
    heads = range(nh)
    streams = range(DIFF_STREAMS)
    lane = lax.broadcasted_iota(I32, (1, LANES), 1)
    scale2 = DIFF_HEAD_DIM ** -0.5 * LOG2E
    slope2 = [slope_ref[hp * nh + hh] * LOG2E for hh in heads]

    def rows_of(block, size):
        return pl.ds(pl.multiple_of(block * size, size), size)

    def prepare_queries(blk, _):
        for hh in heads:
            q2 = q_ref[0, rows_of(blk, tq), hh * LANES:(hh + 1) * LANES]
            zero = jnp.zeros_like(q2)
            stacked = jnp.concatenate([jnp.where(lane < DIFF_HEAD_DIM, q2, zero),
                                       jnp.where(lane >= DIFF_HEAD_DIM, q2, zero)], axis=0)
            qs_s[hh, rows_of(blk, 2 * tq), :] = (stacked.astype(F32) * scale2).astype(MXU_DTYPE)
        return 0

    lax.fori_loop(0, q_ref.shape[1] // tq, prepare_queries, 0)
    key = lax.broadcasted_iota(I32, (tk, tq), 0)
    qry = lax.broadcasted_iota(I32, (tk, tq), 1)
    visible = key // CHUNK <= qry // CHUNK
    for hh in heads:
        plain = slope2[hh] * key.astype(F32)
        diag = jnp.where(visible, slope2[hh] * (qry - jnp.abs(qry - key)).astype(F32), MASKED)
        bias_ref[hh, ITEM_PLAIN] = jnp.concatenate([plain, plain], axis=1)
        bias_ref[hh, ITEM_DIAGONAL] = jnp.concatenate([diag, diag], axis=1)
    m_ref[...] = jnp.zeros_like(m_ref)
    l_ref[...] = jnp.zeros_like(l_ref)
    acc_ref[...] = jnp.zeros_like(acc_ref)
    z_s[...] = jnp.zeros_like(z_s)
    p_s[...] = jnp.zeros_like(p_s)
    a_s[...] = jnp.ones_like(a_s)
    lfin_s[...] = jnp.ones_like(lfin_s)

    def trip(it, parity):
        col_a, col_a1, col_c = it + 2, it + 1, it
        for s in streams:
            krows = rows_of(tab_ref[4 * s + 1, col_a], tk)
            qrows = rows_of(tab_ref[4 * s, col_a], 2 * tq)
            for hh in heads:
                z_s[nh * s + hh, parity] = _dot_nt(k_ref[0, krows, hh * LANES:(hh + 1) * LANES], qs_s[hh, qrows, :])
        lfin = {}
        for s in streams:
            vrows = rows_of(tab_ref[4 * s + 1, col_c], tk)
            for hh in heads:
                c = nh * s + hh
                acc_ref[c] = a_s[c] * acc_ref[c] + _dot(vt_ref[0, hh * dv:(hh + 1) * dv, vrows], p_s[c])
                lfin[c] = lfin_s[c]
        for s in streams:
            kind = tab_ref[4 * s + 2, col_a1]
            first = (tab_ref[4 * s + 3, col_a1] & FLAG_FIRST) != 0
            offset = ((tab_ref[4 * s + 1, col_a1] - tab_ref[4 * s, col_a1]) * tq).astype(F32)
            for hh in heads:
                c = nh * s + hh
                shift = jnp.where(kind == ITEM_PLAIN, slope2[hh] * offset,
                                  jnp.where(kind == ITEM_DIAGONAL, 0.0, MASKED))
                z = z_s[c, 1 - parity] + bias_ref[hh, jnp.minimum(kind, ITEM_DIAGONAL)]
                m_old = jnp.where(first, MASKED, m_ref[c])
                m_new = jnp.maximum(m_old, jnp.max(z, axis=0, keepdims=True) + shift)
                p = jnp.exp2(z - (m_new - shift))
                a = jnp.exp2(m_old - m_new)
                l_new = a * l_ref[c] + jnp.sum(p, axis=0, keepdims=True)
                p_s[c] = p.astype(MXU_DTYPE)
                a_s[c] = a
                lfin_s[c] = l_new
                l_ref[c] = l_new
                m_ref[c] = m_new

        for s in streams:
            @pl.when((tab_ref[4 * s + 3, col_c] & FLAG_LAST) != 0)
            def _(s=s):
                lam_v = lam_ref[...]
                lam = (jnp.exp(jnp.sum(lam_v[0:1] * lam_v[1:2], axis=-1, keepdims=True))
                       - jnp.exp(jnp.sum(lam_v[2:3] * lam_v[3:4], axis=-1, keepdims=True)) + lambda_init)
                orows = rows_of(tab_ref[4 * s, col_c], tq)
                for hh in heads:
                    c = nh * s + hh
                    o = acc_ref[c] * (1.0 / lfin[c])
                    o = o[:, :tq] - lam * o[:, tq:]
                    o = o * lax.rsqrt(jnp.mean(o * o, axis=0, keepdims=True) + LN_EPS)
                    o_ref[0, orows, hh * dv:(hh + 1) * dv] = (o * gain_ref[...] * (1.0 - lambda_init)).T.astype(
                        o_ref.dtype)

    def trip_pair(i, _):
        trip(2 * i, 0)
        trip(2 * i + 1, 1)
        return 0

    lax.fori_loop(0, pl.cdiv(n_items + DIFF_STAGES - 1, 2), trip_pair, 0)


def _diff_attention(proj, v_t, lam_rows, subln_gain, lambda_init):
    b, s, _ = proj.shape
    tq = min(256, s)
    dv = subln_gain.shape[-1]
    nh = DIFF_HEADS_PER_STEP
    slopes = jnp.exp2(-8.0 / DIFF_HEADS * (jnp.arange(DIFF_HEADS, dtype=F32) + 1.0))
    kb = DIFF_QK // (nh * LANES)
    table, n_items = _triangle_items(s // tq, DIFF_STAGES - 1, diagonal_first=False, n_streams=DIFF_STREAMS)
    n_chains = nh * DIFF_STREAMS
    grid_spec = pltpu.PrefetchScalarGridSpec(
        num_scalar_prefetch=2,
        grid=(b, DIFF_HEADS // nh),
        in_specs=[
            pl.BlockSpec((1, s, nh * LANES), lambda i, h, sl, t: (i, 0, h)),
            pl.BlockSpec((1, s, nh * LANES), lambda i, h, sl, t: (i, 0, kb + h)),
            pl.BlockSpec((1, nh * dv, s), lambda i, h, sl, t: (i, h, 0)),
            pl.BlockSpec((4, DIFF_HEAD_DIM), lambda i, h, sl, t: (0, 0)),
            pl.BlockSpec((dv, 1), lambda i, h, sl, t: (0, 0)),
        ],
        out_specs=pl.BlockSpec((1, s, nh * dv), lambda i, h, sl, t: (i, 0, h)),
        scratch_shapes=[
            pltpu.VMEM((nh, 2 * s, LANES), MXU_DTYPE),
            pltpu.VMEM((n_chains, 1, 2 * tq), F32),
            pltpu.VMEM((n_chains, 1, 2 * tq), F32),
            pltpu.VMEM((n_chains, dv, 2 * tq), F32),
            pltpu.VMEM((nh, 2, tq, 2 * tq), F32),
            pltpu.VMEM((n_chains, 2, tq, 2 * tq), F32),
            pltpu.VMEM((n_chains, tq, 2 * tq), MXU_DTYPE),
            pltpu.VMEM((n_chains, 1, 2 * tq), F32),
            pltpu.VMEM((n_chains, 1, 2 * tq), F32),
        ],
    )
    return pl.pallas_call(
        functools.partial(_diff_kernel, tq=tq, n_items=n_items, lambda_init=lambda_init),
        grid_spec=grid_spec,
        out_shape=jax.ShapeDtypeStruct((b, s, DIFF_HEADS * dv), MXU_DTYPE),
        compiler_params=_params("parallel", "parallel"),
        name="diff_attention",
    )(slopes, jnp.asarray(table), proj, proj, v_t, lam_rows.astype(F32), subln_gain.astype(F32).reshape(dv, 1))


def _layer_norm(y, g, b):
    mu = jnp.mean(y, axis=-1, keepdims=True)
    yc = y - mu
    var = jnp.mean(yc * yc, axis=-1, keepdims=True)
    return yc * lax.rsqrt(var + LN_EPS) * g + b


def _outproj_kernel(*refs, n_in, alpha):
    a_refs, w_refs = refs[:n_in], refs[n_in:2 * n_in]
    x_ref, gate_ref, lng_ref, lnb_ref, sc_ref, sh_ref, wr_ref, br_ref, xo_ref, u_ref, lg_ref = refs[2 * n_in:]
    mix = _dot(a_refs[0][0], w_refs[0][...])
    for a_ref, w_ref in zip(a_refs[1:], w_refs[1:]):
        mix += _dot(a_ref[0], w_ref[...])
    xn = _layer_norm(alpha * x_ref[0] + gate_ref[0] * mix, lng_ref[...], lnb_ref[...])
    xo_ref[0] = xn
    u = xn * sc_ref[0] + sh_ref[0]
    u_ref[0] = u
    lg_ref[...] = lax.dot_general(wr_ref[...], u, (((1,), (1,)), ((), ())), preferred_element_type=F32,
                                  precision=HIGHEST) + br_ref[...]


def _outproj(parts, weights, x, gate1p, ln_g, ln_b, scale1p, shift, w_router_t, b_router, alpha):
    b, s, d = x.shape
    tm = min(512, s)
    n_in = len(parts)
    vec = pl.BlockSpec((1, 1, d), lambda i, j: (i, 0, 0))
    row = pl.BlockSpec((1, d), lambda i, j: (0, 0))
    in_specs = [pl.BlockSpec((1, tm, p.shape[-1]), lambda i, j: (i, j, 0)) for p in parts]
    in_specs += [pl.BlockSpec(w.shape, lambda i, j: (0, 0)) for w in weights]
    in_specs += [pl.BlockSpec((1, tm, d), lambda i, j: (i, j, 0)), vec, row, row, vec, vec,
                 pl.BlockSpec((ROUTER_ROWS, d), lambda i, j: (0, 0)),
                 pl.BlockSpec((ROUTER_ROWS, 1), lambda i, j: (0, 0))]
    nb = s // tm
    return pl.pallas_call(
        functools.partial(_outproj_kernel, n_in=n_in, alpha=alpha),
        grid=(b, nb),
        in_specs=in_specs,
        out_specs=[pl.BlockSpec((1, tm, d), lambda i, j: (i, j, 0)),
                   pl.BlockSpec((1, tm, d), lambda i, j: (i, j, 0)),
                   pl.BlockSpec((ROUTER_ROWS, tm), lambda i, j: (0, i * nb + j))],
        out_shape=[jax.ShapeDtypeStruct((b, s, d), F32), jax.ShapeDtypeStruct((b, s, d), F32),
                   jax.ShapeDtypeStruct((ROUTER_ROWS, b * s), F32)],
        compiler_params=_params("parallel", "parallel"),
        name="out_proj_norm",
    )(*parts, *[w.astype(MXU_DTYPE) for w in weights], x, gate1p, ln_g.reshape(1, d), ln_b.reshape(1, d),
      scale1p, shift, w_router_t, b_router)


def _route_kernel(lg_ref, ids_ref, gates_ref, cnt_ref, run_ref, *, tm):
    @pl.when(pl.program_id(0) == 0)
    def _():
        run_ref[...] = jnp.zeros_like(run_ref)

    lg = lg_ref[...]
    g0 = N_EXPERTS
    g_max = lg[g0:g0 + 1]
    grp = jnp.zeros((1, tm), I32)
    for i in range(1, N_GROUPS):
        gi = lg[g0 + i:g0 + i + 1]
        better = gi > g_max
        grp = jnp.where(better, i, grp)
        g_max = jnp.where(better, gi, g_max)
    den = jnp.exp(lg[g0:g0 + 1] - g_max)
    for i in range(1, N_GROUPS):
        den += jnp.exp(lg[g0 + i:g0 + i + 1] - g_max)
    p_grp = 1.0 / den

    cand = lg[0:EXPERTS_PER_GROUP]
    for g in range(1, N_GROUPS):
        cand = jnp.where(grp == g, lg[g * EXPERTS_PER_GROUP:(g + 1) * EXPERTS_PER_GROUP], cand)
    ridx = lax.broadcasted_iota(I32, (EXPERTS_PER_GROUP, tm), 0).astype(F32)
    none = float(EXPERTS_PER_GROUP)
    v1 = jnp.max(cand, axis=0, keepdims=True)
    i1 = jnp.min(jnp.where(cand == v1, ridx, none), axis=0, keepdims=True)
    rest = jnp.where(ridx == i1, -jnp.inf, cand)
    v2 = jnp.max(rest, axis=0, keepdims=True)
    i2 = jnp.min(jnp.where(rest == v2, ridx, none), axis=0, keepdims=True)
    e21 = jnp.exp(v2 - v1)
    gate1 = p_grp / (1.0 + e21)
    gate2 = p_grp * e21 / (1.0 + e21)
    ex1 = grp * EXPERTS_PER_GROUP + i1.astype(I32)
    ex2 = grp * EXPERTS_PER_GROUP + i2.astype(I32)

    eidx = lax.broadcasted_iota(I32, (N_EXPERTS, tm), 0)
    oh1 = jnp.where(eidx == ex1, 1.0, 0.0)
    oh2 = jnp.where(eidx == ex2, 1.0, 0.0)
    oh = (oh1 + oh2).astype(MXU_DTYPE)
    earlier = jnp.where(lax.broadcasted_iota(I32, (tm, tm), 0) < lax.broadcasted_iota(I32, (tm, tm), 1),
                        1.0, 0.0).astype(MXU_DTYPE)
    run = run_ref[...]
    before = _dot(oh, earlier) + jnp.concatenate([run] * (tm // LANES), axis=1)
    rank1 = jnp.sum(oh1 * before, axis=0, keepdims=True).astype(I32)
    rank2 = jnp.sum(oh2 * before, axis=0, keepdims=True).astype(I32)
    run = run + _dot(oh, jnp.ones((tm, LANES), MXU_DTYPE))
    run_ref[...] = run
    cnt_ref[...] = run
    ids_ref[...] = jnp.concatenate([ex1, ex2, rank1, rank2, jnp.zeros((4, tm), I32)], axis=0)
    gates_ref[...] = jnp.concatenate([gate1, gate2, jnp.zeros((6, tm), F32)], axis=0)


def _route(logits_t):
    t = logits_t.shape[1]
    tm = min(512, t)
    return pl.pallas_call(
        functools.partial(_route_kernel, tm=tm),
        grid=(t // tm,),
        in_specs=[pl.BlockSpec((ROUTER_ROWS, tm), lambda i: (0, i))],
        out_specs=[pl.BlockSpec((8, tm), lambda i: (0, i)), pl.BlockSpec((8, tm), lambda i: (0, i)),
                   pl.BlockSpec((N_EXPERTS, LANES), lambda i: (0, 0))],
        out_shape=[jax.ShapeDtypeStruct((8, t), I32), jax.ShapeDtypeStruct((8, t), F32),
                   jax.ShapeDtypeStruct((N_EXPERTS, LANES), F32)],
        scratch_shapes=[pltpu.VMEM((N_EXPERTS, LANES), F32)],
        compiler_params=_params("arbitrary"),
        name="route",
    )(logits_t)


def _row_copy(idx_ref, src_hbm, dst, sem, r):
    return pltpu.make_async_copy(src_hbm.at[pl.ds(idx_ref[0, 0, r], 1), :], dst.at[pl.ds(r, 1), :], sem)


def _gather_rows(idx_ref, src_hbm, dst, sem, n):
    def body(r, _):
        _row_copy(idx_ref, src_hbm, dst, sem, r).start()
        return 0

    lax.fori_loop(0, n, body, 0)


def _gather_rows_inline(idx_ref, src_hbm, dst, sem, n):
    for r in range(n):
        _row_copy(idx_ref, src_hbm, dst, sem, r).start()


def _wait_rows(src_hbm, dst, sem, n):
    pltpu.make_async_copy(src_hbm.at[pl.ds(0, n), :], dst, sem).wait()


def _expert_kernel(be_ref, idx_a_ref, idx_b_ref, idx_next_ref, x_hbm, w1a_ref, w3a_ref, w2a_ref, w1b_ref, w3b_ref,
                   w2b_ref, y_ref, xbuf_a, xbuf_b, sem, *, tm):
    i = pl.program_id(0)

    @pl.when(i == 0)
    def _():
        _gather_rows(idx_a_ref, x_hbm, xbuf_a, sem.at[0], tm)

    def block(cur, cur_sem, w1_ref, w3_ref, w2_ref, out_rows, nxt_idx_ref, nxt, nxt_sem):
        _gather_rows_inline(nxt_idx_ref, x_hbm, nxt, nxt_sem, tm)
        _wait_rows(x_hbm, cur, cur_sem, tm)
        xb = cur[...].astype(MXU_DTYPE)
        hidden = _silu(_dot(xb, w1_ref[0].astype(MXU_DTYPE))) * _dot(xb, w3_ref[0].astype(MXU_DTYPE))
        y_ref[out_rows, :] = _dot(hidden.astype(MXU_DTYPE), w2_ref[0].astype(MXU_DTYPE))

    block(xbuf_a, sem.at[0], w1a_ref, w3a_ref, w2a_ref, slice(0, tm), idx_b_ref, xbuf_b, sem.at[1])
    block(xbuf_b, sem.at[1], w1b_ref, w3b_ref, w2b_ref, slice(tm, 2 * tm), idx_next_ref, xbuf_a, sem.at[0])

    @pl.when(i == pl.num_programs(0) - 1)
    def _():
        _wait_rows(x_hbm, xbuf_a, sem.at[0], tm)


def _experts(u_flat, slot_tok, blk_expert, w1, w3, w2, tm):
    t, d = u_flat.shape
    n_blk = slot_tok.shape[0] // tm
    assert n_blk % 2 == 0
    ff = w1.shape[-1]
    idx3 = slot_tok.reshape(n_blk, 1, tm)
    last = n_blk - 1

    def idx_spec(block_of):
        return pl.BlockSpec((1, 1, tm), lambda i, be: (block_of(i), 0, 0), memory_space=pltpu.SMEM)

    def weight_specs(which):
        return [pl.BlockSpec((1, d, ff), lambda i, be: (be[2 * i + which], 0, 0)),
                pl.BlockSpec((1, d, ff), lambda i, be: (be[2 * i + which], 0, 0)),
                pl.BlockSpec((1, ff, d), lambda i, be: (be[2 * i + which], 0, 0))]

    grid_spec = pltpu.PrefetchScalarGridSpec(
        num_scalar_prefetch=1,
        grid=(n_blk // 2,),
        in_specs=[idx_spec(lambda i: 2 * i), idx_spec(lambda i: 2 * i + 1),
                  idx_spec(lambda i: jnp.minimum(2 * i + 2, last)), pl.BlockSpec(memory_space=pl.ANY)]
        + weight_specs(0) + weight_specs(1),
        out_specs=pl.BlockSpec((2 * tm, d), lambda i, be: (i, 0)),
        scratch_shapes=[pltpu.VMEM((tm, d), F32), pltpu.VMEM((tm, d), F32), pltpu.SemaphoreType.DMA((2,))],
    )
    return pl.pallas_call(
        functools.partial(_expert_kernel, tm=tm),
        grid_spec=grid_spec,
        out_shape=jax.ShapeDtypeStruct((n_blk * tm, d), F32),
        compiler_params=_params("arbitrary"),
        name="experts",
    )(blk_expert, idx3, idx3, idx3, u_flat, w1, w3, w2, w1, w3, w2)


def _combine_kernel(d1_ref, d2_ref, d1n_ref, d2n_ref, y_hbm, gates_ref, x_ref, gate_ref, lng_ref, lnb_ref,
                    o_ref, ybuf, sem, *, tm, alpha):
    i = pl.program_id(0)
    n = pl.num_programs(0)
    slot = i % 2

    def wait(s):
        _wait_rows(y_hbm, ybuf.at[s, 0], sem.at[s, 0], tm)
        _wait_rows(y_hbm, ybuf.at[s, 1], sem.at[s, 1], tm)

    @pl.when(i == 0)
    def _():
        _gather_rows(d1_ref, y_hbm, ybuf.at[0, 0], sem.at[0, 0], tm)
        _gather_rows(d2_ref, y_hbm, ybuf.at[0, 1], sem.at[0, 1], tm)

    _gather_rows_inline(d1n_ref, y_hbm, ybuf.at[1 - slot, 0], sem.at[1 - slot, 0], tm)
    _gather_rows_inline(d2n_ref, y_hbm, ybuf.at[1 - slot, 1], sem.at[1 - slot, 1], tm)
    wait(slot)
    gates = gates_ref[...]
    f = gates[:, 0:1] * ybuf[slot, 0] + gates[:, 1:2] * ybuf[slot, 1]
    o_ref[...] = _layer_norm(alpha * x_ref[...] + gate_ref[0] * f, lng_ref[...], lnb_ref[...])

    @pl.when(i == n - 1)
    def _():
        wait(1 - slot)


def _combine(y_slots, dest1, dest2, gates, x_flat, gate1p, ln_g, ln_b, seq, alpha):
    t, d = x_flat.shape
    tm = min(256, seq)
    n_blk = t // tm
    per_seq = seq // tm
    d1 = dest1.reshape(n_blk, 1, tm)
    d2 = dest2.reshape(n_blk, 1, tm)
    cur = pl.BlockSpec((1, 1, tm), lambda i: (i, 0, 0), memory_space=pltpu.SMEM)
    nxt = pl.BlockSpec((1, 1, tm), lambda i: (jnp.minimum(i + 1, n_blk - 1), 0, 0), memory_space=pltpu.SMEM)
    row = pl.BlockSpec((1, d), lambda i: (0, 0))
    return pl.pallas_call(
        functools.partial(_combine_kernel, tm=tm, alpha=alpha),
        grid=(n_blk,),
        in_specs=[cur, cur, nxt, nxt, pl.BlockSpec(memory_space=pl.ANY),
                  pl.BlockSpec((tm, TOP_K), lambda i: (i, 0)),
                  pl.BlockSpec((tm, d), lambda i: (i, 0)),
                  pl.BlockSpec((1, 1, d), lambda i: (i // per_seq, 0, 0)), row, row],
        out_specs=pl.BlockSpec((tm, d), lambda i: (i, 0)),
        out_shape=jax.ShapeDtypeStruct((t, d), F32),
        scratch_shapes=[pltpu.VMEM((2, 2, tm, d), F32), pltpu.SemaphoreType.DMA((2, 2))],
        compiler_params=_params("arbitrary"),
        name="combine_norm",
    )(d1, d2, d1, d2, y_slots, gates, x_flat, gate1p, ln_g.reshape(1, d), ln_b.reshape(1, d))


EXPERT_TILE = 256
SLOT_STRIDE = 7919

def _moe(u, logits_t, x, gate2p, ln_g, ln_b, w1, w3, w2, alpha):
    b, s, d = x.shape
    t = b * s
    ids, gates8, cnt = _route(logits_t)
    counts = cnt[:, 0].astype(I32)
    padded = (counts + EXPERT_TILE - 1) // EXPERT_TILE * EXPERT_TILE
    pad_end = jnp.cumsum(padded)
    pad_start = pad_end - padded
    def slot(e, rank):
        return pad_start[e] + (rank * SLOT_STRIDE) % padded[e]

    dest1 = slot(ids[0], ids[2])
    dest2 = slot(ids[1], ids[3])
    n_slots = t * TOP_K + N_EXPERTS * EXPERT_TILE
    n_blk = n_slots // EXPERT_TILE
    tok = jnp.arange(t, dtype=I32)
    slot_tok = jnp.zeros((n_slots,), I32).at[jnp.concatenate([dest1, dest2])].set(jnp.concatenate([tok, tok]))
    blk_start = jnp.arange(n_blk, dtype=I32) * EXPERT_TILE
    blk_expert = jnp.minimum(jnp.sum((pad_end[None, :] <= blk_start[:, None]).astype(I32), axis=1), N_EXPERTS - 1)
    y_slots = _experts(u.reshape(t, d), slot_tok, blk_expert.astype(I32), w1, w3, w2, EXPERT_TILE)
    out = _combine(y_slots, dest1, dest2, gates8[:TOP_K].T, x.reshape(t, d), gate2p, ln_g, ln_b, s, alpha)
    return out.reshape(b, s, d)


def kernel(x, c, ln1_g, ln1_b, ln2_g, ln2_b, w_ada, b_ada, even_w_in, even_w_out, ret_gn_g, odd_w_in, odd_w_out, lambda_q1, lambda_k1, lambda_q2, lambda_k2, diff_subln_g, moe_w_group, moe_b_group, moe_w_router, moe_b_router, moe_w1, moe_w3, moe_w2):
    b, s, d = x.shape
    depth = w_ada.shape[0]
    alpha = (2.0 * depth) ** 0.25
    mod = _ada(c, w_ada, b_ada)
    for l in range(depth):
        sh1, sc1, g1, sh2, sc2, g2 = [m[:, None, :] for m in jnp.split(mod[l], 6, axis=-1)]
        i = l // 2
        if l % 2 == 0:
            w_in = even_w_in[i]
            w_main = jnp.concatenate([w_in[:, :2 * SB_WIDTH], w_in[:, 3 * SB_WIDTH:]], axis=1)
            proj, v_t = _inproj(x, 1.0 + sc1, sh1, w_main, w_in[:, 2 * SB_WIDTH:3 * SB_WIDTH])
            parts = [_sb_attention(proj, v_t), _retention(proj, ret_gn_g[i], 2 * SB_WIDTH // LANES)]
            w_out = even_w_out[i]
            weights = [w_out[:SB_WIDTH], w_out[SB_WIDTH:]]
        else:
            w_in = odd_w_in[i]
            proj, v_t = _inproj(x, 1.0 + sc1, sh1, w_in[:, :2 * DIFF_QK], w_in[:, 2 * DIFF_QK:])
            lambda_init = 0.8 - 0.6 * math.exp(-0.3 * l)
            lam_rows = jnp.stack([lambda_q1[i], lambda_k1[i], lambda_q2[i], lambda_k2[i]])
            parts = [_diff_attention(proj, v_t, lam_rows, diff_subln_g[i], lambda_init)]
            weights = [odd_w_out[i]]
        w_router_t = jnp.zeros((ROUTER_ROWS, d), F32).at[:N_EXPERTS].set(moe_w_router[l].T.astype(F32))
        w_router_t = w_router_t.at[N_EXPERTS:N_EXPERTS + N_GROUPS].set(moe_w_group[l].T.astype(F32))
        b_router = jnp.zeros((ROUTER_ROWS, 1), F32).at[:N_EXPERTS, 0].set(moe_b_router[l].astype(F32))
        b_router = b_router.at[N_EXPERTS:N_EXPERTS + N_GROUPS, 0].set(moe_b_group[l].astype(F32))
        x, u, logits_t = _outproj(parts, weights, x, 1.0 + g1, ln1_g[l], ln1_b[l], 1.0 + sc2, sh2,
                                  w_router_t, b_router, alpha)
        x = _moe(u, logits_t, x, 1.0 + g2, ln2_g[l], ln2_b[l], moe_w1[l], moe_w3[l], moe_w2[l], alpha)
    return x
```

```python
import functools
import math

import jax
import jax.numpy as jnp
import numpy as np
from jax import lax
from jax.experimental import pallas as pl
from jax.experimental.pallas import tpu as pltpu

F32 = jnp.float32
I32 = jnp.int32
MXU_DTYPE = jnp.bfloat16
HIGHEST = lax.Precision.HIGHEST
LOG2E = math.log2(math.e)

LN_EPS = 1e-5
CHUNK = 64
LANES = 128
SB_HEADS, SB_HEAD_DIM = 8, 64
RET_HEADS, RET_HEAD_DIM = 4, 128
DIFF_HEADS, DIFF_HEAD_DIM = 8, 64
SB_WIDTH = SB_HEADS * SB_HEAD_DIM
RET_WIDTH = RET_HEADS * RET_HEAD_DIM
DIFF_QK = DIFF_HEADS * 2 * DIFF_HEAD_DIM
N_GROUPS, EXPERTS_PER_GROUP = 4, 8
N_EXPERTS = N_GROUPS * EXPERTS_PER_GROUP
TOP_K = 2
ROUTER_ROWS = 40

VMEM_LIMIT = 56 * 1024 * 1024


def _params(*sem):
    return pltpu.CompilerParams(dimension_semantics=sem, vmem_limit_bytes=VMEM_LIMIT)


def _dot(a, b):
    return jnp.dot(a, b, preferred_element_type=F32)


def _dot_nt(a, b):
    return lax.dot_general(a, b, (((1,), (1,)), ((), ())), preferred_element_type=F32)


def _dot_tn(a, b):
    return lax.dot_general(a, b, (((0,), (0,)), ((), ())), preferred_element_type=F32)


def _silu(x):
    return x * (1.0 / (1.0 + jnp.exp(-x)))


def _ada_kernel(c_ref, w_ref, b_ref, o_ref):
    o_ref[0] = jnp.dot(_silu(c_ref[...]), w_ref[0], preferred_element_type=F32, precision=HIGHEST) + b_ref[0]


def _ada(c, w_ada, b_ada):
    depth, d, n = w_ada.shape
    bp = 8
    cp = jnp.zeros((bp, d), F32).at[: c.shape[0]].set(c)
    tn = 1536
    out = pl.pallas_call(
        _ada_kernel,
        grid=(depth, n // tn),
        in_specs=[
            pl.BlockSpec((bp, d), lambda l, j: (0, 0)),
            pl.BlockSpec((1, d, tn), lambda l, j: (l, 0, j)),
            pl.BlockSpec((1, 1, tn), lambda l, j: (l, 0, j)),
        ],
        out_specs=pl.BlockSpec((1, bp, tn), lambda l, j: (l, 0, j)),
        out_shape=jax.ShapeDtypeStruct((depth, bp, n), F32),
        compiler_params=_params("parallel", "parallel"),
        name="ada_mod",
    )(cp, w_ada, b_ada.reshape(depth, 1, n))
    return out[:, : c.shape[0]]


def _inproj_kernel(x_ref, sc_ref, sh_ref, w_ref, wvt_ref, o_ref, vt_ref, *, tn):
    u = (x_ref[0] * sc_ref[0] + sh_ref[0]).astype(MXU_DTYPE)
    for j in range(o_ref.shape[2] // tn):
        o_ref[0, :, j * tn:(j + 1) * tn] = _dot(u, w_ref[:, j * tn:(j + 1) * tn]).astype(o_ref.dtype)
    for j in range(vt_ref.shape[1] // tn):
        vt_ref[0, j * tn:(j + 1) * tn, :] = _dot_nt(wvt_ref[j * tn:(j + 1) * tn, :], u).astype(vt_ref.dtype)


def _inproj(x, scale1p, shift, w, w_v):
    b, s, d = x.shape
    n, n_v = w.shape[1], w_v.shape[1]
    tm = min(512, s)
    return pl.pallas_call(
        functools.partial(_inproj_kernel, tn=512),
        grid=(b, s // tm),
        in_specs=[
            pl.BlockSpec((1, tm, d), lambda i, j: (i, j, 0)),
            pl.BlockSpec((1, 1, d), lambda i, j: (i, 0, 0)),
            pl.BlockSpec((1, 1, d), lambda i, j: (i, 0, 0)),
            pl.BlockSpec((d, n), lambda i, j: (0, 0)),
            pl.BlockSpec((n_v, d), lambda i, j: (0, 0)),
        ],
        out_specs=[pl.BlockSpec((1, tm, n), lambda i, j: (i, j, 0)),
                   pl.BlockSpec((1, n_v, tm), lambda i, j: (i, 0, j))],
        out_shape=[jax.ShapeDtypeStruct((b, s, n), MXU_DTYPE), jax.ShapeDtypeStruct((b, n_v, s), MXU_DTYPE)],
        compiler_params=_params("parallel", "parallel"),
        name="in_proj",
    )(x, scale1p, shift, w.astype(MXU_DTYPE), w_v.T.astype(MXU_DTYPE))


MASKED = -float("inf")
ITEM_PLAIN, ITEM_DIAGONAL, ITEM_NULL = 0, 1, 2
FLAG_FIRST, FLAG_LAST = 1, 2


def _triangle_items(n_q, pad, diagonal_first, n_streams):
    streams = []
    for s in range(n_streams):
        items = []
        for qb in range(s, n_q, n_streams):
            order = range(qb, -1, -1) if diagonal_first else range(qb + 1)
            for n, kb in enumerate(order):
                flags = (FLAG_FIRST if n == 0 else 0) | (FLAG_LAST if n == qb else 0)
                items.append((qb, kb, ITEM_DIAGONAL if kb == qb else ITEM_PLAIN, flags))
        streams.append(items)
    n_items = max(len(items) for items in streams)
    null = (0, 0, ITEM_NULL, 0)
    rows = [np.asarray([null] * pad + items + [null] * (n_items - len(items) + pad + 1), np.int32).T
            for items in streams]
    return np.concatenate(rows, axis=0).copy(), n_items


SB_STAGES = 3
SB_STREAMS = 2


def _sb_kernel(tab_ref, q_ref, k_ref, vt_ref, o_ref, qh_s, acc_ref, carry_ref, mask_ref, y_s, yms_s, sums_s,
               *, tq, n_items):
    tk = tq
    extra = 16
    heads = range(2)
    streams = range(SB_STREAMS)
    lane = lax.broadcasted_iota(I32, (1, LANES), 1)
    key = lax.broadcasted_iota(I32, (tk, tq), 0)
    qry = lax.broadcasted_iota(I32, (tk, tq), 1)
    r = lax.broadcasted_iota(I32, (tk + extra, tk), 0)
    c = lax.broadcasted_iota(I32, (tk + extra, tk), 1)
    neg_later = jnp.where(r >= tk, -1.0, jnp.where(c > r, -1.0, 0.0)).astype(MXU_DTYPE)
    scale2 = SB_HEAD_DIM ** -0.5 * LOG2E

    def prepare_queries(blk, _):
        rows = pl.ds(pl.multiple_of(blk * tq, tq), tq)
        q2 = q_ref[0, rows, :]
        for h in heads:
            qh_s[h, rows, :] = (jnp.where((lane // SB_HEAD_DIM) == h, q2, jnp.zeros_like(q2)).astype(F32)
                                * scale2).astype(MXU_DTYPE)
        return 0

    lax.fori_loop(0, q_ref.shape[1] // tq, prepare_queries, 0)
    acc_ref[...] = jnp.zeros_like(acc_ref)
    carry_ref[...] = jnp.zeros_like(carry_ref)
    mask_ref[ITEM_PLAIN] = jnp.zeros((tk, tq), F32)
    mask_ref[ITEM_DIAGONAL] = jnp.where(key < qry, 0.0, MASKED)
    mask_ref[ITEM_NULL] = jnp.full((tk, tq), MASKED, F32)
    y_s[...] = jnp.zeros_like(y_s)
    yms_s[...] = jnp.full(yms_s.shape, MASKED, F32)
    sums_s[...] = jnp.zeros_like(sums_s)

    def rows_of(block, size):
        return pl.ds(pl.multiple_of(block * size, size), size)

    def trip(it, parity):
        col_x, col_y, col_c = it + 2, it + 1, it
        for s in streams:
            kj = k_ref[0, rows_of(tab_ref[4 * s + 1, col_x], tk), :]
            qrows = rows_of(tab_ref[4 * s, col_x], tq)
            for h in heads:
                y_s[2 * s + h, parity] = _dot_nt(kj, qh_s[h, qrows, :])
        for s in streams:
            mask = mask_ref[tab_ref[4 * s + 2, col_y]]
            for h in heads:
                ym = y_s[2 * s + h, 1 - parity] + mask
                sp = jnp.maximum(ym, jnp.log2(1.0 + jnp.exp2(jnp.minimum(ym, 126.0))))
                yms_s[2 * s + h, parity] = ym - sp
                sums_s[2 * s + h, parity] = _dot(neg_later, sp.astype(MXU_DTYPE))
        for s in streams:
            vtj = vt_ref[0, :, rows_of(tab_ref[4 * s + 1, col_c], tk)]
            keep = jnp.where((tab_ref[4 * s + 3, col_c] & FLAG_FIRST) != 0, 0.0, 1.0)
            for h in heads:
                sums = sums_s[2 * s + h, 1 - parity]
                carry = carry_ref[2 * s + h] * keep
                w = jnp.exp2(yms_s[2 * s + h, 1 - parity] + sums[:tk] + carry).astype(MXU_DTYPE)
                acc_ref[2 * s + h] = acc_ref[2 * s + h] * keep + _dot(vtj, w)
                carry_ref[2 * s + h] = carry + sums[tk:tk + 1]

        for s in streams:
            @pl.when((tab_ref[4 * s + 3, col_c] & FLAG_LAST) != 0)
            def _(s=s):
                sub = lax.broadcasted_iota(I32, (LANES, 1), 0)
                o_ref[0, rows_of(tab_ref[4 * s, col_c], tq), :] = jnp.where(
                    sub < SB_HEAD_DIM, acc_ref[2 * s], acc_ref[2 * s + 1]).T.astype(o_ref.dtype)

    def trip_pair(i, _):
        trip(2 * i, 0)
        trip(2 * i + 1, 1)
        return 0

    lax.fori_loop(0, pl.cdiv(n_items + SB_STAGES - 1, 2), trip_pair, 0)


def _sb_attention(proj, v_t):
    b, s, _ = proj.shape
    tq = min(256, s)
    n_pairs = SB_WIDTH // LANES
    table, n_items = _triangle_items(s // tq, SB_STAGES - 1, diagonal_first=True, n_streams=SB_STREAMS)
    n_chains = 2 * SB_STREAMS
    grid_spec = pltpu.PrefetchScalarGridSpec(
        num_scalar_prefetch=1,
        grid=(b, n_pairs),
        in_specs=[
            pl.BlockSpec((1, s, LANES), lambda i, p, t: (i, 0, p)),
            pl.BlockSpec((1, s, LANES), lambda i, p, t: (i, 0, n_pairs + p)),
            pl.BlockSpec((1, LANES, s), lambda i, p, t: (i, p, 0)),
        ],
        out_specs=pl.BlockSpec((1, s, LANES), lambda i, p, t: (i, 0, p)),
        scratch_shapes=[
            pltpu.VMEM((2, s, LANES), MXU_DTYPE),
            pltpu.VMEM((n_chains, LANES, tq), F32),
            pltpu.VMEM((n_chains, 1, tq), F32),
            pltpu.VMEM((3, tq, tq), F32),
            pltpu.VMEM((n_chains, 2, tq, tq), F32),
            pltpu.VMEM((n_chains, 2, tq, tq), F32),
            pltpu.VMEM((n_chains, 2, tq + 16, tq), F32),
        ],
    )
    return pl.pallas_call(
        functools.partial(_sb_kernel, tq=tq, n_items=n_items),
        grid_spec=grid_spec,
        out_shape=jax.ShapeDtypeStruct((b, s, SB_WIDTH), MXU_DTYPE),
        compiler_params=_params("parallel", "parallel"),
        name="sb_attention",
    )(jnp.asarray(table), proj, proj, v_t)


def _ret_kernel(lg_ref, q_ref, k_ref, v_ref, g_ref, gn_ref, o_ref, state_ref, *, tb):
    h = pl.program_id(1)
    blk = pl.program_id(2)

    @pl.when(blk == 0)
    def _():
        state_ref[...] = jnp.zeros_like(state_ref)

    lg = lg_ref[h]
    scale = RET_HEAD_DIM ** -0.5
    q = q_ref[0].astype(F32)
    k = k_ref[0].astype(F32)
    v = v_ref[0]
    row = lax.broadcasted_iota(I32, (tb, tb), 0)
    col = lax.broadcasted_iota(I32, (tb, tb), 1)
    dist = jnp.abs(row - col).astype(F32)
    decay = jnp.where(col // CHUNK <= row // CHUNK, jnp.exp(lg * dist) * scale, 0.0)
    pos = lax.broadcasted_iota(I32, (tb, 1), 0).astype(F32)
    scores = _dot_nt(q.astype(MXU_DTYPE), k.astype(MXU_DTYPE)) * decay
    intra = _dot(scores.astype(MXU_DTYPE), v)
    state = state_ref[...]
    q_in = (q * jnp.exp(lg * (pos + 1.0))).astype(MXU_DTYPE)
    inter = _dot(q_in, state.astype(MXU_DTYPE))
    k_out = (k * (jnp.exp(lg * (tb - 1.0 - pos)) * scale)).astype(MXU_DTYPE)
    block_decay = jnp.exp(lg * jnp.full((1, RET_HEAD_DIM), float(tb), F32))
    state_ref[...] = block_decay * state + _dot_tn(k_out, v)
    o = intra + inter
    mu = jnp.mean(o, axis=-1, keepdims=True)
    oc = o - mu
    var = jnp.mean(oc * oc, axis=-1, keepdims=True)
    o = oc * lax.rsqrt(var + LN_EPS) * gn_ref[0] * _silu(g_ref[0].astype(F32))
    o_ref[0] = o.astype(o_ref.dtype)


def _retention(proj, gn_gain, first_col_block):
    b, s, _ = proj.shape
    tb = min(512, s)
    log_gamma = jnp.log1p(-jnp.exp2(-5.0 - jnp.arange(RET_HEADS, dtype=F32)))

    def col(which):
        return lambda i, h, j, lg: (i, j, first_col_block + which * RET_HEADS + h)

    grid_spec = pltpu.PrefetchScalarGridSpec(
        num_scalar_prefetch=1,
        grid=(b, RET_HEADS, s // tb),
        in_specs=[pl.BlockSpec((1, tb, LANES), col(w)) for w in range(4)]
        + [pl.BlockSpec((1, 1, LANES), lambda i, h, j, lg: (h, 0, 0))],
        out_specs=pl.BlockSpec((1, tb, LANES), lambda i, h, j, lg: (i, j, h)),
        scratch_shapes=[pltpu.VMEM((RET_HEAD_DIM, RET_HEAD_DIM), F32)],
    )
    return pl.pallas_call(
        functools.partial(_ret_kernel, tb=tb),
        grid_spec=grid_spec,
        out_shape=jax.ShapeDtypeStruct((b, s, RET_WIDTH), MXU_DTYPE),
        compiler_params=_params("parallel", "parallel", "arbitrary"),
        name="retention",
    )(log_gamma, proj, proj, proj, proj, gn_gain.astype(F32).reshape(RET_HEADS, 1, RET_HEAD_DIM))


DIFF_STAGES = 3
DIFF_HEADS_PER_STEP = 2
DIFF_STREAMS = 1


def _diff_kernel(slope_ref, tab_ref, q_ref, k_ref, vt_ref, lam_ref, gain_ref, o_ref, qs_s, m_ref, l_ref, acc_ref,
                 bias_ref, z_s, p_s, a_s, lfin_s, *, tq, n_items, lambda_init):
    hp = pl.program_id(1)
    tk = tq
    nh = DIFF_HEADS_PER_STEP
    dv = LANES
    heads = range(nh)
    streams = range(DIFF_STREAMS)
    lane = lax.broadcasted_iota(I32, (1, LANES), 1)
    scale2 = DIFF_HEAD_DIM ** -0.5 * LOG2E
    slope2 = [slope_ref[hp * nh + hh] * LOG2E for hh in heads]

    def rows_of(block, size):
        return pl.ds(pl.multiple_of(block * size, size), size)

    def prepare_queries(blk, _):
        for hh in heads:
            q2 = q_ref[0, rows_of(blk, tq), hh * LANES:(hh + 1) * LANES]
            zero = jnp.zeros_like(q2)
            stacked = jnp.concatenate([jnp.where(lane < DIFF_HEAD_DIM, q2, zero),
                                       jnp.where(lane >= DIFF_HEAD_DIM, q2, zero)], axis=0)
            qs_s[hh, rows_of(blk, 2 * tq), :] = (stacked.astype(F32) * scale2).astype(MXU_DTYPE)
        return 0

    lax.fori_loop(0, q_ref.shape[1] // tq, prepare_queries, 0)
    key = lax.broadcasted_iota(I32, (tk, tq), 0)
    qry = lax.broadcasted_iota(I32, (tk, tq), 1)
    visible = key // CHUNK <= qry // CHUNK
    for hh in heads:
        plain = slope2[hh] * key.astype(F32)
        diag = jnp.where(visible, slope2[hh] * (qry - jnp.abs(qry - key)).astype(F32), MASKED)
        bias_ref[hh, ITEM_PLAIN] = jnp.concatenate([plain, plain], axis=1)
        bias_ref[hh, ITEM_DIAGONAL] = jnp.concatenate([diag, diag], axis=1)
    m_ref[...] = jnp.zeros_like(m_ref)
    l_ref[...] = jnp.zeros_like(l_ref)
    acc_ref[...] = jnp.zeros_like(acc_ref)
    z_s[...] = jnp.zeros_like(z_s)
    p_s[...] = jnp.zeros_like(p_s)
    a_s[...] = jnp.ones_like(a_s)
    lfin_s[...] = jnp.ones_like(lfin_s)

    def trip(it, parity):
        col_a, col_a1, col_c = it + 2, it + 1, it
        for s in streams:
            krows = rows_of(tab_ref[4 * s + 1, col_a], tk)
            qrows = rows_of(tab_ref[4 * s, col_a], 2 * tq)
            for hh in heads:
                z_s[nh * s + hh, parity] = _dot_nt(k_ref[0, krows, hh * LANES:(hh + 1) * LANES], qs_s[hh, qrows, :])
        lfin = {}
        for s in streams:
            vrows = rows_of(tab_ref[4 * s + 1, col_c], tk)
            for hh in heads:
                c = nh * s + hh
                acc_ref[c] = a_s[c] * acc_ref[c] + _dot(vt_ref[0, hh * dv:(hh + 1) * dv, vrows], p_s[c])
                lfin[c] = lfin_s[c]
        for s in streams:
            kind = tab_ref[4 * s + 2, col_a1]
            first = (tab_ref[4 * s + 3, col_a1] & FLAG_FIRST) != 0
            offset = ((tab_ref[4 * s + 1, col_a1] - tab_ref[4 * s, col_a1]) * tq).astype(F32)
            for hh in heads:
                c = nh * s + hh
                shift = jnp.where(kind == ITEM_PLAIN, slope2[hh] * offset,
                                  jnp.where(kind == ITEM_DIAGONAL, 0.0, MASKED))
                z = z_s[c, 1 - parity] + bias_ref[hh, jnp.minimum(kind, ITEM_DIAGONAL)]
                m_old = jnp.where(first, MASKED, m_ref[c])
                m_new = jnp.maximum(m_old, jnp.max(z, axis=0, keepdims=True) + shift)
                p = jnp.exp2(z - (m_new - shift))
                a = jnp.exp2(m_old - m_new)
                l_new = a * l_ref[c] + jnp.sum(p, axis=0, keepdims=True)
                p_s[c] = p.astype(MXU_DTYPE)
                a_s[c] = a
                lfin_s[c] = l_new
                l_ref[c] = l_new
                m_ref[c] = m_new

        for s in streams:
            @pl.when((tab_ref[4 * s + 3, col_c] & FLAG_LAST) != 0)
            def _(s=s):
                lam_v = lam_ref[...]
                lam = (jnp.exp(jnp.sum(lam_v[0:1] * lam_v[1:2], axis=-1, keepdims=True))
                       - jnp.exp(jnp.sum(lam_v[2:3] * lam_v[3:4], axis=-1, keepdims=True)) + lambda_init)
                orows = rows_of(tab_ref[4 * s, col_c], tq)
                for hh in heads:
                    c = nh * s + hh
                    o = acc_ref[c] * (1.0 / lfin[c])
                    o = o[:, :tq] - lam * o[:, tq:]
                    o = o * lax.rsqrt(jnp.mean(o * o, axis=0, keepdims=True) + LN_EPS)
                    o_ref[0, orows, hh * dv:(hh + 1) * dv] = (o * gain_ref[...] * (1.0 - lambda_init)).T.astype(
                        o_ref.dtype)

    def trip_pair(i, _):
        trip(2 * i, 0)
        trip(2 * i + 1, 1)
        return 0

    lax.fori_loop(0, pl.cdiv(n_items + DIFF_STAGES - 1, 2), trip_pair, 0)


def _diff_attention(proj, v_t, lam_rows, subln_gain, lambda_init):
    b, s, _ = proj.shape
    tq = min(256, s)
    dv = subln_gain.shape[-1]
    nh = DIFF_HEADS_PER_STEP
    slopes = jnp.exp2(-8.0 / DIFF_HEADS * (jnp.arange(DIFF_HEADS, dtype=F32) + 1.0))
    kb = DIFF_QK // (nh * LANES)
    table, n_items = _triangle_items(s // tq, DIFF_STAGES - 1, diagonal_first=False, n_streams=DIFF_STREAMS)
    n_chains = nh * DIFF_STREAMS
    grid_spec = pltpu.PrefetchScalarGridSpec(
        num_scalar_prefetch=2,
        grid=(b, DIFF_HEADS // nh),
        in_specs=[
            pl.BlockSpec((1, s, nh * LANES), lambda i, h, sl, t: (i, 0, h)),
            pl.BlockSpec((1, s, nh * LANES), lambda i, h, sl, t: (i, 0, kb + h)),
            pl.BlockSpec((1, nh * dv, s), lambda i, h, sl, t: (i, h, 0)),
            pl.BlockSpec((4, DIFF_HEAD_DIM), lambda i, h, sl, t: (0, 0)),
            pl.BlockSpec((dv, 1), lambda i, h, sl, t: (0, 0)),
        ],
        out_specs=pl.BlockSpec((1, s, nh * dv), lambda i, h, sl, t: (i, 0, h)),
        scratch_shapes=[
            pltpu.VMEM((nh, 2 * s, LANES), MXU_DTYPE),
            pltpu.VMEM((n_chains, 1, 2 * tq), F32),
            pltpu.VMEM((n_chains, 1, 2 * tq), F32),
            pltpu.VMEM((n_chains, dv, 2 * tq), F32),
            pltpu.VMEM((nh, 2, tq, 2 * tq), F32),
            pltpu.VMEM((n_chains, 2, tq, 2 * tq), F32),
            pltpu.VMEM((n_chains, tq, 2 * tq), MXU_DTYPE),
            pltpu.VMEM((n_chains, 1, 2 * tq), F32),
            pltpu.VMEM((n_chains, 1, 2 * tq), F32),
        ],
    )
    return pl.pallas_call(
        functools.partial(_diff_kernel, tq=tq, n_items=n_items, lambda_init=lambda_init),
        grid_spec=grid_spec,
        out_shape=jax.ShapeDtypeStruct((b, s, DIFF_HEADS * dv), MXU_DTYPE),
        compiler_params=_params("parallel", "parallel"),
        name="diff_attention",
    )(slopes, jnp.asarray(table), proj, proj, v_t, lam_rows.astype(F32), subln_gain.astype(F32).reshape(dv, 1))


def _layer_norm(y, g, b):
    mu = jnp.mean(y, axis=-1, keepdims=True)
    yc = y - mu
    var = jnp.mean(yc * yc, axis=-1, keepdims=True)
    return yc * lax.rsqrt(var + LN_EPS) * g + b


def _outproj_kernel(*refs, n_in, alpha):
    a_refs, w_refs = refs[:n_in], refs[n_in:2 * n_in]
    x_ref, gate_ref, lng_ref, lnb_ref, sc_ref, sh_ref, wr_ref, br_ref, xo_ref, u_ref, lg_ref = refs[2 * n_in:]
    mix = _dot(a_refs[0][0], w_refs[0][...])
    for a_ref, w_ref in zip(a_refs[1:], w_refs[1:]):
        mix += _dot(a_ref[0], w_ref[...])
    xn = _layer_norm(alpha * x_ref[0] + gate_ref[0] * mix, lng_ref[...], lnb_ref[...])
    xo_ref[0] = xn
    u = xn * sc_ref[0] + sh_ref[0]
    u_ref[0] = u
    lg_ref[...] = lax.dot_general(wr_ref[...], u, (((1,), (1,)), ((), ())), preferred_element_type=F32,
                                  precision=HIGHEST) + br_ref[...]


def _outproj(parts, weights, x, gate1p, ln_g, ln_b, scale1p, shift, w_router_t, b_router, alpha):
    b, s, d = x.shape
    tm = min(512, s)
    n_in = len(parts)
    vec = pl.BlockSpec((1, 1, d), lambda i, j: (i, 0, 0))
    row = pl.BlockSpec((1, d), lambda i, j: (0, 0))
    in_specs = [pl.BlockSpec((1, tm, p.shape[-1]), lambda i, j: (i, j, 0)) for p in parts]
    in_specs += [pl.BlockSpec(w.shape, lambda i, j: (0, 0)) for w in weights]
    in_specs += [pl.BlockSpec((1, tm, d), lambda i, j: (i, j, 0)), vec, row, row, vec, vec,
                 pl.BlockSpec((ROUTER_ROWS, d), lambda i, j: (0, 0)),
                 pl.BlockSpec((ROUTER_ROWS, 1), lambda i, j: (0, 0))]
    nb = s // tm
    return pl.pallas_call(
        functools.partial(_outproj_kernel, n_in=n_in, alpha=alpha),
        grid=(b, nb),
        in_specs=in_specs,
        out_specs=[pl.BlockSpec((1, tm, d), lambda i, j: (i, j, 0)),
                   pl.BlockSpec((1, tm, d), lambda i, j: (i, j, 0)),
                   pl.BlockSpec((ROUTER_ROWS, tm), lambda i, j: (0, i * nb + j))],
        out_shape=[jax.ShapeDtypeStruct((b, s, d), F32), jax.ShapeDtypeStruct((b, s, d), F32),
                   jax.ShapeDtypeStruct((ROUTER_ROWS, b * s), F32)],
        compiler_params=_params("parallel", "parallel"),
        name="out_proj_norm",
    )(*parts, *[w.astype(MXU_DTYPE) for w in weights], x, gate1p, ln_g.reshape(1, d), ln_b.reshape(1, d),
      scale1p, shift, w_router_t, b_router)


def _route_kernel(lg_ref, ids_ref, gates_ref, cnt_ref, run_ref, *, tm):
    @pl.when(pl.program_id(0) == 0)
    def _():
        run_ref[...] = jnp.zeros_like(run_ref)

    lg = lg_ref[...]
    g0 = N_EXPERTS
    g_max = lg[g0:g0 + 1]
    grp = jnp.zeros((1, tm), I32)
    for i in range(1, N_GROUPS):
        gi = lg[g0 + i:g0 + i + 1]
        better = gi > g_max
        grp = jnp.where(better, i, grp)
        g_max = jnp.where(better, gi, g_max)
    den = jnp.exp(lg[g0:g0 + 1] - g_max)
    for i in range(1, N_GROUPS):
        den += jnp.exp(lg[g0 + i:g0 + i + 1] - g_max)
    p_grp = 1.0 / den

    cand = lg[0:EXPERTS_PER_GROUP]
    for g in range(1, N_GROUPS):
        cand = jnp.where(grp == g, lg[g * EXPERTS_PER_GROUP:(g + 1) * EXPERTS_PER_GROUP], cand)
    ridx = lax.broadcasted_iota(I32, (EXPERTS_PER_GROUP, tm), 0).astype(F32)
    none = float(EXPERTS_PER_GROUP)
    v1 = jnp.max(cand, axis=0, keepdims=True)
    i1 = jnp.min(jnp.where(cand == v1, ridx, none), axis=0, keepdims=True)
    rest = jnp.where(ridx == i1, -jnp.inf, cand)
    v2 = jnp.max(rest, axis=0, keepdims=True)
    i2 = jnp.min(jnp.where(rest == v2, ridx, none), axis=0, keepdims=True)
    e21 = jnp.exp(v2 - v1)
    gate1 = p_grp / (1.0 + e21)
    gate2 = p_grp * e21 / (1.0 + e21)
    ex1 = grp * EXPERTS_PER_GROUP + i1.astype(I32)
    ex2 = grp * EXPERTS_PER_GROUP + i2.astype(I32)

    eidx = lax.broadcasted_iota(I32, (N_EXPERTS, tm), 0)
    oh1 = jnp.where(eidx == ex1, 1.0, 0.0)
    oh2 = jnp.where(eidx == ex2, 1.0, 0.0)
    oh = (oh1 + oh2).astype(MXU_DTYPE)
    earlier = jnp.where(lax.broadcasted_iota(I32, (tm, tm), 0) < lax.broadcasted_iota(I32, (tm, tm), 1),
                        1.0, 0.0).astype(MXU_DTYPE)
    run = run_ref[...]
    before = _dot(oh, earlier) + jnp.concatenate([run] * (tm // LANES), axis=1)
    rank1 = jnp.sum(oh1 * before, axis=0, keepdims=True).astype(I32)
    rank2 = jnp.sum(oh2 * before, axis=0, keepdims=True).astype(I32)
    run = run + _dot(oh, jnp.ones((tm, LANES), MXU_DTYPE))
    run_ref[...] = run
    cnt_ref[...] = run
    ids_ref[...] = jnp.concatenate([ex1, ex2, rank1, rank2, jnp.zeros((4, tm), I32)], axis=0)
    gates_ref[...] = jnp.concatenate([gate1, gate2, jnp.zeros((6, tm), F32)], axis=0)


def _route(logits_t):
    t = logits_t.shape[1]
    tm = min(512, t)
    return pl.pallas_call(
        functools.partial(_route_kernel, tm=tm),
        grid=(t // tm,),
        in_specs=[pl.BlockSpec((ROUTER_ROWS, tm), lambda i: (0, i))],
        out_specs=[pl.BlockSpec((8, tm), lambda i: (0, i)), pl.BlockSpec((8, tm), lambda i: (0, i)),
                   pl.BlockSpec((N_EXPERTS, LANES), lambda i: (0, 0))],
        out_shape=[jax.ShapeDtypeStruct((8, t), I32), jax.ShapeDtypeStruct((8, t), F32),
                   jax.ShapeDtypeStruct((N_EXPERTS, LANES), F32)],
        scratch_shapes=[pltpu.VMEM((N_EXPERTS, LANES), F32)],
        compiler_params=_params("arbitrary"),
        name="route",
    )(logits_t)


def _row_copy(idx_ref, src_hbm, dst, sem, r):
    return pltpu.make_async_copy(src_hbm.at[pl.ds(idx_ref[0, 0, r], 1), :], dst.at[pl.ds(r, 1), :], sem)


def _gather_rows(idx_ref, src_hbm, dst, sem, n):
    def body(r, _):
        _row_copy(idx_ref, src_hbm, dst, sem, r).start()
        return 0

    lax.fori_loop(0, n, body, 0)


def _gather_rows_inline(idx_ref, src_hbm, dst, sem, n):
    for r in range(n):
        _row_copy(idx_ref, src_hbm, dst, sem, r).start()


def _wait_rows(src_hbm, dst, sem, n):
    pltpu.make_async_copy(src_hbm.at[pl.ds(0, n), :], dst, sem).wait()


def _expert_kernel(be_ref, idx_a_ref, idx_b_ref, idx_next_ref, x_hbm, w1a_ref, w3a_ref, w2a_ref, w1b_ref, w3b_ref,
                   w2b_ref, y_ref, xbuf_a, xbuf_b, sem, *, tm):
    i = pl.program_id(0)

    @pl.when(i == 0)
    def _():
        _gather_rows(idx_a_ref, x_hbm, xbuf_a, sem.at[0], tm)

    def block(cur, cur_sem, w1_ref, w3_ref, w2_ref, out_rows, nxt_idx_ref, nxt, nxt_sem):
        _gather_rows_inline(nxt_idx_ref, x_hbm, nxt, nxt_sem, tm)
        _wait_rows(x_hbm, cur, cur_sem, tm)
        xb = cur[...].astype(MXU_DTYPE)
        hidden = _silu(_dot(xb, w1_ref[0, 0].astype(MXU_DTYPE))) * _dot(xb, w3_ref[0, 0].astype(MXU_DTYPE))
        y_ref[out_rows, :] = _dot(hidden.astype(MXU_DTYPE), w2_ref[0, 0].astype(MXU_DTYPE))

    block(xbuf_a, sem.at[0], w1a_ref, w3a_ref, w2a_ref, slice(0, tm), idx_b_ref, xbuf_b, sem.at[1])
    block(xbuf_b, sem.at[1], w1b_ref, w3b_ref, w2b_ref, slice(tm, 2 * tm), idx_next_ref, xbuf_a, sem.at[0])

    @pl.when(i == pl.num_programs(0) - 1)
    def _():
        _wait_rows(x_hbm, xbuf_a, sem.at[0], tm)


def _experts(u_flat, slot_tok, blk_expert, w1, w3, w2, layer, tm):
    t, d = u_flat.shape
    n_blk = slot_tok.shape[0] // tm
    assert n_blk % 2 == 0
    ff = w1.shape[-1]
    idx3 = slot_tok.reshape(n_blk, 1, tm)
    last = n_blk - 1

    def idx_spec(block_of):
        return pl.BlockSpec((1, 1, tm), lambda i, be: (block_of(i), 0, 0), memory_space=pltpu.SMEM)

    def weight_specs(which):
        return [pl.BlockSpec((1, 1, d, ff), lambda i, be: (layer, be[2 * i + which], 0, 0)),
                pl.BlockSpec((1, 1, d, ff), lambda i, be: (layer, be[2 * i + which], 0, 0)),
                pl.BlockSpec((1, 1, ff, d), lambda i, be: (layer, be[2 * i + which], 0, 0))]

    grid_spec = pltpu.PrefetchScalarGridSpec(
        num_scalar_prefetch=1,
        grid=(n_blk // 2,),
        in_specs=[idx_spec(lambda i: 2 * i), idx_spec(lambda i: 2 * i + 1),
                  idx_spec(lambda i: jnp.minimum(2 * i + 2, last)), pl.BlockSpec(memory_space=pl.ANY)]
        + weight_specs(0) + weight_specs(1),
        out_specs=pl.BlockSpec((2 * tm, d), lambda i, be: (i, 0)),
        scratch_shapes=[pltpu.VMEM((tm, d), F32), pltpu.VMEM((tm, d), F32), pltpu.SemaphoreType.DMA((2,))],
    )
    return pl.pallas_call(
        functools.partial(_expert_kernel, tm=tm),
        grid_spec=grid_spec,
        out_shape=jax.ShapeDtypeStruct((n_blk * tm, d), F32),
        compiler_params=_params("arbitrary"),
        name="experts",
    )(blk_expert, idx3, idx3, idx3, u_flat, w1, w3, w2, w1, w3, w2)


def _combine_kernel(d1_ref, d2_ref, d1n_ref, d2n_ref, y_hbm, gates_ref, x_ref, gate_ref, lng_ref, lnb_ref,
                    o_ref, ybuf, sem, *, tm, alpha):
    i = pl.program_id(0)
    n = pl.num_programs(0)
    slot = i % 2

    def wait(s):
        _wait_rows(y_hbm, ybuf.at[s, 0], sem.at[s, 0], tm)
        _wait_rows(y_hbm, ybuf.at[s, 1], sem.at[s, 1], tm)

    @pl.when(i == 0)
    def _():
        _gather_rows(d1_ref, y_hbm, ybuf.at[0, 0], sem.at[0, 0], tm)
        _gather_rows(d2_ref, y_hbm, ybuf.at[0, 1], sem.at[0, 1], tm)

    _gather_rows_inline(d1n_ref, y_hbm, ybuf.at[1 - slot, 0], sem.at[1 - slot, 0], tm)
    _gather_rows_inline(d2n_ref, y_hbm, ybuf.at[1 - slot, 1], sem.at[1 - slot, 1], tm)
    wait(slot)
    gates = gates_ref[...]
    f = gates[:, 0:1] * ybuf[slot, 0] + gates[:, 1:2] * ybuf[slot, 1]
    o_ref[...] = _layer_norm(alpha * x_ref[...] + gate_ref[0] * f, lng_ref[...], lnb_ref[...])

    @pl.when(i == n - 1)
    def _():
        wait(1 - slot)


def _combine(y_slots, dest1, dest2, gates, x_flat, gate1p, ln_g, ln_b, seq, alpha):
    t, d = x_flat.shape
    tm = min(256, seq)
    n_blk = t // tm
    per_seq = seq // tm
    d1 = dest1.reshape(n_blk, 1, tm)
    d2 = dest2.reshape(n_blk, 1, tm)
    cur = pl.BlockSpec((1, 1, tm), lambda i: (i, 0, 0), memory_space=pltpu.SMEM)
    nxt = pl.BlockSpec((1, 1, tm), lambda i: (jnp.minimum(i + 1, n_blk - 1), 0, 0), memory_space=pltpu.SMEM)
    row = pl.BlockSpec((1, d), lambda i: (0, 0))
    return pl.pallas_call(
        functools.partial(_combine_kernel, tm=tm, alpha=alpha),
        grid=(n_blk,),
        in_specs=[cur, cur, nxt, nxt, pl.BlockSpec(memory_space=pl.ANY),
                  pl.BlockSpec((tm, TOP_K), lambda i: (i, 0)),
                  pl.BlockSpec((tm, d), lambda i: (i, 0)),
                  pl.BlockSpec((1, 1, d), lambda i: (i // per_seq, 0, 0)), row, row],
        out_specs=pl.BlockSpec((tm, d), lambda i: (i, 0)),
        out_shape=jax.ShapeDtypeStruct((t, d), F32),
        scratch_shapes=[pltpu.VMEM((2, 2, tm, d), F32), pltpu.SemaphoreType.DMA((2, 2))],
        compiler_params=_params("arbitrary"),
        name="combine_norm",
    )(d1, d2, d1, d2, y_slots, gates, x_flat, gate1p, ln_g.reshape(1, d), ln_b.reshape(1, d))


EXPERT_TILE = 256
SLOT_STRIDE = 7919

def _moe(u, logits_t, x, gate2p, ln_g, ln_b, w1, w3, w2, layer, alpha):
    b, s, d = x.shape
    t = b * s
    ids, gates8, cnt = _route(logits_t)
    counts = cnt[:, 0].astype(I32)
    padded = (counts + EXPERT_TILE - 1) // EXPERT_TILE * EXPERT_TILE
    pad_end = jnp.cumsum(padded)
    pad_start = pad_end - padded
    def slot(e, rank):
        return pad_start[e] + (rank * SLOT_STRIDE) % padded[e]

    dest1 = slot(ids[0], ids[2])
    dest2 = slot(ids[1], ids[3])
    n_slots = t * TOP_K + N_EXPERTS * EXPERT_TILE
    n_blk = n_slots // EXPERT_TILE
    tok = jnp.arange(t, dtype=I32)
    slot_tok = jnp.zeros((n_slots,), I32).at[jnp.concatenate([dest1, dest2])].set(jnp.concatenate([tok, tok]))
    blk_start = jnp.arange(n_blk, dtype=I32) * EXPERT_TILE
    blk_expert = jnp.minimum(jnp.sum((pad_end[None, :] <= blk_start[:, None]).astype(I32), axis=1), N_EXPERTS - 1)
    y_slots = _experts(u.reshape(t, d), slot_tok, blk_expert.astype(I32), w1, w3, w2, layer, EXPERT_TILE)
    out = _combine(y_slots, dest1, dest2, gates8[:TOP_K].T, x.reshape(t, d), gate2p, ln_g, ln_b, s, alpha)
    return out.reshape(b, s, d)


def kernel(x, c, ln1_g, ln1_b, ln2_g, ln2_b, w_ada, b_ada, even_w_in, even_w_out, ret_gn_g, odd_w_in, odd_w_out, lambda_q1, lambda_k1, lambda_q2, lambda_k2, diff_subln_g, moe_w_group, moe_b_group, moe_w_router, moe_b_router, moe_w1, moe_w3, moe_w2):
    b, s, d = x.shape
    depth = w_ada.shape[0]
    alpha = (2.0 * depth) ** 0.25
    mod = _ada(c, w_ada, b_ada)
    for l in range(depth):
        sh1, sc1, g1, sh2, sc2, g2 = [m[:, None, :] for m in jnp.split(mod[l], 6, axis=-1)]
        i = l // 2
        if l % 2 == 0:
            w_in = even_w_in[i]
            w_main = jnp.concatenate([w_in[:, :2 * SB_WIDTH], w_in[:, 3 * SB_WIDTH:]], axis=1)
            proj, v_t = _inproj(x, 1.0 + sc1, sh1, w_main, w_in[:, 2 * SB_WIDTH:3 * SB_WIDTH])
            parts = [_sb_attention(proj, v_t), _retention(proj, ret_gn_g[i], 2 * SB_WIDTH // LANES)]
            w_out = even_w_out[i]
            weights = [w_out[:SB_WIDTH], w_out[SB_WIDTH:]]
        else:
            w_in = odd_w_in[i]
            proj, v_t = _inproj(x, 1.0 + sc1, sh1, w_in[:, :2 * DIFF_QK], w_in[:, 2 * DIFF_QK:])
            lambda_init = 0.8 - 0.6 * math.exp(-0.3 * l)
            lam_rows = jnp.stack([lambda_q1[i], lambda_k1[i], lambda_q2[i], lambda_k2[i]])
            parts = [_diff_attention(proj, v_t, lam_rows, diff_subln_g[i], lambda_init)]
            weights = [odd_w_out[i]]
        w_router_t = jnp.zeros((ROUTER_ROWS, d), F32).at[:N_EXPERTS].set(moe_w_router[l].T.astype(F32))
        w_router_t = w_router_t.at[N_EXPERTS:N_EXPERTS + N_GROUPS].set(moe_w_group[l].T.astype(F32))
        b_router = jnp.zeros((ROUTER_ROWS, 1), F32).at[:N_EXPERTS, 0].set(moe_b_router[l].astype(F32))
        b_router = b_router.at[N_EXPERTS:N_EXPERTS + N_GROUPS, 0].set(moe_b_group[l].astype(F32))
        x, u, logits_t = _outproj(parts, weights, x, 1.0 + g1, ln1_g[l], ln1_b[l], 1.0 + sc2, sh2,
                                  w_router_t, b_router, alpha)
        x = _moe(u, logits_t, x, 1.0 + g2, ln2_g[l], ln2_b[l], moe_w1, moe_w3, moe_w2, l, alpha)
    return x
```

```python
import functools
import math

import jax
import jax.numpy as jnp
import numpy as np
from jax import lax
from jax.experimental import pallas as pl
from jax.experimental.pallas import tpu as pltpu

F32 = jnp.float32
I32 = jnp.int32
MXU_DTYPE = jnp.bfloat16
HIGHEST = lax.Precision.HIGHEST
LOG2E = math.log2(math.e)

LN_EPS = 1e-5
CHUNK = 64
LANES = 128
SB_HEADS, SB_HEAD_DIM = 8, 64
RET_HEADS, RET_HEAD_DIM = 4, 128
DIFF_HEADS, DIFF_HEAD_DIM = 8, 64
SB_WIDTH = SB_HEADS * SB_HEAD_DIM
RET_WIDTH = RET_HEADS * RET_HEAD_DIM
DIFF_QK = DIFF_HEADS * 2 * DIFF_HEAD_DIM
N_GROUPS, EXPERTS_PER_GROUP = 4, 8
N_EXPERTS = N_GROUPS * EXPERTS_PER_GROUP
TOP_K = 2
ROUTER_ROWS = 40

VMEM_LIMIT = 56 * 1024 * 1024

MXU_TILE = 256
ATTENTION_TILE = MXU_TILE
TOKEN_TILE = 2 * MXU_TILE
COMBINE_TILE = MXU_TILE
PROJ_COLS = 2 * MXU_TILE
ADA_COLS = 6 * MXU_TILE
EXP2_MAX = 126.0


def _params(*sem):
    return pltpu.CompilerParams(dimension_semantics=sem, vmem_limit_bytes=VMEM_LIMIT)


def _dot(a, b):
    return jnp.dot(a, b, preferred_element_type=F32)


def _dot_nt(a, b):
    return lax.dot_general(a, b, (((1,), (1,)), ((), ())), preferred_element_type=F32)


def _dot_tn(a, b):
    return lax.dot_general(a, b, (((0,), (0,)), ((), ())), preferred_element_type=F32)


def _silu(x):
    return x * (1.0 / (1.0 + jnp.exp(-x)))


def _ada_kernel(c_ref, w_ref, b_ref, o_ref):
    o_ref[0] = jnp.dot(_silu(c_ref[...]), w_ref[0], preferred_element_type=F32, precision=HIGHEST) + b_ref[0]


def _ada(c, w_ada, b_ada):
    depth, d, n = w_ada.shape
    bp = 8
    cp = jnp.zeros((bp, d), F32).at[: c.shape[0]].set(c)
    tn = ADA_COLS
    out = pl.pallas_call(
        _ada_kernel,
        grid=(depth, n // tn),
        in_specs=[
            pl.BlockSpec((bp, d), lambda l, j: (0, 0)),
            pl.BlockSpec((1, d, tn), lambda l, j: (l, 0, j)),
            pl.BlockSpec((1, 1, tn), lambda l, j: (l, 0, j)),
        ],
        out_specs=pl.BlockSpec((1, bp, tn), lambda l, j: (l, 0, j)),
        out_shape=jax.ShapeDtypeStruct((depth, bp, n), F32),
        compiler_params=_params("parallel", "parallel"),
        name="ada_mod",
    )(cp, w_ada, b_ada.reshape(depth, 1, n))
    return out[:, : c.shape[0]]


def _inproj_kernel(x_ref, sc_ref, sh_ref, w_ref, wvt_ref, o_ref, vt_ref, *, tn):
    u = (x_ref[0] * sc_ref[0] + sh_ref[0]).astype(MXU_DTYPE)
    for j in range(o_ref.shape[2] // tn):
        o_ref[0, :, j * tn:(j + 1) * tn] = _dot(u, w_ref[:, j * tn:(j + 1) * tn]).astype(o_ref.dtype)
    for j in range(vt_ref.shape[1] // tn):
        vt_ref[0, j * tn:(j + 1) * tn, :] = _dot_nt(wvt_ref[j * tn:(j + 1) * tn, :], u).astype(vt_ref.dtype)


def _inproj(x, scale1p, shift, w, w_v):
    b, s, d = x.shape
    n, n_v = w.shape[1], w_v.shape[1]
    tm = min(TOKEN_TILE, s)
    return pl.pallas_call(
        functools.partial(_inproj_kernel, tn=PROJ_COLS),
        grid=(b, s // tm),
        in_specs=[
            pl.BlockSpec((1, tm, d), lambda i, j: (i, j, 0)),
            pl.BlockSpec((1, 1, d), lambda i, j: (i, 0, 0)),
            pl.BlockSpec((1, 1, d), lambda i, j: (i, 0, 0)),
            pl.BlockSpec((d, n), lambda i, j: (0, 0)),
            pl.BlockSpec((n_v, d), lambda i, j: (0, 0)),
        ],
        out_specs=[pl.BlockSpec((1, tm, n), lambda i, j: (i, j, 0)),
                   pl.BlockSpec((1, n_v, tm), lambda i, j: (i, 0, j))],
        out_shape=[jax.ShapeDtypeStruct((b, s, n), MXU_DTYPE), jax.ShapeDtypeStruct((b, n_v, s), MXU_DTYPE)],
        compiler_params=_params("parallel", "parallel"),
        name="in_proj",
    )(x, scale1p, shift, w.astype(MXU_DTYPE), w_v.T.astype(MXU_DTYPE))


MASKED = -float("inf")
ITEM_PLAIN, ITEM_DIAGONAL, ITEM_NULL = 0, 1, 2
FLAG_FIRST, FLAG_LAST = 1, 2


def _triangle_items(n_q, pad, diagonal_first, n_streams):
    streams = []
    for s in range(n_streams):
        items = []
        for qb in range(s, n_q, n_streams):
            order = range(qb, -1, -1) if diagonal_first else range(qb + 1)
            for n, kb in enumerate(order):
                flags = (FLAG_FIRST if n == 0 else 0) | (FLAG_LAST if n == qb else 0)
                items.append((qb, kb, ITEM_DIAGONAL if kb == qb else ITEM_PLAIN, flags))
        streams.append(items)
    n_items = max(len(items) for items in streams)
    null = (0, 0, ITEM_NULL, 0)
    rows = [np.asarray([null] * pad + items + [null] * (n_items - len(items) + pad + 1), np.int32).T
            for items in streams]
    return np.concatenate(rows, axis=0).copy(), n_items


SB_STAGES = 3
SUM_ROWS = 16
SB_STREAMS = 2


def _sb_kernel(tab_ref, q_ref, k_ref, vt_ref, o_ref, qh_s, acc_ref, carry_ref, mask_ref, y_s, yms_s, sums_s,
               *, tq, n_items):
    tk = tq
    extra = SUM_ROWS
    heads = range(2)
    streams = range(SB_STREAMS)
    lane = lax.broadcasted_iota(I32, (1, LANES), 1)
    key = lax.broadcasted_iota(I32, (tk, tq), 0)
    qry = lax.broadcasted_iota(I32, (tk, tq), 1)
    r = lax.broadcasted_iota(I32, (tk + extra, tk), 0)
    c = lax.broadcasted_iota(I32, (tk + extra, tk), 1)
    neg_later = jnp.where(r >= tk, -1.0, jnp.where(c > r, -1.0, 0.0)).astype(MXU_DTYPE)
    scale2 = SB_HEAD_DIM ** -0.5 * LOG2E

    def prepare_queries(blk, _):
        rows = pl.ds(pl.multiple_of(blk * tq, tq), tq)
        q2 = q_ref[0, rows, :]
        for h in heads:
            qh_s[h, rows, :] = (jnp.where((lane // SB_HEAD_DIM) == h, q2, jnp.zeros_like(q2)).astype(F32)
                                * scale2).astype(MXU_DTYPE)
        return 0

    lax.fori_loop(0, q_ref.shape[1] // tq, prepare_queries, 0)
    acc_ref[...] = jnp.zeros_like(acc_ref)
    carry_ref[...] = jnp.zeros_like(carry_ref)
    mask_ref[ITEM_PLAIN] = jnp.zeros((tk, tq), F32)
    mask_ref[ITEM_DIAGONAL] = jnp.where(key < qry, 0.0, MASKED)
    mask_ref[ITEM_NULL] = jnp.full((tk, tq), MASKED, F32)
    y_s[...] = jnp.zeros_like(y_s)
    yms_s[...] = jnp.full(yms_s.shape, MASKED, F32)
    sums_s[...] = jnp.zeros_like(sums_s)

    def rows_of(block, size):
        return pl.ds(pl.multiple_of(block * size, size), size)

    def trip(it, parity):
        col_x, col_y, col_c = it + 2, it + 1, it
        for s in streams:
            kj = k_ref[0, rows_of(tab_ref[4 * s + 1, col_x], tk), :]
            qrows = rows_of(tab_ref[4 * s, col_x], tq)
            for h in heads:
                y_s[2 * s + h, parity] = _dot_nt(kj, qh_s[h, qrows, :])
        for s in streams:
            mask = mask_ref[tab_ref[4 * s + 2, col_y]]
            for h in heads:
                ym = y_s[2 * s + h, 1 - parity] + mask
                sp = jnp.maximum(ym, jnp.log2(1.0 + jnp.exp2(jnp.minimum(ym, EXP2_MAX))))
                yms_s[2 * s + h, parity] = ym - sp
                sums_s[2 * s + h, parity] = _dot(neg_later, sp.astype(MXU_DTYPE))
        for s in streams:
            vtj = vt_ref[0, :, rows_of(tab_ref[4 * s + 1, col_c], tk)]
            keep = jnp.where((tab_ref[4 * s + 3, col_c] & FLAG_FIRST) != 0, 0.0, 1.0)
            for h in heads:
                sums = sums_s[2 * s + h, 1 - parity]
                carry = carry_ref[2 * s + h] * keep
                w = jnp.exp2(yms_s[2 * s + h, 1 - parity] + sums[:tk] + carry).astype(MXU_DTYPE)
                acc_ref[2 * s + h] = acc_ref[2 * s + h] * keep + _dot(vtj, w)
                carry_ref[2 * s + h] = carry + sums[tk:tk + 1]

        for s in streams:
            @pl.when((tab_ref[4 * s + 3, col_c] & FLAG_LAST) != 0)
            def _(s=s):
                sub = lax.broadcasted_iota(I32, (LANES, 1), 0)
                o_ref[0, rows_of(tab_ref[4 * s, col_c], tq), :] = jnp.where(
                    sub < SB_HEAD_DIM, acc_ref[2 * s], acc_ref[2 * s + 1]).T.astype(o_ref.dtype)

    def trip_pair(i, _):
        trip(2 * i, 0)
        trip(2 * i + 1, 1)
        return 0

    lax.fori_loop(0, pl.cdiv(n_items + SB_STAGES - 1, 2), trip_pair, 0)


def _sb_attention(proj, v_t):
    b, s, _ = proj.shape
    tq = min(ATTENTION_TILE, s)
    n_pairs = SB_WIDTH // LANES
    table, n_items = _triangle_items(s // tq, SB_STAGES - 1, diagonal_first=True, n_streams=SB_STREAMS)
    n_chains = 2 * SB_STREAMS
    grid_spec = pltpu.PrefetchScalarGridSpec(
        num_scalar_prefetch=1,
        grid=(b, n_pairs),
        in_specs=[
            pl.BlockSpec((1, s, LANES), lambda i, p, t: (i, 0, p)),
            pl.BlockSpec((1, s, LANES), lambda i, p, t: (i, 0, n_pairs + p)),
            pl.BlockSpec((1, LANES, s), lambda i, p, t: (i, p, 0)),
        ],
        out_specs=pl.BlockSpec((1, s, LANES), lambda i, p, t: (i, 0, p)),
        scratch_shapes=[
            pltpu.VMEM((2, s, LANES), MXU_DTYPE),
            pltpu.VMEM((n_chains, LANES, tq), F32),
            pltpu.VMEM((n_chains, 1, tq), F32),
            pltpu.VMEM((3, tq, tq), F32),
            pltpu.VMEM((n_chains, 2, tq, tq), F32),
            pltpu.VMEM((n_chains, 2, tq, tq), F32),
            pltpu.VMEM((n_chains, 2, tq + SUM_ROWS, tq), F32),
        ],
    )
    return pl.pallas_call(
        functools.partial(_sb_kernel, tq=tq, n_items=n_items),
        grid_spec=grid_spec,
        out_shape=jax.ShapeDtypeStruct((b, s, SB_WIDTH), MXU_DTYPE),
        compiler_params=_params("parallel", "parallel"),
        name="sb_attention",
    )(jnp.asarray(table), proj, proj, v_t)


def _ret_kernel(lg_ref, q_ref, k_ref, v_ref, g_ref, gn_ref, o_ref, state_ref, *, tb):
    h = pl.program_id(1)
    blk = pl.program_id(2)

    @pl.when(blk == 0)
    def _():
        state_ref[...] = jnp.zeros_like(state_ref)

    lg = lg_ref[h]
    scale = RET_HEAD_DIM ** -0.5
    q = q_ref[0].astype(F32)
    k = k_ref[0].astype(F32)
    v = v_ref[0]
    row = lax.broadcasted_iota(I32, (tb, tb), 0)
    col = lax.broadcasted_iota(I32, (tb, tb), 1)
    dist = jnp.abs(row - col).astype(F32)
    decay = jnp.where(col // CHUNK <= row // CHUNK, jnp.exp(lg * dist) * scale, 0.0)
    pos = lax.broadcasted_iota(I32, (tb, 1), 0).astype(F32)
    scores = _dot_nt(q.astype(MXU_DTYPE), k.astype(MXU_DTYPE)) * decay
    intra = _dot(scores.astype(MXU_DTYPE), v)
    state = state_ref[...]
    q_in = (q * jnp.exp(lg * (pos + 1.0))).astype(MXU_DTYPE)
    inter = _dot(q_in, state.astype(MXU_DTYPE))
    k_out = (k * (jnp.exp(lg * (tb - 1.0 - pos)) * scale)).astype(MXU_DTYPE)
    block_decay = jnp.exp(lg * jnp.full((1, RET_HEAD_DIM), float(tb), F32))
    state_ref[...] = block_decay * state + _dot_tn(k_out, v)
    o = intra + inter
    mu = jnp.mean(o, axis=-1, keepdims=True)
    oc = o - mu
    var = jnp.mean(oc * oc, axis=-1, keepdims=True)
    o = oc * lax.rsqrt(var + LN_EPS) * gn_ref[0] * _silu(g_ref[0].astype(F32))
    o_ref[0] = o.astype(o_ref.dtype)


def _retention(proj, gn_gain, first_col_block):
    b, s, _ = proj.shape
    tb = min(TOKEN_TILE, s)
    log_gamma = jnp.log1p(-jnp.exp2(-5.0 - jnp.arange(RET_HEADS, dtype=F32)))

    def col(which):
        return lambda i, h, j, lg: (i, j, first_col_block + which * RET_HEADS + h)

    grid_spec = pltpu.PrefetchScalarGridSpec(
        num_scalar_prefetch=1,
        grid=(b, RET_HEADS, s // tb),
        in_specs=[pl.BlockSpec((1, tb, LANES), col(w)) for w in range(4)]
        + [pl.BlockSpec((1, 1, LANES), lambda i, h, j, lg: (h, 0, 0))],
        out_specs=pl.BlockSpec((1, tb, LANES), lambda i, h, j, lg: (i, j, h)),
        scratch_shapes=[pltpu.VMEM((RET_HEAD_DIM, RET_HEAD_DIM), F32)],
    )
    return pl.pallas_call(
        functools.partial(_ret_kernel, tb=tb),
        grid_spec=grid_spec,
        out_shape=jax.ShapeDtypeStruct((b, s, RET_WIDTH), MXU_DTYPE),
        compiler_params=_params("parallel", "parallel", "arbitrary"),
        name="retention",
    )(log_gamma, proj, proj, proj, proj, gn_gain.astype(F32).reshape(RET_HEADS, 1, RET_HEAD_DIM))


DIFF_STAGES = 3
DIFF_HEADS_PER_STEP = 2
DIFF_STREAMS = 1


def _diff_kernel(slope_ref, tab_ref, q_ref, k_ref, vt_ref, lam_ref, gain_ref, o_ref, qs_s, m_ref, l_ref, acc_ref,
                 bias_ref, z_s, p_s, a_s, lfin_s, *, tq, n_items, lambda_init):
    hp = pl.program_id(1)
    tk = tq
    nh = DIFF_HEADS_PER_STEP
    dv = LANES
    heads = range(nh)
    streams = range(DIFF_STREAMS)
    lane = lax.broadcasted_iota(I32, (1, LANES), 1)
    scale2 = DIFF_HEAD_DIM ** -0.5 * LOG2E
    slope2 = [slope_ref[hp * nh + hh] * LOG2E for hh in heads]

    def rows_of(block, size):
        return pl.ds(pl.multiple_of(block * size, size), size)

    def prepare_queries(blk, _):
        for hh in heads:
            q2 = q_ref[0, rows_of(blk, tq), hh * LANES:(hh + 1) * LANES]
            zero = jnp.zeros_like(q2)
            stacked = jnp.concatenate([jnp.where(lane < DIFF_HEAD_DIM, q2, zero),
                                       jnp.where(lane >= DIFF_HEAD_DIM, q2, zero)], axis=0)
            qs_s[hh, rows_of(blk, 2 * tq), :] = (stacked.astype(F32) * scale2).astype(MXU_DTYPE)
        return 0

    lax.fori_loop(0, q_ref.shape[1] // tq, prepare_queries, 0)
    key = lax.broadcasted_iota(I32, (tk, tq), 0)
    qry = lax.broadcasted_iota(I32, (tk, tq), 1)
    visible = key // CHUNK <= qry // CHUNK
    for hh in heads:
        plain = slope2[hh] * key.astype(F32)
        diag = jnp.where(visible, slope2[hh] * (qry - jnp.abs(qry - key)).astype(F32), MASKED)
        bias_ref[hh, ITEM_PLAIN] = jnp.concatenate([plain, plain], axis=1)
        bias_ref[hh, ITEM_DIAGONAL] = jnp.concatenate([diag, diag], axis=1)
    m_ref[...] = jnp.zeros_like(m_ref)
    l_ref[...] = jnp.zeros_like(l_ref)
    acc_ref[...] = jnp.zeros_like(acc_ref)
    z_s[...] = jnp.zeros_like(z_s)
    p_s[...] = jnp.zeros_like(p_s)
    a_s[...] = jnp.ones_like(a_s)
    lfin_s[...] = jnp.ones_like(lfin_s)

    def trip(it, parity):
        col_a, col_a1, col_c = it + 2, it + 1, it
        for s in streams:
            krows = rows_of(tab_ref[4 * s + 1, col_a], tk)
            qrows = rows_of(tab_ref[4 * s, col_a], 2 * tq)
            for hh in heads:
                z_s[nh * s + hh, parity] = _dot_nt(k_ref[0, krows, hh * LANES:(hh + 1) * LANES], qs_s[hh, qrows, :])
        lfin = {}
        for s in streams:
            vrows = rows_of(tab_ref[4 * s + 1, col_c], tk)
            for hh in heads:
                c = nh * s + hh
                acc_ref[c] = a_s[c] * acc_ref[c] + _dot(vt_ref[0, hh * dv:(hh + 1) * dv, vrows], p_s[c])
                lfin[c] = lfin_s[c]
        for s in streams:
            kind = tab_ref[4 * s + 2, col_a1]
            first = (tab_ref[4 * s + 3, col_a1] & FLAG_FIRST) != 0
            offset = ((tab_ref[4 * s + 1, col_a1] - tab_ref[4 * s, col_a1]) * tq).astype(F32)
            for hh in heads:
                c = nh * s + hh
                shift = jnp.where(kind == ITEM_PLAIN, slope2[hh] * offset,
                                  jnp.where(kind == ITEM_DIAGONAL, 0.0, MASKED))
                z = z_s[c, 1 - parity] + bias_ref[hh, jnp.minimum(kind, ITEM_DIAGONAL)]
                m_old = jnp.where(first, MASKED, m_ref[c])
                m_new = jnp.maximum(m_old, jnp.max(z, axis=0, keepdims=True) + shift)
                p = jnp.exp2(z - (m_new - shift))
                a = jnp.exp2(m_old - m_new)
                l_new = a * l_ref[c] + jnp.sum(p, axis=0, keepdims=True)
                p_s[c] = p.astype(MXU_DTYPE)
                a_s[c] = a
                lfin_s[c] = l_new
                l_ref[c] = l_new
                m_ref[c] = m_new

        for s in streams:
            @pl.when((tab_ref[4 * s + 3, col_c] & FLAG_LAST) != 0)
            def _(s=s):
                lam_v = lam_ref[...]
                lam = (jnp.exp(jnp.sum(lam_v[0:1] * lam_v[1:2], axis=-1, keepdims=True))
                       - jnp.exp(jnp.sum(lam_v[2:3] * lam_v[3:4], axis=-1, keepdims=True)) + lambda_init)
                orows = rows_of(tab_ref[4 * s, col_c], tq)
                for hh in heads:
                    c = nh * s + hh
                    o = acc_ref[c] * (1.0 / lfin[c])
                    o = o[:, :tq] - lam * o[:, tq:]
                    o = o * lax.rsqrt(jnp.mean(o * o, axis=0, keepdims=True) + LN_EPS)
                    o_ref[0, orows, hh * dv:(hh + 1) * dv] = (o * gain_ref[...] * (1.0 - lambda_init)).T.astype(
                        o_ref.dtype)

    def trip_pair(i, _):
        trip(2 * i, 0)
        trip(2 * i + 1, 1)
        return 0

    lax.fori_loop(0, pl.cdiv(n_items + DIFF_STAGES - 1, 2), trip_pair, 0)


def _diff_attention(proj, v_t, lam_rows, subln_gain, lambda_init):
    b, s, _ = proj.shape
    tq = min(ATTENTION_TILE, s)
    dv = subln_gain.shape[-1]
    nh = DIFF_HEADS_PER_STEP
    slopes = jnp.exp2(-8.0 / DIFF_HEADS * (jnp.arange(DIFF_HEADS, dtype=F32) + 1.0))
    kb = DIFF_QK // (nh * LANES)
    table, n_items = _triangle_items(s // tq, DIFF_STAGES - 1, diagonal_first=False, n_streams=DIFF_STREAMS)
    n_chains = nh * DIFF_STREAMS
    grid_spec = pltpu.PrefetchScalarGridSpec(
        num_scalar_prefetch=2,
        grid=(b, DIFF_HEADS // nh),
        in_specs=[
            pl.BlockSpec((1, s, nh * LANES), lambda i, h, sl, t: (i, 0, h)),
            pl.BlockSpec((1, s, nh * LANES), lambda i, h, sl, t: (i, 0, kb + h)),
            pl.BlockSpec((1, nh * dv, s), lambda i, h, sl, t: (i, h, 0)),
            pl.BlockSpec((4, DIFF_HEAD_DIM), lambda i, h, sl, t: (0, 0)),
            pl.BlockSpec((dv, 1), lambda i, h, sl, t: (0, 0)),
        ],
        out_specs=pl.BlockSpec((1, s, nh * dv), lambda i, h, sl, t: (i, 0, h)),
        scratch_shapes=[
            pltpu.VMEM((nh, 2 * s, LANES), MXU_DTYPE),
            pltpu.VMEM((n_chains, 1, 2 * tq), F32),
            pltpu.VMEM((n_chains, 1, 2 * tq), F32),
            pltpu.VMEM((n_chains, dv, 2 * tq), F32),
            pltpu.VMEM((nh, 2, tq, 2 * tq), F32),
            pltpu.VMEM((n_chains, 2, tq, 2 * tq), F32),
            pltpu.VMEM((n_chains, tq, 2 * tq), MXU_DTYPE),
            pltpu.VMEM((n_chains, 1, 2 * tq), F32),
            pltpu.VMEM((n_chains, 1, 2 * tq), F32),
        ],
    )
    return pl.pallas_call(
        functools.partial(_diff_kernel, tq=tq, n_items=n_items, lambda_init=lambda_init),
        grid_spec=grid_spec,
        out_shape=jax.ShapeDtypeStruct((b, s, DIFF_HEADS * dv), MXU_DTYPE),
        compiler_params=_params("parallel", "parallel"),
        name="diff_attention",
    )(slopes, jnp.asarray(table), proj, proj, v_t, lam_rows.astype(F32), subln_gain.astype(F32).reshape(dv, 1))


def _layer_norm(y, g, b):
    mu = jnp.mean(y, axis=-1, keepdims=True)
    yc = y - mu
    var = jnp.mean(yc * yc, axis=-1, keepdims=True)
    return yc * lax.rsqrt(var + LN_EPS) * g + b


def _outproj_kernel(*refs, n_in, alpha):
    a_refs, w_refs = refs[:n_in], refs[n_in:2 * n_in]
    x_ref, gate_ref, lng_ref, lnb_ref, sc_ref, sh_ref, wr_ref, br_ref, xo_ref, u_ref, lg_ref = refs[2 * n_in:]
    mix = _dot(a_refs[0][0], w_refs[0][...])
    for a_ref, w_ref in zip(a_refs[1:], w_refs[1:]):
        mix += _dot(a_ref[0], w_ref[...])
    xn = _layer_norm(alpha * x_ref[0] + gate_ref[0] * mix, lng_ref[...], lnb_ref[...])
    xo_ref[0] = xn
    u = xn * sc_ref[0] + sh_ref[0]
    u_ref[0] = u
    lg_ref[...] = lax.dot_general(wr_ref[...], u, (((1,), (1,)), ((), ())), preferred_element_type=F32,
                                  precision=HIGHEST) + br_ref[...]


def _outproj(parts, weights, x, gate1p, ln_g, ln_b, scale1p, shift, w_router_t, b_router, alpha):
    b, s, d = x.shape
    tm = min(TOKEN_TILE, s)
    n_in = len(parts)
    vec = pl.BlockSpec((1, 1, d), lambda i, j: (i, 0, 0))
    row = pl.BlockSpec((1, d), lambda i, j: (0, 0))
    in_specs = [pl.BlockSpec((1, tm, p.shape[-1]), lambda i, j: (i, j, 0)) for p in parts]
    in_specs += [pl.BlockSpec(w.shape, lambda i, j: (0, 0)) for w in weights]
    in_specs += [pl.BlockSpec((1, tm, d), lambda i, j: (i, j, 0)), vec, row, row, vec, vec,
                 pl.BlockSpec((ROUTER_ROWS, d), lambda i, j: (0, 0)),
                 pl.BlockSpec((ROUTER_ROWS, 1), lambda i, j: (0, 0))]
    nb = s // tm
    return pl.pallas_call(
        functools.partial(_outproj_kernel, n_in=n_in, alpha=alpha),
        grid=(b, nb),
        in_specs=in_specs,
        out_specs=[pl.BlockSpec((1, tm, d), lambda i, j: (i, j, 0)),
                   pl.BlockSpec((1, tm, d), lambda i, j: (i, j, 0)),
                   pl.BlockSpec((ROUTER_ROWS, tm), lambda i, j: (0, i * nb + j))],
        out_shape=[jax.ShapeDtypeStruct((b, s, d), F32), jax.ShapeDtypeStruct((b, s, d), F32),
                   jax.ShapeDtypeStruct((ROUTER_ROWS, b * s), F32)],
        compiler_params=_params("parallel", "parallel"),
        name="out_proj_norm",
    )(*parts, *[w.astype(MXU_DTYPE) for w in weights], x, gate1p, ln_g.reshape(1, d), ln_b.reshape(1, d),
      scale1p, shift, w_router_t, b_router)


def _route_kernel(lg_ref, ids_ref, gates_ref, cnt_ref, run_ref, *, tm):
    @pl.when(pl.program_id(0) == 0)
    def _():
        run_ref[...] = jnp.zeros_like(run_ref)

    lg = lg_ref[...]
    g0 = N_EXPERTS
    g_max = lg[g0:g0 + 1]
    grp = jnp.zeros((1, tm), I32)
    for i in range(1, N_GROUPS):
        gi = lg[g0 + i:g0 + i + 1]
        better = gi > g_max
        grp = jnp.where(better, i, grp)
        g_max = jnp.where(better, gi, g_max)
    den = jnp.exp(lg[g0:g0 + 1] - g_max)
    for i in range(1, N_GROUPS):
        den += jnp.exp(lg[g0 + i:g0 + i + 1] - g_max)
    p_grp = 1.0 / den

    cand = lg[0:EXPERTS_PER_GROUP]
    for g in range(1, N_GROUPS):
        cand = jnp.where(grp == g, lg[g * EXPERTS_PER_GROUP:(g + 1) * EXPERTS_PER_GROUP], cand)
    ridx = lax.broadcasted_iota(I32, (EXPERTS_PER_GROUP, tm), 0).astype(F32)
    none = float(EXPERTS_PER_GROUP)
    v1 = jnp.max(cand, axis=0, keepdims=True)
    i1 = jnp.min(jnp.where(cand == v1, ridx, none), axis=0, keepdims=True)
    rest = jnp.where(ridx == i1, -jnp.inf, cand)
    v2 = jnp.max(rest, axis=0, keepdims=True)
    i2 = jnp.min(jnp.where(rest == v2, ridx, none), axis=0, keepdims=True)
    e21 = jnp.exp(v2 - v1)
    gate1 = p_grp / (1.0 + e21)
    gate2 = p_grp * e21 / (1.0 + e21)
    ex1 = grp * EXPERTS_PER_GROUP + i1.astype(I32)
    ex2 = grp * EXPERTS_PER_GROUP + i2.astype(I32)

    eidx = lax.broadcasted_iota(I32, (N_EXPERTS, tm), 0)
    oh1 = jnp.where(eidx == ex1, 1.0, 0.0)
    oh2 = jnp.where(eidx == ex2, 1.0, 0.0)
    oh = (oh1 + oh2).astype(MXU_DTYPE)
    earlier = jnp.where(lax.broadcasted_iota(I32, (tm, tm), 0) < lax.broadcasted_iota(I32, (tm, tm), 1),
                        1.0, 0.0).astype(MXU_DTYPE)
    run = run_ref[...]
    before = _dot(oh, earlier) + jnp.concatenate([run] * (tm // LANES), axis=1)
    rank1 = jnp.sum(oh1 * before, axis=0, keepdims=True).astype(I32)
    rank2 = jnp.sum(oh2 * before, axis=0, keepdims=True).astype(I32)
    run = run + _dot(oh, jnp.ones((tm, LANES), MXU_DTYPE))
    run_ref[...] = run
    cnt_ref[...] = run
    ids_ref[...] = jnp.concatenate([ex1, ex2, rank1, rank2, jnp.zeros((4, tm), I32)], axis=0)
    gates_ref[...] = jnp.concatenate([gate1, gate2, jnp.zeros((6, tm), F32)], axis=0)


def _route(logits_t):
    t = logits_t.shape[1]
    tm = min(TOKEN_TILE, t)
    return pl.pallas_call(
        functools.partial(_route_kernel, tm=tm),
        grid=(t // tm,),
        in_specs=[pl.BlockSpec((ROUTER_ROWS, tm), lambda i: (0, i))],
        out_specs=[pl.BlockSpec((8, tm), lambda i: (0, i)), pl.BlockSpec((8, tm), lambda i: (0, i)),
                   pl.BlockSpec((N_EXPERTS, LANES), lambda i: (0, 0))],
        out_shape=[jax.ShapeDtypeStruct((8, t), I32), jax.ShapeDtypeStruct((8, t), F32),
                   jax.ShapeDtypeStruct((N_EXPERTS, LANES), F32)],
        scratch_shapes=[pltpu.VMEM((N_EXPERTS, LANES), F32)],
        compiler_params=_params("arbitrary"),
        name="route",
    )(logits_t)


def _row_copy(idx_ref, src_hbm, dst, sem, r):
    return pltpu.make_async_copy(src_hbm.at[pl.ds(idx_ref[0, 0, r], 1), :], dst.at[pl.ds(r, 1), :], sem)


def _gather_rows(idx_ref, src_hbm, dst, sem, n):
    def body(r, _):
        _row_copy(idx_ref, src_hbm, dst, sem, r).start()
        return 0

    lax.fori_loop(0, n, body, 0)


def _gather_rows_inline(idx_ref, src_hbm, dst, sem, n):
    for r in range(n):
        _row_copy(idx_ref, src_hbm, dst, sem, r).start()


def _wait_rows(src_hbm, dst, sem, n):
    pltpu.make_async_copy(src_hbm.at[pl.ds(0, n), :], dst, sem).wait()


def _expert_kernel(be_ref, idx_a_ref, idx_b_ref, idx_next_ref, x_hbm, w1a_ref, w3a_ref, w2a_ref, w1b_ref, w3b_ref,
                   w2b_ref, y_ref, xbuf_a, xbuf_b, sem, *, tm):
    i = pl.program_id(0)

    @pl.when(i == 0)
    def _():
        _gather_rows(idx_a_ref, x_hbm, xbuf_a, sem.at[0], tm)

    def block(cur, cur_sem, w1_ref, w3_ref, w2_ref, out_rows, nxt_idx_ref, nxt, nxt_sem):
        _gather_rows_inline(nxt_idx_ref, x_hbm, nxt, nxt_sem, tm)
        _wait_rows(x_hbm, cur, cur_sem, tm)
        xb = cur[...].astype(MXU_DTYPE)
        hidden = _silu(_dot(xb, w1_ref[0, 0].astype(MXU_DTYPE))) * _dot(xb, w3_ref[0, 0].astype(MXU_DTYPE))
        y_ref[out_rows, :] = _dot(hidden.astype(MXU_DTYPE), w2_ref[0, 0].astype(MXU_DTYPE))

    block(xbuf_a, sem.at[0], w1a_ref, w3a_ref, w2a_ref, slice(0, tm), idx_b_ref, xbuf_b, sem.at[1])
    block(xbuf_b, sem.at[1], w1b_ref, w3b_ref, w2b_ref, slice(tm, 2 * tm), idx_next_ref, xbuf_a, sem.at[0])

    @pl.when(i == pl.num_programs(0) - 1)
    def _():
        _wait_rows(x_hbm, xbuf_a, sem.at[0], tm)


def _experts(u_flat, slot_tok, blk_expert, w1, w3, w2, layer, tm):
    t, d = u_flat.shape
    n_blk = slot_tok.shape[0] // tm
    assert n_blk % 2 == 0
    ff = w1.shape[-1]
    idx3 = slot_tok.reshape(n_blk, 1, tm)
    last = n_blk - 1

    def idx_spec(block_of):
        return pl.BlockSpec((1, 1, tm), lambda i, be: (block_of(i), 0, 0), memory_space=pltpu.SMEM)

    def weight_specs(which):
        return [pl.BlockSpec((1, 1, d, ff), lambda i, be: (layer, be[2 * i + which], 0, 0)),
                pl.BlockSpec((1, 1, d, ff), lambda i, be: (layer, be[2 * i + which], 0, 0)),
                pl.BlockSpec((1, 1, ff, d), lambda i, be: (layer, be[2 * i + which], 0, 0))]

    grid_spec = pltpu.PrefetchScalarGridSpec(
        num_scalar_prefetch=1,
        grid=(n_blk // 2,),
        in_specs=[idx_spec(lambda i: 2 * i), idx_spec(lambda i: 2 * i + 1),
                  idx_spec(lambda i: jnp.minimum(2 * i + 2, last)), pl.BlockSpec(memory_space=pl.ANY)]
        + weight_specs(0) + weight_specs(1),
        out_specs=pl.BlockSpec((2 * tm, d), lambda i, be: (i, 0)),
        scratch_shapes=[pltpu.VMEM((tm, d), F32), pltpu.VMEM((tm, d), F32), pltpu.SemaphoreType.DMA((2,))],
    )
    return pl.pallas_call(
        functools.partial(_expert_kernel, tm=tm),
        grid_spec=grid_spec,
        out_shape=jax.ShapeDtypeStruct((n_blk * tm, d), F32),
        compiler_params=_params("arbitrary"),
        name="experts",
    )(blk_expert, idx3, idx3, idx3, u_flat, w1, w3, w2, w1, w3, w2)


def _combine_kernel(d1_ref, d2_ref, d1n_ref, d2n_ref, y_hbm, gates_ref, x_ref, gate_ref, lng_ref, lnb_ref,
                    o_ref, ybuf, sem, *, tm, alpha):
    i = pl.program_id(0)
    n = pl.num_programs(0)
    slot = i % 2

    def wait(s):
        _wait_rows(y_hbm, ybuf.at[s, 0], sem.at[s, 0], tm)
        _wait_rows(y_hbm, ybuf.at[s, 1], sem.at[s, 1], tm)

    @pl.when(i == 0)
    def _():
        _gather_rows(d1_ref, y_hbm, ybuf.at[0, 0], sem.at[0, 0], tm)
        _gather_rows(d2_ref, y_hbm, ybuf.at[0, 1], sem.at[0, 1], tm)

    _gather_rows_inline(d1n_ref, y_hbm, ybuf.at[1 - slot, 0], sem.at[1 - slot, 0], tm)
    _gather_rows_inline(d2n_ref, y_hbm, ybuf.at[1 - slot, 1], sem.at[1 - slot, 1], tm)
    wait(slot)
    gates = gates_ref[...]
    f = gates[:, 0:1] * ybuf[slot, 0] + gates[:, 1:2] * ybuf[slot, 1]
    o_ref[...] = _layer_norm(alpha * x_ref[...] + gate_ref[0] * f, lng_ref[...], lnb_ref[...])

    @pl.when(i == n - 1)
    def _():
        wait(1 - slot)


def _combine(y_slots, dest1, dest2, gates, x_flat, gate1p, ln_g, ln_b, seq, alpha):
    t, d = x_flat.shape
    tm = min(COMBINE_TILE, seq)
    n_blk = t // tm
    per_seq = seq // tm
    d1 = dest1.reshape(n_blk, 1, tm)
    d2 = dest2.reshape(n_blk, 1, tm)
    cur = pl.BlockSpec((1, 1, tm), lambda i: (i, 0, 0), memory_space=pltpu.SMEM)
    nxt = pl.BlockSpec((1, 1, tm), lambda i: (jnp.minimum(i + 1, n_blk - 1), 0, 0), memory_space=pltpu.SMEM)
    row = pl.BlockSpec((1, d), lambda i: (0, 0))
    return pl.pallas_call(
        functools.partial(_combine_kernel, tm=tm, alpha=alpha),
        grid=(n_blk,),
        in_specs=[cur, cur, nxt, nxt, pl.BlockSpec(memory_space=pl.ANY),
                  pl.BlockSpec((tm, TOP_K), lambda i: (i, 0)),
                  pl.BlockSpec((tm, d), lambda i: (i, 0)),
                  pl.BlockSpec((1, 1, d), lambda i: (i // per_seq, 0, 0)), row, row],
        out_specs=pl.BlockSpec((tm, d), lambda i: (i, 0)),
        out_shape=jax.ShapeDtypeStruct((t, d), F32),
        scratch_shapes=[pltpu.VMEM((2, 2, tm, d), F32), pltpu.SemaphoreType.DMA((2, 2))],
        compiler_params=_params("arbitrary"),
        name="combine_norm",
    )(d1, d2, d1, d2, y_slots, gates, x_flat, gate1p, ln_g.reshape(1, d), ln_b.reshape(1, d))


EXPERT_TILE = MXU_TILE
SLOT_STRIDE = 7919

def _moe(u, logits_t, x, gate2p, ln_g, ln_b, w1, w3, w2, layer, alpha):
    b, s, d = x.shape
    t = b * s
    ids, gates8, cnt = _route(logits_t)
    counts = cnt[:, 0].astype(I32)
    padded = (counts + EXPERT_TILE - 1) // EXPERT_TILE * EXPERT_TILE
    pad_end = jnp.cumsum(padded)
    pad_start = pad_end - padded
    def slot(e, rank):
        return pad_start[e] + (rank * SLOT_STRIDE) % padded[e]

    dest1 = slot(ids[0], ids[2])
    dest2 = slot(ids[1], ids[3])
    n_slots = t * TOP_K + N_EXPERTS * EXPERT_TILE
    n_blk = n_slots // EXPERT_TILE
    tok = jnp.arange(t, dtype=I32)
    slot_tok = jnp.zeros((n_slots,), I32).at[jnp.concatenate([dest1, dest2])].set(
        jnp.concatenate([tok, tok]), unique_indices=True, mode='promise_in_bounds')
    blk_start = jnp.arange(n_blk, dtype=I32) * EXPERT_TILE
    blk_expert = jnp.minimum(jnp.sum((pad_end[None, :] <= blk_start[:, None]).astype(I32), axis=1), N_EXPERTS - 1)
    y_slots = _experts(u.reshape(t, d), slot_tok, blk_expert.astype(I32), w1, w3, w2, layer, EXPERT_TILE)
    out = _combine(y_slots, dest1, dest2, gates8[:TOP_K].T, x.reshape(t, d), gate2p, ln_g, ln_b, s, alpha)
    return out.reshape(b, s, d)


def kernel(x, c, ln1_g, ln1_b, ln2_g, ln2_b, w_ada, b_ada, even_w_in, even_w_out, ret_gn_g, odd_w_in, odd_w_out, lambda_q1, lambda_k1, lambda_q2, lambda_k2, diff_subln_g, moe_w_group, moe_b_group, moe_w_router, moe_b_router, moe_w1, moe_w3, moe_w2):
    b, s, d = x.shape
    depth = w_ada.shape[0]
    alpha = (2.0 * depth) ** 0.25
    mod = _ada(c, w_ada, b_ada)
    for l in range(depth):
        sh1, sc1, g1, sh2, sc2, g2 = [m[:, None, :] for m in jnp.split(mod[l], 6, axis=-1)]
        i = l // 2
        if l % 2 == 0:
            w_in = even_w_in[i]
            w_main = jnp.concatenate([w_in[:, :2 * SB_WIDTH], w_in[:, 3 * SB_WIDTH:]], axis=1)
            proj, v_t = _inproj(x, 1.0 + sc1, sh1, w_main, w_in[:, 2 * SB_WIDTH:3 * SB_WIDTH])
            parts = [_sb_attention(proj, v_t), _retention(proj, ret_gn_g[i], 2 * SB_WIDTH // LANES)]
            w_out = even_w_out[i]
            weights = [w_out[:SB_WIDTH], w_out[SB_WIDTH:]]
        else:
            w_in = odd_w_in[i]
            proj, v_t = _inproj(x, 1.0 + sc1, sh1, w_in[:, :2 * DIFF_QK], w_in[:, 2 * DIFF_QK:])
            lambda_init = 0.8 - 0.6 * math.exp(-0.3 * l)
            lam_rows = jnp.stack([lambda_q1[i], lambda_k1[i], lambda_q2[i], lambda_k2[i]])
            parts = [_diff_attention(proj, v_t, lam_rows, diff_subln_g[i], lambda_init)]
            weights = [odd_w_out[i]]
        w_router_t = jnp.zeros((ROUTER_ROWS, d), F32).at[:N_EXPERTS].set(moe_w_router[l].T.astype(F32))
        w_router_t = w_router_t.at[N_EXPERTS:N_EXPERTS + N_GROUPS].set(moe_w_group[l].T.astype(F32))
        b_router = jnp.zeros((ROUTER_ROWS, 1), F32).at[:N_EXPERTS, 0].set(moe_b_router[l].astype(F32))
        b_router = b_router.at[N_EXPERTS:N_EXPERTS + N_GROUPS, 0].set(moe_b_group[l].astype(F32))
        x, u, logits_t = _outproj(parts, weights, x, 1.0 + g1, ln1_g[l], ln1_b[l], 1.0 + sc2, sh2,
                                  w_router_t, b_router, alpha)
        x = _moe(u, logits_t, x, 1.0 + g2, ln2_g[l], ln2_b[l], moe_w1, moe_w3, moe_w2, l, alpha)
    return x
```

```python
import functools
import math

import jax
import jax.numpy as jnp
import numpy as np
from jax import lax
from jax.experimental import pallas as pl
from jax.experimental.pallas import tpu as pltpu

F32 = jnp.float32
I32 = jnp.int32
MXU_DTYPE = jnp.bfloat16
HIGHEST = lax.Precision.HIGHEST
LOG2E = math.log2(math.e)

LN_EPS = 1e-5
CHUNK = 64
LANES = 128
SB_HEADS, SB_HEAD_DIM = 8, 64
RET_HEADS, RET_HEAD_DIM = 4, 128
DIFF_HEADS, DIFF_HEAD_DIM = 8, 64
SB_WIDTH = SB_HEADS * SB_HEAD_DIM
RET_WIDTH = RET_HEADS * RET_HEAD_DIM
DIFF_QK = DIFF_HEADS * 2 * DIFF_HEAD_DIM
N_GROUPS, EXPERTS_PER_GROUP = 4, 8
N_EXPERTS = N_GROUPS * EXPERTS_PER_GROUP
TOP_K = 2
ROUTER_ROWS = 40

VMEM_LIMIT = 56 * 1024 * 1024

MXU_TILE = 256
ATTENTION_TILE = MXU_TILE
TOKEN_TILE = 2 * MXU_TILE
COMBINE_TILE = MXU_TILE
PROJ_COLS = 2 * MXU_TILE
ADA_COLS = 6 * MXU_TILE
EXP2_MAX = 126.0


def _params(*sem):
    return pltpu.CompilerParams(dimension_semantics=sem, vmem_limit_bytes=VMEM_LIMIT)


def _dot(a, b):
    return jnp.dot(a, b, preferred_element_type=F32)


def _dot_nt(a, b):
    return lax.dot_general(a, b, (((1,), (1,)), ((), ())), preferred_element_type=F32)


def _dot_tn(a, b):
    return lax.dot_general(a, b, (((0,), (0,)), ((), ())), preferred_element_type=F32)


def _silu(x):
    return x * (1.0 / (1.0 + jnp.exp(-x)))


def _ada_kernel(c_ref, w_ref, b_ref, o_ref):
    o_ref[0] = jnp.dot(_silu(c_ref[...]), w_ref[0], preferred_element_type=F32, precision=HIGHEST) + b_ref[0]


def _ada(c, w_ada, b_ada):
    depth, d, n = w_ada.shape
    bp = 8
    cp = jnp.zeros((bp, d), F32).at[: c.shape[0]].set(c)
    tn = ADA_COLS
    out = pl.pallas_call(
        _ada_kernel,
        grid=(depth, n // tn),
        in_specs=[
            pl.BlockSpec((bp, d), lambda l, j: (0, 0)),
            pl.BlockSpec((1, d, tn), lambda l, j: (l, 0, j)),
            pl.BlockSpec((1, 1, tn), lambda l, j: (l, 0, j)),
        ],
        out_specs=pl.BlockSpec((1, bp, tn), lambda l, j: (l, 0, j)),
        out_shape=jax.ShapeDtypeStruct((depth, bp, n), F32),
        compiler_params=_params("parallel", "parallel"),
        name="ada_mod",
    )(cp, w_ada, b_ada.reshape(depth, 1, n))
    return out[:, : c.shape[0]]


def _inproj_kernel(x_ref, sc_ref, sh_ref, w_ref, wvt_ref, o_ref, vt_ref, *, tn):
    u = (x_ref[0] * sc_ref[0] + sh_ref[0]).astype(MXU_DTYPE)
    for j in range(o_ref.shape[2] // tn):
        o_ref[0, :, j * tn:(j + 1) * tn] = _dot(u, w_ref[:, j * tn:(j + 1) * tn]).astype(o_ref.dtype)
    for j in range(vt_ref.shape[1] // tn):
        vt_ref[0, j * tn:(j + 1) * tn, :] = _dot_nt(wvt_ref[j * tn:(j + 1) * tn, :], u).astype(vt_ref.dtype)


def _inproj(x, scale1p, shift, w, w_v):
    b, s, d = x.shape
    n, n_v = w.shape[1], w_v.shape[1]
    tm = min(TOKEN_TILE, s)
    return pl.pallas_call(
        functools.partial(_inproj_kernel, tn=PROJ_COLS),
        grid=(b, s // tm),
        in_specs=[
            pl.BlockSpec((1, tm, d), lambda i, j: (i, j, 0)),
            pl.BlockSpec((1, 1, d), lambda i, j: (i, 0, 0)),
            pl.BlockSpec((1, 1, d), lambda i, j: (i, 0, 0)),
            pl.BlockSpec((d, n), lambda i, j: (0, 0)),
            pl.BlockSpec((n_v, d), lambda i, j: (0, 0)),
        ],
        out_specs=[pl.BlockSpec((1, tm, n), lambda i, j: (i, j, 0)),
                   pl.BlockSpec((1, n_v, tm), lambda i, j: (i, 0, j))],
        out_shape=[jax.ShapeDtypeStruct((b, s, n), MXU_DTYPE), jax.ShapeDtypeStruct((b, n_v, s), MXU_DTYPE)],
        compiler_params=_params("parallel", "parallel"),
        name="in_proj",
    )(x, scale1p, shift, w.astype(MXU_DTYPE), w_v.T.astype(MXU_DTYPE))


MASKED = -float("inf")
ITEM_PLAIN, ITEM_DIAGONAL, ITEM_NULL = 0, 1, 2
FLAG_FIRST, FLAG_LAST = 1, 2


def _triangle_items(n_q, pad, diagonal_first, n_streams):
    streams = []
    for s in range(n_streams):
        items = []
        for qb in range(s, n_q, n_streams):
            order = range(qb, -1, -1) if diagonal_first else range(qb + 1)
            for n, kb in enumerate(order):
                flags = (FLAG_FIRST if n == 0 else 0) | (FLAG_LAST if n == qb else 0)
                items.append((qb, kb, ITEM_DIAGONAL if kb == qb else ITEM_PLAIN, flags))
        streams.append(items)
    n_items = max(len(items) for items in streams)
    null = (0, 0, ITEM_NULL, 0)
    rows = [np.asarray([null] * pad + items + [null] * (n_items - len(items) + pad + 1), np.int32).T
            for items in streams]
    return np.concatenate(rows, axis=0).copy(), n_items


SB_STAGES = 3
SUM_ROWS = 16
SB_STREAMS = 2


def _sb_kernel(tab_ref, q_ref, k_ref, vt_ref, o_ref, qh_s, acc_ref, carry_ref, mask_ref, y_s, yms_s, sums_s,
               *, tq, n_items):
    tk = tq
    extra = SUM_ROWS
    heads = range(2)
    streams = range(SB_STREAMS)
    lane = lax.broadcasted_iota(I32, (1, LANES), 1)
    key = lax.broadcasted_iota(I32, (tk, tq), 0)
    qry = lax.broadcasted_iota(I32, (tk, tq), 1)
    r = lax.broadcasted_iota(I32, (tk + extra, tk), 0)
    c = lax.broadcasted_iota(I32, (tk + extra, tk), 1)
    neg_later = jnp.where(r >= tk, -1.0, jnp.where(c > r, -1.0, 0.0)).astype(MXU_DTYPE)
    scale2 = SB_HEAD_DIM ** -0.5 * LOG2E

    def prepare_queries(blk, _):
        rows = pl.ds(pl.multiple_of(blk * tq, tq), tq)
        q2 = q_ref[0, rows, :]
        for h in heads:
            qh_s[h, rows, :] = (jnp.where((lane // SB_HEAD_DIM) == h, q2, jnp.zeros_like(q2)).astype(F32)
                                * scale2).astype(MXU_DTYPE)
        return 0

    lax.fori_loop(0, q_ref.shape[1] // tq, prepare_queries, 0)
    acc_ref[...] = jnp.zeros_like(acc_ref)
    carry_ref[...] = jnp.zeros_like(carry_ref)
    mask_ref[ITEM_PLAIN] = jnp.zeros((tk, tq), F32)
    mask_ref[ITEM_DIAGONAL] = jnp.where(key < qry, 0.0, MASKED)
    mask_ref[ITEM_NULL] = jnp.full((tk, tq), MASKED, F32)
    y_s[...] = jnp.zeros_like(y_s)
    yms_s[...] = jnp.full(yms_s.shape, MASKED, F32)
    sums_s[...] = jnp.zeros_like(sums_s)

    def rows_of(block, size):
        return pl.ds(pl.multiple_of(block * size, size), size)

    def trip(it, parity):
        col_x, col_y, col_c = it + 2, it + 1, it
        for s in streams:
            kj = k_ref[0, rows_of(tab_ref[4 * s + 1, col_x], tk), :]
            qrows = rows_of(tab_ref[4 * s, col_x], tq)
            for h in heads:
                y_s[2 * s + h, parity] = _dot_nt(kj, qh_s[h, qrows, :])
        for s in streams:
            mask = mask_ref[tab_ref[4 * s + 2, col_y]]
            for h in heads:
                ym = y_s[2 * s + h, 1 - parity] + mask
                sp = jnp.maximum(ym, jnp.log2(1.0 + jnp.exp2(jnp.minimum(ym, EXP2_MAX))))
                yms_s[2 * s + h, parity] = ym - sp
                sums_s[2 * s + h, parity] = _dot(neg_later, sp.astype(MXU_DTYPE))
        for s in streams:
            vtj = vt_ref[0, :, rows_of(tab_ref[4 * s + 1, col_c], tk)]
            keep = jnp.where((tab_ref[4 * s + 3, col_c] & FLAG_FIRST) != 0, 0.0, 1.0)
            for h in heads:
                sums = sums_s[2 * s + h, 1 - parity]
                carry = carry_ref[2 * s + h] * keep
                w = jnp.exp2(yms_s[2 * s + h, 1 - parity] + sums[:tk] + carry).astype(MXU_DTYPE)
                acc_ref[2 * s + h] = acc_ref[2 * s + h] * keep + _dot(vtj, w)
                carry_ref[2 * s + h] = carry + sums[tk:tk + 1]

        for s in streams:
            @pl.when((tab_ref[4 * s + 3, col_c] & FLAG_LAST) != 0)
            def _(s=s):
                sub = lax.broadcasted_iota(I32, (LANES, 1), 0)
                o_ref[0, rows_of(tab_ref[4 * s, col_c], tq), :] = jnp.where(
                    sub < SB_HEAD_DIM, acc_ref[2 * s], acc_ref[2 * s + 1]).T.astype(o_ref.dtype)

    def trip_pair(i, _):
        trip(2 * i, 0)
        trip(2 * i + 1, 1)
        return 0

    lax.fori_loop(0, pl.cdiv(n_items + SB_STAGES - 1, 2), trip_pair, 0)


def _sb_attention(proj, v_t):
    b, s, _ = proj.shape
    tq = min(ATTENTION_TILE, s)
    n_pairs = SB_WIDTH // LANES
    table, n_items = _triangle_items(s // tq, SB_STAGES - 1, diagonal_first=True, n_streams=SB_STREAMS)
    n_chains = 2 * SB_STREAMS
    grid_spec = pltpu.PrefetchScalarGridSpec(
        num_scalar_prefetch=1,
        grid=(b, n_pairs),
        in_specs=[
            pl.BlockSpec((1, s, LANES), lambda i, p, t: (i, 0, p)),
            pl.BlockSpec((1, s, LANES), lambda i, p, t: (i, 0, n_pairs + p)),
            pl.BlockSpec((1, LANES, s), lambda i, p, t: (i, p, 0)),
        ],
        out_specs=pl.BlockSpec((1, s, LANES), lambda i, p, t: (i, 0, p)),
        scratch_shapes=[
            pltpu.VMEM((2, s, LANES), MXU_DTYPE),
            pltpu.VMEM((n_chains, LANES, tq), F32),
            pltpu.VMEM((n_chains, 1, tq), F32),
            pltpu.VMEM((3, tq, tq), F32),
            pltpu.VMEM((n_chains, 2, tq, tq), F32),
            pltpu.VMEM((n_chains, 2, tq, tq), F32),
            pltpu.VMEM((n_chains, 2, tq + SUM_ROWS, tq), F32),
        ],
    )
    return pl.pallas_call(
        functools.partial(_sb_kernel, tq=tq, n_items=n_items),
        grid_spec=grid_spec,
        out_shape=jax.ShapeDtypeStruct((b, s, SB_WIDTH), MXU_DTYPE),
        compiler_params=_params("parallel", "parallel"),
        name="sb_attention",
    )(jnp.asarray(table), proj, proj, v_t)


def _ret_kernel(lg_ref, q_ref, k_ref, v_ref, g_ref, gn_ref, o_ref, state_ref, *, tb):
    h = pl.program_id(1)
    blk = pl.program_id(2)

    @pl.when(blk == 0)
    def _():
        state_ref[...] = jnp.zeros_like(state_ref)

    lg = lg_ref[h]
    scale = RET_HEAD_DIM ** -0.5
    q = q_ref[0].astype(F32)
    k = k_ref[0].astype(F32)
    v = v_ref[0]
    row = lax.broadcasted_iota(I32, (tb, tb), 0)
    col = lax.broadcasted_iota(I32, (tb, tb), 1)
    dist = jnp.abs(row - col).astype(F32)
    decay = jnp.where(col // CHUNK <= row // CHUNK, jnp.exp(lg * dist) * scale, 0.0)
    pos = lax.broadcasted_iota(I32, (tb, 1), 0).astype(F32)
    scores = _dot_nt(q.astype(MXU_DTYPE), k.astype(MXU_DTYPE)) * decay
    intra = _dot(scores.astype(MXU_DTYPE), v)
    state = state_ref[...]
    q_in = (q * jnp.exp(lg * (pos + 1.0))).astype(MXU_DTYPE)
    inter = _dot(q_in, state.astype(MXU_DTYPE))
    k_out = (k * (jnp.exp(lg * (tb - 1.0 - pos)) * scale)).astype(MXU_DTYPE)
    block_decay = jnp.exp(lg * jnp.full((1, RET_HEAD_DIM), float(tb), F32))
    state_ref[...] = block_decay * state + _dot_tn(k_out, v)
    o = intra + inter
    mu = jnp.mean(o, axis=-1, keepdims=True)
    oc = o - mu
    var = jnp.mean(oc * oc, axis=-1, keepdims=True)
    o = oc * lax.rsqrt(var + LN_EPS) * gn_ref[0] * _silu(g_ref[0].astype(F32))
    o_ref[0] = o.astype(o_ref.dtype)


def _retention(proj, gn_gain, first_col_block):
    b, s, _ = proj.shape
    tb = min(TOKEN_TILE, s)
    log_gamma = jnp.log1p(-jnp.exp2(-5.0 - jnp.arange(RET_HEADS, dtype=F32)))

    def col(which):
        return lambda i, h, j, lg: (i, j, first_col_block + which * RET_HEADS + h)

    grid_spec = pltpu.PrefetchScalarGridSpec(
        num_scalar_prefetch=1,
        grid=(b, RET_HEADS, s // tb),
        in_specs=[pl.BlockSpec((1, tb, LANES), col(w)) for w in range(4)]
        + [pl.BlockSpec((1, 1, LANES), lambda i, h, j, lg: (h, 0, 0))],
        out_specs=pl.BlockSpec((1, tb, LANES), lambda i, h, j, lg: (i, j, h)),
        scratch_shapes=[pltpu.VMEM((RET_HEAD_DIM, RET_HEAD_DIM), F32)],
    )
    return pl.pallas_call(
        functools.partial(_ret_kernel, tb=tb),
        grid_spec=grid_spec,
        out_shape=jax.ShapeDtypeStruct((b, s, RET_WIDTH), MXU_DTYPE),
        compiler_params=_params("parallel", "parallel", "arbitrary"),
        name="retention",
    )(log_gamma, proj, proj, proj, proj, gn_gain.astype(F32).reshape(RET_HEADS, 1, RET_HEAD_DIM))


DIFF_STAGES = 3
DIFF_HEADS_PER_STEP = 2
DIFF_STREAMS = 1


def _diff_kernel(slope_ref, tab_ref, q_ref, k_ref, vt_ref, lam_ref, gain_ref, o_ref, qs_s, m_ref, l_ref, acc_ref,
                 bias_ref, z_s, p_s, a_s, lfin_s, *, tq, n_items, lambda_init):
    hp = pl.program_id(1)
    tk = tq
    nh = DIFF_HEADS_PER_STEP
    dv = LANES
    heads = range(nh)
    streams = range(DIFF_STREAMS)
    lane = lax.broadcasted_iota(I32, (1, LANES), 1)
    scale2 = DIFF_HEAD_DIM ** -0.5 * LOG2E
    slope2 = [slope_ref[hp * nh + hh] * LOG2E for hh in heads]

    def rows_of(block, size):
        return pl.ds(pl.multiple_of(block * size, size), size)

    def prepare_queries(blk, _):
        for hh in heads:
            q2 = q_ref[0, rows_of(blk, tq), hh * LANES:(hh + 1) * LANES]
            zero = jnp.zeros_like(q2)
            stacked = jnp.concatenate([jnp.where(lane < DIFF_HEAD_DIM, q2, zero),
                                       jnp.where(lane >= DIFF_HEAD_DIM, q2, zero)], axis=0)
            qs_s[hh, rows_of(blk, 2 * tq), :] = (stacked.astype(F32) * scale2).astype(MXU_DTYPE)
        return 0

    lax.fori_loop(0, q_ref.shape[1] // tq, prepare_queries, 0)
    key = lax.broadcasted_iota(I32, (tk, tq), 0)
    qry = lax.broadcasted_iota(I32, (tk, tq), 1)
    visible = key // CHUNK <= qry // CHUNK
    for hh in heads:
        plain = slope2[hh] * key.astype(F32)
        diag = jnp.where(visible, slope2[hh] * (qry - jnp.abs(qry - key)).astype(F32), MASKED)
        bias_ref[hh, ITEM_PLAIN] = jnp.concatenate([plain, plain], axis=1)
        bias_ref[hh, ITEM_DIAGONAL] = jnp.concatenate([diag, diag], axis=1)
    m_ref[...] = jnp.zeros_like(m_ref)
    l_ref[...] = jnp.zeros_like(l_ref)
    acc_ref[...] = jnp.zeros_like(acc_ref)
    z_s[...] = jnp.zeros_like(z_s)
    p_s[...] = jnp.zeros_like(p_s)
    a_s[...] = jnp.ones_like(a_s)
    lfin_s[...] = jnp.ones_like(lfin_s)

    def trip(it, parity):
        col_a, col_a1, col_c = it + 2, it + 1, it
        for s in streams:
            krows = rows_of(tab_ref[4 * s + 1, col_a], tk)
            qrows = rows_of(tab_ref[4 * s, col_a], 2 * tq)
            for hh in heads:
                z_s[nh * s + hh, parity] = _dot_nt(k_ref[0, krows, hh * LANES:(hh + 1) * LANES], qs_s[hh, qrows, :])
        lfin = {}
        for s in streams:
            vrows = rows_of(tab_ref[4 * s + 1, col_c], tk)
            for hh in heads:
                c = nh * s + hh
                acc_ref[c] = a_s[c] * acc_ref[c] + _dot(vt_ref[0, hh * dv:(hh + 1) * dv, vrows], p_s[c])
                lfin[c] = lfin_s[c]
        for s in streams:
            kind = tab_ref[4 * s + 2, col_a1]
            first = (tab_ref[4 * s + 3, col_a1] & FLAG_FIRST) != 0
            offset = ((tab_ref[4 * s + 1, col_a1] - tab_ref[4 * s, col_a1]) * tq).astype(F32)
            for hh in heads:
                c = nh * s + hh
                shift = jnp.where(kind == ITEM_PLAIN, slope2[hh] * offset,
                                  jnp.where(kind == ITEM_DIAGONAL, 0.0, MASKED))
                z = z_s[c, 1 - parity] + bias_ref[hh, jnp.minimum(kind, ITEM_DIAGONAL)]
                m_old = jnp.where(first, MASKED, m_ref[c])
                m_new = jnp.maximum(m_old, jnp.max(z, axis=0, keepdims=True) + shift)
                p = jnp.exp2(z - (m_new - shift))
                a = jnp.exp2(m_old - m_new)
                l_new = a * l_ref[c] + jnp.sum(p, axis=0, keepdims=True)
                p_s[c] = p.astype(MXU_DTYPE)
                a_s[c] = a
                lfin_s[c] = l_new
                l_ref[c] = l_new
                m_ref[c] = m_new

        for s in streams:
            @pl.when((tab_ref[4 * s + 3, col_c] & FLAG_LAST) != 0)
            def _(s=s):
                lam_v = lam_ref[...]
                lam = (jnp.exp(jnp.sum(lam_v[0:1] * lam_v[1:2], axis=-1, keepdims=True))
                       - jnp.exp(jnp.sum(lam_v[2:3] * lam_v[3:4], axis=-1, keepdims=True)) + lambda_init)
                orows = rows_of(tab_ref[4 * s, col_c], tq)
                for hh in heads:
                    c = nh * s + hh
                    o = acc_ref[c] * (1.0 / lfin[c])
                    o = o[:, :tq] - lam * o[:, tq:]
                    o = o * lax.rsqrt(jnp.mean(o * o, axis=0, keepdims=True) + LN_EPS)
                    o_ref[0, orows, hh * dv:(hh + 1) * dv] = (o * gain_ref[...] * (1.0 - lambda_init)).T.astype(
                        o_ref.dtype)

    def trip_pair(i, _):
        trip(2 * i, 0)
        trip(2 * i + 1, 1)
        return 0

    lax.fori_loop(0, pl.cdiv(n_items + DIFF_STAGES - 1, 2), trip_pair, 0)


def _diff_attention(proj, v_t, lam_rows, subln_gain, lambda_init):
    b, s, _ = proj.shape
    tq = min(ATTENTION_TILE, s)
    dv = subln_gain.shape[-1]
    nh = DIFF_HEADS_PER_STEP
    slopes = jnp.exp2(-8.0 / DIFF_HEADS * (jnp.arange(DIFF_HEADS, dtype=F32) + 1.0))
    kb = DIFF_QK // (nh * LANES)
    table, n_items = _triangle_items(s // tq, DIFF_STAGES - 1, diagonal_first=False, n_streams=DIFF_STREAMS)
    n_chains = nh * DIFF_STREAMS
    grid_spec = pltpu.PrefetchScalarGridSpec(
        num_scalar_prefetch=2,
        grid=(b, DIFF_HEADS // nh),
        in_specs=[
            pl.BlockSpec((1, s, nh * LANES), lambda i, h, sl, t: (i, 0, h)),
            pl.BlockSpec((1, s, nh * LANES), lambda i, h, sl, t: (i, 0, kb + h)),
            pl.BlockSpec((1, nh * dv, s), lambda i, h, sl, t: (i, h, 0)),
            pl.BlockSpec((4, DIFF_HEAD_DIM), lambda i, h, sl, t: (0, 0)),
            pl.BlockSpec((dv, 1), lambda i, h, sl, t: (0, 0)),
        ],
        out_specs=pl.BlockSpec((1, s, nh * dv), lambda i, h, sl, t: (i, 0, h)),
        scratch_shapes=[
            pltpu.VMEM((nh, 2 * s, LANES), MXU_DTYPE),
            pltpu.VMEM((n_chains, 1, 2 * tq), F32),
            pltpu.VMEM((n_chains, 1, 2 * tq), F32),
            pltpu.VMEM((n_chains, dv, 2 * tq), F32),
            pltpu.VMEM((nh, 2, tq, 2 * tq), F32),
            pltpu.VMEM((n_chains, 2, tq, 2 * tq), F32),
            pltpu.VMEM((n_chains, tq, 2 * tq), MXU_DTYPE),
            pltpu.VMEM((n_chains, 1, 2 * tq), F32),
            pltpu.VMEM((n_chains, 1, 2 * tq), F32),
        ],
    )
    return pl.pallas_call(
        functools.partial(_diff_kernel, tq=tq, n_items=n_items, lambda_init=lambda_init),
        grid_spec=grid_spec,
        out_shape=jax.ShapeDtypeStruct((b, s, DIFF_HEADS * dv), MXU_DTYPE),
        compiler_params=_params("parallel", "parallel"),
        name="diff_attention",
    )(slopes, jnp.asarray(table), proj, proj, v_t, lam_rows.astype(F32), subln_gain.astype(F32).reshape(dv, 1))


def _layer_norm(y, g, b):
    mu = jnp.mean(y, axis=-1, keepdims=True)
    yc = y - mu
    var = jnp.mean(yc * yc, axis=-1, keepdims=True)
    return yc * lax.rsqrt(var + LN_EPS) * g + b


def _outproj_kernel(*refs, n_in, alpha):
    a_refs, w_refs = refs[:n_in], refs[n_in:2 * n_in]
    x_ref, gate_ref, lng_ref, lnb_ref, sc_ref, sh_ref, wr_ref, br_ref, xo_ref, u_ref, lg_ref = refs[2 * n_in:]
    mix = _dot(a_refs[0][0], w_refs[0][...])
    for a_ref, w_ref in zip(a_refs[1:], w_refs[1:]):
        mix += _dot(a_ref[0], w_ref[...])
    xn = _layer_norm(alpha * x_ref[0] + gate_ref[0] * mix, lng_ref[...], lnb_ref[...])
    xo_ref[0] = xn
    u = xn * sc_ref[0] + sh_ref[0]
    u_ref[0] = u
    lg_ref[...] = lax.dot_general(wr_ref[...], u, (((1,), (1,)), ((), ())), preferred_element_type=F32,
                                  precision=HIGHEST) + br_ref[...]


def _outproj(parts, weights, x, gate1p, ln_g, ln_b, scale1p, shift, w_router_t, b_router, alpha):
    b, s, d = x.shape
    tm = min(TOKEN_TILE, s)
    n_in = len(parts)
    vec = pl.BlockSpec((1, 1, d), lambda i, j: (i, 0, 0))
    row = pl.BlockSpec((1, d), lambda i, j: (0, 0))
    in_specs = [pl.BlockSpec((1, tm, p.shape[-1]), lambda i, j: (i, j, 0)) for p in parts]
    in_specs += [pl.BlockSpec(w.shape, lambda i, j: (0, 0)) for w in weights]
    in_specs += [pl.BlockSpec((1, tm, d), lambda i, j: (i, j, 0)), vec, row, row, vec, vec,
                 pl.BlockSpec((ROUTER_ROWS, d), lambda i, j: (0, 0)),
                 pl.BlockSpec((ROUTER_ROWS, 1), lambda i, j: (0, 0))]
    nb = s // tm
    return pl.pallas_call(
        functools.partial(_outproj_kernel, n_in=n_in, alpha=alpha),
        grid=(b, nb),
        in_specs=in_specs,
        out_specs=[pl.BlockSpec((1, tm, d), lambda i, j: (i, j, 0)),
                   pl.BlockSpec((1, tm, d), lambda i, j: (i, j, 0)),
                   pl.BlockSpec((ROUTER_ROWS, tm), lambda i, j: (0, i * nb + j))],
        out_shape=[jax.ShapeDtypeStruct((b, s, d), F32), jax.ShapeDtypeStruct((b, s, d), F32),
                   jax.ShapeDtypeStruct((ROUTER_ROWS, b * s), F32)],
        compiler_params=_params("parallel", "parallel"),
        name="out_proj_norm",
    )(*parts, *[w.astype(MXU_DTYPE) for w in weights], x, gate1p, ln_g.reshape(1, d), ln_b.reshape(1, d),
      scale1p, shift, w_router_t, b_router)


def _route_kernel(lg_ref, ids_ref, gates_ref, cnt_ref, run_ref, *, tm):
    @pl.when(pl.program_id(0) == 0)
    def _():
        run_ref[...] = jnp.zeros_like(run_ref)

    lg = lg_ref[...]
    g0 = N_EXPERTS
    g_max = lg[g0:g0 + 1]
    grp = jnp.zeros((1, tm), I32)
    for i in range(1, N_GROUPS):
        gi = lg[g0 + i:g0 + i + 1]
        better = gi > g_max
        grp = jnp.where(better, i, grp)
        g_max = jnp.where(better, gi, g_max)
    den = jnp.exp(lg[g0:g0 + 1] - g_max)
    for i in range(1, N_GROUPS):
        den += jnp.exp(lg[g0 + i:g0 + i + 1] - g_max)
    p_grp = 1.0 / den

    cand = lg[0:EXPERTS_PER_GROUP]
    for g in range(1, N_GROUPS):
        cand = jnp.where(grp == g, lg[g * EXPERTS_PER_GROUP:(g + 1) * EXPERTS_PER_GROUP], cand)
    ridx = lax.broadcasted_iota(I32, (EXPERTS_PER_GROUP, tm), 0).astype(F32)
    none = float(EXPERTS_PER_GROUP)
    v1 = jnp.max(cand, axis=0, keepdims=True)
    i1 = jnp.min(jnp.where(cand == v1, ridx, none), axis=0, keepdims=True)
    rest = jnp.where(ridx == i1, -jnp.inf, cand)
    v2 = jnp.max(rest, axis=0, keepdims=True)
    i2 = jnp.min(jnp.where(rest == v2, ridx, none), axis=0, keepdims=True)
    e21 = jnp.exp(v2 - v1)
    gate1 = p_grp / (1.0 + e21)
    gate2 = p_grp * e21 / (1.0 + e21)
    ex1 = grp * EXPERTS_PER_GROUP + i1.astype(I32)
    ex2 = grp * EXPERTS_PER_GROUP + i2.astype(I32)

    eidx = lax.broadcasted_iota(I32, (N_EXPERTS, tm), 0)
    oh1 = jnp.where(eidx == ex1, 1.0, 0.0)
    oh2 = jnp.where(eidx == ex2, 1.0, 0.0)
    oh = (oh1 + oh2).astype(MXU_DTYPE)
    earlier = jnp.where(lax.broadcasted_iota(I32, (tm, tm), 0) < lax.broadcasted_iota(I32, (tm, tm), 1),
                        1.0, 0.0).astype(MXU_DTYPE)
    run = run_ref[...]
    before = _dot(oh, earlier) + jnp.concatenate([run] * (tm // LANES), axis=1)
    rank1 = jnp.sum(oh1 * before, axis=0, keepdims=True).astype(I32)
    rank2 = jnp.sum(oh2 * before, axis=0, keepdims=True).astype(I32)
    run = run + _dot(oh, jnp.ones((tm, LANES), MXU_DTYPE))
    run_ref[...] = run
    cnt_ref[...] = run
    ids_ref[...] = jnp.concatenate([ex1, ex2, rank1, rank2, jnp.zeros((4, tm), I32)], axis=0)
    gates_ref[...] = jnp.concatenate([gate1, gate2, jnp.zeros((6, tm), F32)], axis=0)


def _route(logits_t):
    t = logits_t.shape[1]
    tm = min(TOKEN_TILE, t)
    return pl.pallas_call(
        functools.partial(_route_kernel, tm=tm),
        grid=(t // tm,),
        in_specs=[pl.BlockSpec((ROUTER_ROWS, tm), lambda i: (0, i))],
        out_specs=[pl.BlockSpec((8, tm), lambda i: (0, i)), pl.BlockSpec((8, tm), lambda i: (0, i)),
                   pl.BlockSpec((N_EXPERTS, LANES), lambda i: (0, 0))],
        out_shape=[jax.ShapeDtypeStruct((8, t), I32), jax.ShapeDtypeStruct((8, t), F32),
                   jax.ShapeDtypeStruct((N_EXPERTS, LANES), F32)],
        scratch_shapes=[pltpu.VMEM((N_EXPERTS, LANES), F32)],
        compiler_params=_params("arbitrary"),
        name="route",
    )(logits_t)


def _row_copy(idx_ref, src_hbm, dst, sem, r):
    return pltpu.make_async_copy(src_hbm.at[pl.ds(idx_ref[0, 0, r], 1), :], dst.at[pl.ds(r, 1), :], sem)


def _gather_rows(idx_ref, src_hbm, dst, sem, n):
    def body(r, _):
        _row_copy(idx_ref, src_hbm, dst, sem, r).start()
        return 0

    lax.fori_loop(0, n, body, 0)


def _gather_rows_inline(idx_ref, src_hbm, dst, sem, n):
    for r in range(n):
        _row_copy(idx_ref, src_hbm, dst, sem, r).start()


def _wait_rows(src_hbm, dst, sem, n):
    pltpu.make_async_copy(src_hbm.at[pl.ds(0, n), :], dst, sem).wait()


def _expert_kernel(be_ref, idx_a_ref, idx_b_ref, idx_na_ref, idx_nb_ref, x_hbm, w1a_ref, w3a_ref, w2a_ref, w1b_ref,
                   w3b_ref, w2b_ref, y_ref, xa0, xb0, xa1, xb1, sem, *, tm):
    i = pl.program_id(0)
    last = pl.num_programs(0) - 1
    odd = (i % 2) == 1

    @pl.when(i == 0)
    def _():
        _gather_rows(idx_a_ref, x_hbm, xa0, sem.at[0], tm)
        _gather_rows(idx_b_ref, x_hbm, xb0, sem.at[1], tm)

    def block(cur, cur_sem, w1_ref, w3_ref, w2_ref, out_rows, nxt_idx_ref, nxt, nxt_sem):
        _gather_rows_inline(nxt_idx_ref, x_hbm, nxt, nxt_sem, tm)
        _wait_rows(x_hbm, cur, cur_sem, tm)
        xb = cur[...].astype(MXU_DTYPE)
        hidden = _silu(_dot(xb, w1_ref[0, 0].astype(MXU_DTYPE))) * _dot(xb, w3_ref[0, 0].astype(MXU_DTYPE))
        y_ref[out_rows, :] = _dot(hidden.astype(MXU_DTYPE), w2_ref[0, 0].astype(MXU_DTYPE))

    def step(a_cur, b_cur, a_nxt, b_nxt, s_cur, s_nxt):
        block(a_cur, sem.at[s_cur], w1a_ref, w3a_ref, w2a_ref, slice(0, tm), idx_na_ref, a_nxt, sem.at[s_nxt])
        block(b_cur, sem.at[s_cur + 1], w1b_ref, w3b_ref, w2b_ref, slice(tm, 2 * tm), idx_nb_ref, b_nxt,
              sem.at[s_nxt + 1])

        @pl.when(i == last)
        def _():
            _wait_rows(x_hbm, a_nxt, sem.at[s_nxt], tm)
            _wait_rows(x_hbm, b_nxt, sem.at[s_nxt + 1], tm)

    @pl.when(jnp.logical_not(odd))
    def _():
        step(xa0, xb0, xa1, xb1, 0, 2)

    @pl.when(odd)
    def _():
        step(xa1, xb1, xa0, xb0, 2, 0)


def _experts(u_flat, slot_tok, blk_expert, w1, w3, w2, layer, tm):
    t, d = u_flat.shape
    n_blk = slot_tok.shape[0] // tm
    assert n_blk % 2 == 0
    ff = w1.shape[-1]
    idx3 = slot_tok.reshape(n_blk, 1, tm)
    last = n_blk - 1

    def idx_spec(block_of):
        return pl.BlockSpec((1, 1, tm), lambda i, be: (block_of(i), 0, 0), memory_space=pltpu.SMEM)

    def weight_specs(which):
        return [pl.BlockSpec((1, 1, d, ff), lambda i, be: (layer, be[2 * i + which], 0, 0)),
                pl.BlockSpec((1, 1, d, ff), lambda i, be: (layer, be[2 * i + which], 0, 0)),
                pl.BlockSpec((1, 1, ff, d), lambda i, be: (layer, be[2 * i + which], 0, 0))]

    grid_spec = pltpu.PrefetchScalarGridSpec(
        num_scalar_prefetch=1,
        grid=(n_blk // 2,),
        in_specs=[idx_spec(lambda i: 0), idx_spec(lambda i: 1),
                  idx_spec(lambda i: jnp.minimum(2 * i + 2, last - 1)), idx_spec(lambda i: jnp.minimum(2 * i + 3, last)),
                  pl.BlockSpec(memory_space=pl.ANY)]
        + weight_specs(0) + weight_specs(1),
        out_specs=pl.BlockSpec((2 * tm, d), lambda i, be: (i, 0)),
        scratch_shapes=[pltpu.VMEM((tm, d), F32)] * 4 + [pltpu.SemaphoreType.DMA((4,))],
    )
    return pl.pallas_call(
        functools.partial(_expert_kernel, tm=tm),
        grid_spec=grid_spec,
        out_shape=jax.ShapeDtypeStruct((n_blk * tm, d), F32),
        compiler_params=_params("arbitrary"),
        name="experts",
    )(blk_expert, idx3, idx3, idx3, idx3, u_flat, w1, w3, w2, w1, w3, w2)


def _combine_kernel(d1_ref, d2_ref, d1n_ref, d2n_ref, y_hbm, gates_ref, x_ref, gate_ref, lng_ref, lnb_ref,
                    o_ref, ybuf, sem, *, tm, alpha):
    i = pl.program_id(0)
    n = pl.num_programs(0)
    slot = i % 2

    def wait(s):
        _wait_rows(y_hbm, ybuf.at[s, 0], sem.at[s, 0], tm)
        _wait_rows(y_hbm, ybuf.at[s, 1], sem.at[s, 1], tm)

    @pl.when(i == 0)
    def _():
        _gather_rows(d1_ref, y_hbm, ybuf.at[0, 0], sem.at[0, 0], tm)
        _gather_rows(d2_ref, y_hbm, ybuf.at[0, 1], sem.at[0, 1], tm)

    _gather_rows_inline(d1n_ref, y_hbm, ybuf.at[1 - slot, 0], sem.at[1 - slot, 0], tm)
    _gather_rows_inline(d2n_ref, y_hbm, ybuf.at[1 - slot, 1], sem.at[1 - slot, 1], tm)
    wait(slot)
    gates = gates_ref[...]
    f = gates[:, 0:1] * ybuf[slot, 0] + gates[:, 1:2] * ybuf[slot, 1]
    o_ref[...] = _layer_norm(alpha * x_ref[...] + gate_ref[0] * f, lng_ref[...], lnb_ref[...])

    @pl.when(i == n - 1)
    def _():
        wait(1 - slot)


def _combine(y_slots, dest1, dest2, gates, x_flat, gate1p, ln_g, ln_b, seq, alpha):
    t, d = x_flat.shape
    tm = min(COMBINE_TILE, seq)
    n_blk = t // tm
    per_seq = seq // tm
    d1 = dest1.reshape(n_blk, 1, tm)
    d2 = dest2.reshape(n_blk, 1, tm)
    cur = pl.BlockSpec((1, 1, tm), lambda i: (i, 0, 0), memory_space=pltpu.SMEM)
    nxt = pl.BlockSpec((1, 1, tm), lambda i: (jnp.minimum(i + 1, n_blk - 1), 0, 0), memory_space=pltpu.SMEM)
    row = pl.BlockSpec((1, d), lambda i: (0, 0))
    return pl.pallas_call(
        functools.partial(_combine_kernel, tm=tm, alpha=alpha),
        grid=(n_blk,),
        in_specs=[cur, cur, nxt, nxt, pl.BlockSpec(memory_space=pl.ANY),
                  pl.BlockSpec((tm, TOP_K), lambda i: (i, 0)),
                  pl.BlockSpec((tm, d), lambda i: (i, 0)),
                  pl.BlockSpec((1, 1, d), lambda i: (i // per_seq, 0, 0)), row, row],
        out_specs=pl.BlockSpec((tm, d), lambda i: (i, 0)),
        out_shape=jax.ShapeDtypeStruct((t, d), F32),
        scratch_shapes=[pltpu.VMEM((2, 2, tm, d), F32), pltpu.SemaphoreType.DMA((2, 2))],
        compiler_params=_params("arbitrary"),
        name="combine_norm",
    )(d1, d2, d1, d2, y_slots, gates, x_flat, gate1p, ln_g.reshape(1, d), ln_b.reshape(1, d))


EXPERT_TILE = MXU_TILE
SLOT_STRIDE = 7919

def _moe(u, logits_t, x, gate2p, ln_g, ln_b, w1, w3, w2, layer, alpha):
    b, s, d = x.shape
    t = b * s
    ids, gates8, cnt = _route(logits_t)
    counts = cnt[:, 0].astype(I32)
    padded = (counts + EXPERT_TILE - 1) // EXPERT_TILE * EXPERT_TILE
    pad_end = jnp.cumsum(padded)
    pad_start = pad_end - padded
    def slot(e, rank):
        return pad_start[e] + (rank * SLOT_STRIDE) % padded[e]

    dest1 = slot(ids[0], ids[2])
    dest2 = slot(ids[1], ids[3])
    n_slots = t * TOP_K + N_EXPERTS * EXPERT_TILE
    n_blk = n_slots // EXPERT_TILE
    tok = jnp.arange(t, dtype=I32)
    slot_tok = jnp.zeros((n_slots,), I32).at[jnp.concatenate([dest1, dest2])].set(
        jnp.concatenate([tok, tok]), unique_indices=True, mode='promise_in_bounds')
    blk_start = jnp.arange(n_blk, dtype=I32) * EXPERT_TILE
    blk_expert = jnp.minimum(jnp.sum((pad_end[None, :] <= blk_start[:, None]).astype(I32), axis=1), N_EXPERTS - 1)
    y_slots = _experts(u.reshape(t, d), slot_tok, blk_expert.astype(I32), w1, w3, w2, layer, EXPERT_TILE)
    out = _combine(y_slots, dest1, dest2, gates8[:TOP_K].T, x.reshape(t, d), gate2p, ln_g, ln_b, s, alpha)
    return out.reshape(b, s, d)


def kernel(x, c, ln1_g, ln1_b, ln2_g, ln2_b, w_ada, b_ada, even_w_in, even_w_out, ret_gn_g, odd_w_in, odd_w_out, lambda_q1, lambda_k1, lambda_q2, lambda_k2, diff_subln_g, moe_w_group, moe_b_group, moe_w_router, moe_b_router, moe_w1, moe_w3, moe_w2):
    b, s, d = x.shape
    depth = w_ada.shape[0]
    alpha = (2.0 * depth) ** 0.25
    mod = _ada(c, w_ada, b_ada)
    for l in range(depth):
        sh1, sc1, g1, sh2, sc2, g2 = [m[:, None, :] for m in jnp.split(mod[l], 6, axis=-1)]
        i = l // 2
        if l % 2 == 0:
            w_in = even_w_in[i]
            w_main = jnp.concatenate([w_in[:, :2 * SB_WIDTH], w_in[:, 3 * SB_WIDTH:]], axis=1)
            proj, v_t = _inproj(x, 1.0 + sc1, sh1, w_main, w_in[:, 2 * SB_WIDTH:3 * SB_WIDTH])
            parts = [_sb_attention(proj, v_t), _retention(proj, ret_gn_g[i], 2 * SB_WIDTH // LANES)]
            w_out = even_w_out[i]
            weights = [w_out[:SB_WIDTH], w_out[SB_WIDTH:]]
        else:
            w_in = odd_w_in[i]
            proj, v_t = _inproj(x, 1.0 + sc1, sh1, w_in[:, :2 * DIFF_QK], w_in[:, 2 * DIFF_QK:])
            lambda_init = 0.8 - 0.6 * math.exp(-0.3 * l)
            lam_rows = jnp.stack([lambda_q1[i], lambda_k1[i], lambda_q2[i], lambda_k2[i]])
            parts = [_diff_attention(proj, v_t, lam_rows, diff_subln_g[i], lambda_init)]
            weights = [odd_w_out[i]]
        w_router_t = jnp.zeros((ROUTER_ROWS, d), F32).at[:N_EXPERTS].set(moe_w_router[l].T.astype(F32))
        w_router_t = w_router_t.at[N_EXPERTS:N_EXPERTS + N_GROUPS].set(moe_w_group[l].T.astype(F32))
        b_router = jnp.zeros((ROUTER_ROWS, 1), F32).at[:N_EXPERTS, 0].set(moe_b_router[l].astype(F32))
        b_router = b_router.at[N_EXPERTS:N_EXPERTS + N_GROUPS, 0].set(moe_b_group[l].astype(F32))
        x, u, logits_t = _outproj(parts, weights, x, 1.0 + g1, ln1_g[l], ln1_b[l], 1.0 + sc2, sh2,
                                  w_router_t, b_router, alpha)
        x = _moe(u, logits_t, x, 1.0 + g2, ln2_g[l], ln2_b[l], moe_w1, moe_w3, moe_w2, l, alpha)
    return x
```

```python
import functools
import math

import jax
import jax.numpy as jnp
import numpy as np
from jax import lax
from jax.experimental import pallas as pl
from jax.experimental.pallas import tpu as pltpu

F32 = jnp.float32
I32 = jnp.int32
MXU_DTYPE = jnp.bfloat16
HIGHEST = lax.Precision.HIGHEST
LOG2E = math.log2(math.e)

LN_EPS = 1e-5
CHUNK = 64
LANES = 128
SB_HEADS, SB_HEAD_DIM = 8, 64
RET_HEADS, RET_HEAD_DIM = 4, 128
DIFF_HEADS, DIFF_HEAD_DIM = 8, 64
SB_WIDTH = SB_HEADS * SB_HEAD_DIM
RET_WIDTH = RET_HEADS * RET_HEAD_DIM
DIFF_QK = DIFF_HEADS * 2 * DIFF_HEAD_DIM
N_GROUPS, EXPERTS_PER_GROUP = 4, 8
N_EXPERTS = N_GROUPS * EXPERTS_PER_GROUP
TOP_K = 2
ROUTER_ROWS = 40

VMEM_LIMIT = 56 * 1024 * 1024

MXU_TILE = 256
ATTENTION_TILE = MXU_TILE
TOKEN_TILE = 2 * MXU_TILE
COMBINE_TILE = MXU_TILE
PROJ_COLS = 2 * MXU_TILE
ADA_COLS = 6 * MXU_TILE
EXP2_MAX = 126.0


def _params(*sem):
    return pltpu.CompilerParams(dimension_semantics=sem, vmem_limit_bytes=VMEM_LIMIT)


def _dot(a, b):
    return jnp.dot(a, b, preferred_element_type=F32)


def _dot_nt(a, b):
    return lax.dot_general(a, b, (((1,), (1,)), ((), ())), preferred_element_type=F32)


def _dot_tn(a, b):
    return lax.dot_general(a, b, (((0,), (0,)), ((), ())), preferred_element_type=F32)


def _silu(x):
    return x * (1.0 / (1.0 + jnp.exp(-x)))


def _ada_kernel(c_ref, w_ref, b_ref, o_ref):
    o_ref[0] = jnp.dot(_silu(c_ref[...]), w_ref[0], preferred_element_type=F32, precision=HIGHEST) + b_ref[0]


def _ada(c, w_ada, b_ada):
    depth, d, n = w_ada.shape
    bp = 8
    cp = jnp.zeros((bp, d), F32).at[: c.shape[0]].set(c)
    tn = ADA_COLS
    out = pl.pallas_call(
        _ada_kernel,
        grid=(depth, n // tn),
        in_specs=[
            pl.BlockSpec((bp, d), lambda l, j: (0, 0)),
            pl.BlockSpec((1, d, tn), lambda l, j: (l, 0, j)),
            pl.BlockSpec((1, 1, tn), lambda l, j: (l, 0, j)),
        ],
        out_specs=pl.BlockSpec((1, bp, tn), lambda l, j: (l, 0, j)),
        out_shape=jax.ShapeDtypeStruct((depth, bp, n), F32),
        compiler_params=_params("parallel", "parallel"),
        name="ada_mod",
    )(cp, w_ada, b_ada.reshape(depth, 1, n))
    return out[:, : c.shape[0]]


def _inproj_kernel(x_ref, sc_ref, sh_ref, w_ref, wvt_ref, o_ref, vt_ref, *, tn):
    u = (x_ref[0] * sc_ref[0] + sh_ref[0]).astype(MXU_DTYPE)
    for j in range(o_ref.shape[2] // tn):
        o_ref[0, :, j * tn:(j + 1) * tn] = _dot(u, w_ref[:, j * tn:(j + 1) * tn]).astype(o_ref.dtype)
    for j in range(vt_ref.shape[1] // tn):
        vt_ref[0, j * tn:(j + 1) * tn, :] = _dot_nt(wvt_ref[j * tn:(j + 1) * tn, :], u).astype(vt_ref.dtype)


def _inproj(x, scale1p, shift, w, w_v):
    b, s, d = x.shape
    n, n_v = w.shape[1], w_v.shape[1]
    tm = min(TOKEN_TILE, s)
    return pl.pallas_call(
        functools.partial(_inproj_kernel, tn=PROJ_COLS),
        grid=(b, s // tm),
        in_specs=[
            pl.BlockSpec((1, tm, d), lambda i, j: (i, j, 0)),
            pl.BlockSpec((1, 1, d), lambda i, j: (i, 0, 0)),
            pl.BlockSpec((1, 1, d), lambda i, j: (i, 0, 0)),
            pl.BlockSpec((d, n), lambda i, j: (0, 0)),
            pl.BlockSpec((n_v, d), lambda i, j: (0, 0)),
        ],
        out_specs=[pl.BlockSpec((1, tm, n), lambda i, j: (i, j, 0)),
                   pl.BlockSpec((1, n_v, tm), lambda i, j: (i, 0, j))],
        out_shape=[jax.ShapeDtypeStruct((b, s, n), MXU_DTYPE), jax.ShapeDtypeStruct((b, n_v, s), MXU_DTYPE)],
        compiler_params=_params("parallel", "parallel"),
        name="in_proj",
    )(x, scale1p, shift, w.astype(MXU_DTYPE), w_v.T.astype(MXU_DTYPE))


MASKED = -float("inf")
ITEM_PLAIN, ITEM_DIAGONAL, ITEM_NULL = 0, 1, 2
FLAG_FIRST, FLAG_LAST = 1, 2


def _triangle_items(n_q, pad, diagonal_first, n_streams):
    streams = []
    for s in range(n_streams):
        items = []
        for qb in range(s, n_q, n_streams):
            order = range(qb, -1, -1) if diagonal_first else range(qb + 1)
            for n, kb in enumerate(order):
                flags = (FLAG_FIRST if n == 0 else 0) | (FLAG_LAST if n == qb else 0)
                items.append((qb, kb, ITEM_DIAGONAL if kb == qb else ITEM_PLAIN, flags))
        streams.append(items)
    n_items = max(len(items) for items in streams)
    null = (0, 0, ITEM_NULL, 0)
    rows = [np.asarray([null] * pad + items + [null] * (n_items - len(items) + pad + 1), np.int32).T
            for items in streams]
    return np.concatenate(rows, axis=0).copy(), n_items


SB_STAGES = 3
SUM_ROWS = 16
SB_STREAMS = 2


def _sb_kernel(tab_ref, q_ref, k_ref, vt_ref, o_ref, qh_s, acc_ref, carry_ref, mask_ref, y_s, yms_s, sums_s,
               *, tq, n_items):
    tk = tq
    extra = SUM_ROWS
    heads = range(2)
    streams = range(SB_STREAMS)
    lane = lax.broadcasted_iota(I32, (1, LANES), 1)
    key = lax.broadcasted_iota(I32, (tk, tq), 0)
    qry = lax.broadcasted_iota(I32, (tk, tq), 1)
    r = lax.broadcasted_iota(I32, (tk + extra, tk), 0)
    c = lax.broadcasted_iota(I32, (tk + extra, tk), 1)
    neg_later = jnp.where(r >= tk, -1.0, jnp.where(c > r, -1.0, 0.0)).astype(MXU_DTYPE)
    scale2 = SB_HEAD_DIM ** -0.5 * LOG2E

    def prepare_queries(blk, _):
        rows = pl.ds(pl.multiple_of(blk * tq, tq), tq)
        q2 = q_ref[0, rows, :]
        for h in heads:
            qh_s[h, rows, :] = (jnp.where((lane // SB_HEAD_DIM) == h, q2, jnp.zeros_like(q2)).astype(F32)
                                * scale2).astype(MXU_DTYPE)
        return 0

    lax.fori_loop(0, q_ref.shape[1] // tq, prepare_queries, 0)
    acc_ref[...] = jnp.zeros_like(acc_ref)
    carry_ref[...] = jnp.zeros_like(carry_ref)
    mask_ref[ITEM_PLAIN] = jnp.zeros((tk, tq), F32)
    mask_ref[ITEM_DIAGONAL] = jnp.where(key < qry, 0.0, MASKED)
    mask_ref[ITEM_NULL] = jnp.full((tk, tq), MASKED, F32)
    y_s[...] = jnp.zeros_like(y_s)
    yms_s[...] = jnp.full(yms_s.shape, MASKED, F32)
    sums_s[...] = jnp.zeros_like(sums_s)

    def rows_of(block, size):
        return pl.ds(pl.multiple_of(block * size, size), size)

    def trip(it, parity):
        col_x, col_y, col_c = it + 2, it + 1, it
        for s in streams:
            kj = k_ref[0, rows_of(tab_ref[4 * s + 1, col_x], tk), :]
            qrows = rows_of(tab_ref[4 * s, col_x], tq)
            for h in heads:
                y_s[2 * s + h, parity] = _dot_nt(kj, qh_s[h, qrows, :])
        for s in streams:
            mask = mask_ref[tab_ref[4 * s + 2, col_y]]
            for h in heads:
                ym = y_s[2 * s + h, 1 - parity] + mask
                sp = jnp.maximum(ym, jnp.log2(1.0 + jnp.exp2(jnp.minimum(ym, EXP2_MAX))))
                yms_s[2 * s + h, parity] = ym - sp
                sums_s[2 * s + h, parity] = _dot(neg_later, sp.astype(MXU_DTYPE))
        for s in streams:
            vtj = vt_ref[0, :, rows_of(tab_ref[4 * s + 1, col_c], tk)]
            keep = jnp.where((tab_ref[4 * s + 3, col_c] & FLAG_FIRST) != 0, 0.0, 1.0)
            for h in heads:
                sums = sums_s[2 * s + h, 1 - parity]
                carry = carry_ref[2 * s + h] * keep
                w = jnp.exp2(yms_s[2 * s + h, 1 - parity] + sums[:tk] + carry).astype(MXU_DTYPE)
                acc_ref[2 * s + h] = acc_ref[2 * s + h] * keep + _dot(vtj, w)
                carry_ref[2 * s + h] = carry + sums[tk:tk + 1]

        for s in streams:
            @pl.when((tab_ref[4 * s + 3, col_c] & FLAG_LAST) != 0)
            def _(s=s):
                sub = lax.broadcasted_iota(I32, (LANES, 1), 0)
                o_ref[0, rows_of(tab_ref[4 * s, col_c], tq), :] = jnp.where(
                    sub < SB_HEAD_DIM, acc_ref[2 * s], acc_ref[2 * s + 1]).T.astype(o_ref.dtype)

    def trip_pair(i, _):
        trip(2 * i, 0)
        trip(2 * i + 1, 1)
        return 0

    lax.fori_loop(0, pl.cdiv(n_items + SB_STAGES - 1, 2), trip_pair, 0)


def _sb_attention(proj, v_t):
    b, s, _ = proj.shape
    tq = min(ATTENTION_TILE, s)
    n_pairs = SB_WIDTH // LANES
    table, n_items = _triangle_items(s // tq, SB_STAGES - 1, diagonal_first=True, n_streams=SB_STREAMS)
    n_chains = 2 * SB_STREAMS
    grid_spec = pltpu.PrefetchScalarGridSpec(
        num_scalar_prefetch=1,
        grid=(b, n_pairs),
        in_specs=[
            pl.BlockSpec((1, s, LANES), lambda i, p, t: (i, 0, p)),
            pl.BlockSpec((1, s, LANES), lambda i, p, t: (i, 0, n_pairs + p)),
            pl.BlockSpec((1, LANES, s), lambda i, p, t: (i, p, 0)),
        ],
        out_specs=pl.BlockSpec((1, s, LANES), lambda i, p, t: (i, 0, p)),
        scratch_shapes=[
            pltpu.VMEM((2, s, LANES), MXU_DTYPE),
            pltpu.VMEM((n_chains, LANES, tq), F32),
            pltpu.VMEM((n_chains, 1, tq), F32),
            pltpu.VMEM((3, tq, tq), F32),
            pltpu.VMEM((n_chains, 2, tq, tq), F32),
            pltpu.VMEM((n_chains, 2, tq, tq), F32),
            pltpu.VMEM((n_chains, 2, tq + SUM_ROWS, tq), F32),
        ],
    )
    return pl.pallas_call(
        functools.partial(_sb_kernel, tq=tq, n_items=n_items),
        grid_spec=grid_spec,
        out_shape=jax.ShapeDtypeStruct((b, s, SB_WIDTH), MXU_DTYPE),
        compiler_params=_params("parallel", "parallel"),
        name="sb_attention",
    )(jnp.asarray(table), proj, proj, v_t)


def _ret_kernel(lg_ref, q_ref, k_ref, v_ref, g_ref, gn_ref, o_ref, state_ref, *, tb):
    h = pl.program_id(1)
    blk = pl.program_id(2)

    @pl.when(blk == 0)
    def _():
        state_ref[...] = jnp.zeros_like(state_ref)

    lg = lg_ref[h]
    scale = RET_HEAD_DIM ** -0.5
    q = q_ref[0].astype(F32)
    k = k_ref[0].astype(F32)
    v = v_ref[0]
    row = lax.broadcasted_iota(I32, (tb, tb), 0)
    col = lax.broadcasted_iota(I32, (tb, tb), 1)
    dist = jnp.abs(row - col).astype(F32)
    decay = jnp.where(col // CHUNK <= row // CHUNK, jnp.exp(lg * dist) * scale, 0.0)
    pos = lax.broadcasted_iota(I32, (tb, 1), 0).astype(F32)
    scores = _dot_nt(q.astype(MXU_DTYPE), k.astype(MXU_DTYPE)) * decay
    intra = _dot(scores.astype(MXU_DTYPE), v)
    state = state_ref[...]
    q_in = (q * jnp.exp(lg * (pos + 1.0))).astype(MXU_DTYPE)
    inter = _dot(q_in, state.astype(MXU_DTYPE))
    k_out = (k * (jnp.exp(lg * (tb - 1.0 - pos)) * scale)).astype(MXU_DTYPE)
    block_decay = jnp.exp(lg * jnp.full((1, RET_HEAD_DIM), float(tb), F32))
    state_ref[...] = block_decay * state + _dot_tn(k_out, v)
    o = intra + inter
    mu = jnp.mean(o, axis=-1, keepdims=True)
    oc = o - mu
    var = jnp.mean(oc * oc, axis=-1, keepdims=True)
    o = oc * lax.rsqrt(var + LN_EPS) * gn_ref[0] * _silu(g_ref[0].astype(F32))
    o_ref[0] = o.astype(o_ref.dtype)


def _retention(proj, gn_gain, first_col_block):
    b, s, _ = proj.shape
    tb = min(TOKEN_TILE, s)
    log_gamma = jnp.log1p(-jnp.exp2(-5.0 - jnp.arange(RET_HEADS, dtype=F32)))

    def col(which):
        return lambda i, h, j, lg: (i, j, first_col_block + which * RET_HEADS + h)

    grid_spec = pltpu.PrefetchScalarGridSpec(
        num_scalar_prefetch=1,
        grid=(b, RET_HEADS, s // tb),
        in_specs=[pl.BlockSpec((1, tb, LANES), col(w)) for w in range(4)]
        + [pl.BlockSpec((1, 1, LANES), lambda i, h, j, lg: (h, 0, 0))],
        out_specs=pl.BlockSpec((1, tb, LANES), lambda i, h, j, lg: (i, j, h)),
        scratch_shapes=[pltpu.VMEM((RET_HEAD_DIM, RET_HEAD_DIM), F32)],
    )
    return pl.pallas_call(
        functools.partial(_ret_kernel, tb=tb),
        grid_spec=grid_spec,
        out_shape=jax.ShapeDtypeStruct((b, s, RET_WIDTH), MXU_DTYPE),
        compiler_params=_params("parallel", "parallel", "arbitrary"),
        name="retention",
    )(log_gamma, proj, proj, proj, proj, gn_gain.astype(F32).reshape(RET_HEADS, 1, RET_HEAD_DIM))


DIFF_STAGES = 3
DIFF_HEADS_PER_STEP = 2
DIFF_STREAMS = 1


def _diff_kernel(slope_ref, tab_ref, q_ref, k_ref, vt_ref, lam_ref, gain_ref, o_ref, qs_s, m_ref, l_ref, acc_ref,
                 bias_ref, z_s, p_s, a_s, lfin_s, *, tq, n_items, lambda_init):
    hp = pl.program_id(1)
    tk = tq
    nh = DIFF_HEADS_PER_STEP
    dv = LANES
    heads = range(nh)
    streams = range(DIFF_STREAMS)
    lane = lax.broadcasted_iota(I32, (1, LANES), 1)
    scale2 = DIFF_HEAD_DIM ** -0.5 * LOG2E
    slope2 = [slope_ref[hp * nh + hh] * LOG2E for hh in heads]

    def rows_of(block, size):
        return pl.ds(pl.multiple_of(block * size, size), size)

    def prepare_queries(blk, _):
        for hh in heads:
            q2 = q_ref[0, rows_of(blk, tq), hh * LANES:(hh + 1) * LANES]
            zero = jnp.zeros_like(q2)
            stacked = jnp.concatenate([jnp.where(lane < DIFF_HEAD_DIM, q2, zero),
                                       jnp.where(lane >= DIFF_HEAD_DIM, q2, zero)], axis=0)
            qs_s[hh, rows_of(blk, 2 * tq), :] = (stacked.astype(F32) * scale2).astype(MXU_DTYPE)
        return 0

    lax.fori_loop(0, q_ref.shape[1] // tq, prepare_queries, 0)
    key = lax.broadcasted_iota(I32, (tk, tq), 0)
    qry = lax.broadcasted_iota(I32, (tk, tq), 1)
    visible = key // CHUNK <= qry // CHUNK
    for hh in heads:
        plain = slope2[hh] * key.astype(F32)
        diag = jnp.where(visible, slope2[hh] * (qry - jnp.abs(qry - key)).astype(F32), MASKED)
        bias_ref[hh, ITEM_PLAIN] = jnp.concatenate([plain, plain], axis=1)
        bias_ref[hh, ITEM_DIAGONAL] = jnp.concatenate([diag, diag], axis=1)
    m_ref[...] = jnp.zeros_like(m_ref)
    l_ref[...] = jnp.zeros_like(l_ref)
    acc_ref[...] = jnp.zeros_like(acc_ref)
    z_s[...] = jnp.zeros_like(z_s)
    p_s[...] = jnp.zeros_like(p_s)
    a_s[...] = jnp.ones_like(a_s)
    lfin_s[...] = jnp.ones_like(lfin_s)

    def trip(it, parity):
        col_a, col_a1, col_c = it + 2, it + 1, it
        for s in streams:
            krows = rows_of(tab_ref[4 * s + 1, col_a], tk)
            qrows = rows_of(tab_ref[4 * s, col_a], 2 * tq)
            for hh in heads:
                z_s[nh * s + hh, parity] = _dot_nt(k_ref[0, krows, hh * LANES:(hh + 1) * LANES], qs_s[hh, qrows, :])
        lfin = {}
        for s in streams:
            vrows = rows_of(tab_ref[4 * s + 1, col_c], tk)
            for hh in heads:
                c = nh * s + hh
                acc_ref[c] = a_s[c] * acc_ref[c] + _dot(vt_ref[0, hh * dv:(hh + 1) * dv, vrows], p_s[c])
                lfin[c] = lfin_s[c]
        for s in streams:
            kind = tab_ref[4 * s + 2, col_a1]
            first = (tab_ref[4 * s + 3, col_a1] & FLAG_FIRST) != 0
            offset = ((tab_ref[4 * s + 1, col_a1] - tab_ref[4 * s, col_a1]) * tq).astype(F32)
            for hh in heads:
                c = nh * s + hh
                shift = jnp.where(kind == ITEM_PLAIN, slope2[hh] * offset,
                                  jnp.where(kind == ITEM_DIAGONAL, 0.0, MASKED))
                z = z_s[c, 1 - parity] + bias_ref[hh, jnp.minimum(kind, ITEM_DIAGONAL)]
                m_old = jnp.where(first, MASKED, m_ref[c])
                m_new = jnp.maximum(m_old, jnp.max(z, axis=0, keepdims=True) + shift)
                p = jnp.exp2(z - (m_new - shift))
                a = jnp.exp2(m_old - m_new)
                l_new = a * l_ref[c] + jnp.sum(p, axis=0, keepdims=True)
                p_s[c] = p.astype(MXU_DTYPE)
                a_s[c] = a
                lfin_s[c] = l_new
                l_ref[c] = l_new
                m_ref[c] = m_new

        for s in streams:
            @pl.when((tab_ref[4 * s + 3, col_c] & FLAG_LAST) != 0)
            def _(s=s):
                lam_v = lam_ref[...]
                lam = (jnp.exp(jnp.sum(lam_v[0:1] * lam_v[1:2], axis=-1, keepdims=True))
                       - jnp.exp(jnp.sum(lam_v[2:3] * lam_v[3:4], axis=-1, keepdims=True)) + lambda_init)
                orows = rows_of(tab_ref[4 * s, col_c], tq)
                for hh in heads:
                    c = nh * s + hh
                    o = acc_ref[c] * (1.0 / lfin[c])
                    o = o[:, :tq] - lam * o[:, tq:]
                    o = o * lax.rsqrt(jnp.mean(o * o, axis=0, keepdims=True) + LN_EPS)
                    o_ref[0, orows, hh * dv:(hh + 1) * dv] = (o * gain_ref[...] * (1.0 - lambda_init)).T.astype(
                        o_ref.dtype)

    def trip_pair(i, _):
        trip(2 * i, 0)
        trip(2 * i + 1, 1)
        return 0

    lax.fori_loop(0, pl.cdiv(n_items + DIFF_STAGES - 1, 2), trip_pair, 0)


def _diff_attention(proj, v_t, lam_rows, subln_gain, lambda_init):
    b, s, _ = proj.shape
    tq = min(ATTENTION_TILE, s)
    dv = subln_gain.shape[-1]
    nh = DIFF_HEADS_PER_STEP
    slopes = jnp.exp2(-8.0 / DIFF_HEADS * (jnp.arange(DIFF_HEADS, dtype=F32) + 1.0))
    kb = DIFF_QK // (nh * LANES)
    table, n_items = _triangle_items(s // tq, DIFF_STAGES - 1, diagonal_first=False, n_streams=DIFF_STREAMS)
    n_chains = nh * DIFF_STREAMS
    grid_spec = pltpu.PrefetchScalarGridSpec(
        num_scalar_prefetch=2,
        grid=(b, DIFF_HEADS // nh),
        in_specs=[
            pl.BlockSpec((1, s, nh * LANES), lambda i, h, sl, t: (i, 0, h)),
            pl.BlockSpec((1, s, nh * LANES), lambda i, h, sl, t: (i, 0, kb + h)),
            pl.BlockSpec((1, nh * dv, s), lambda i, h, sl, t: (i, h, 0)),
            pl.BlockSpec((4, DIFF_HEAD_DIM), lambda i, h, sl, t: (0, 0)),
            pl.BlockSpec((dv, 1), lambda i, h, sl, t: (0, 0)),
        ],
        out_specs=pl.BlockSpec((1, s, nh * dv), lambda i, h, sl, t: (i, 0, h)),
        scratch_shapes=[
            pltpu.VMEM((nh, 2 * s, LANES), MXU_DTYPE),
            pltpu.VMEM((n_chains, 1, 2 * tq), F32),
            pltpu.VMEM((n_chains, 1, 2 * tq), F32),
            pltpu.VMEM((n_chains, dv, 2 * tq), F32),
            pltpu.VMEM((nh, 2, tq, 2 * tq), F32),
            pltpu.VMEM((n_chains, 2, tq, 2 * tq), F32),
            pltpu.VMEM((n_chains, tq, 2 * tq), MXU_DTYPE),
            pltpu.VMEM((n_chains, 1, 2 * tq), F32),
            pltpu.VMEM((n_chains, 1, 2 * tq), F32),
        ],
    )
    return pl.pallas_call(
        functools.partial(_diff_kernel, tq=tq, n_items=n_items, lambda_init=lambda_init),
        grid_spec=grid_spec,
        out_shape=jax.ShapeDtypeStruct((b, s, DIFF_HEADS * dv), MXU_DTYPE),
        compiler_params=_params("parallel", "parallel"),
        name="diff_attention",
    )(slopes, jnp.asarray(table), proj, proj, v_t, lam_rows.astype(F32), subln_gain.astype(F32).reshape(dv, 1))


def _layer_norm(y, g, b):
    mu = jnp.mean(y, axis=-1, keepdims=True)
    yc = y - mu
    var = jnp.mean(yc * yc, axis=-1, keepdims=True)
    return yc * lax.rsqrt(var + LN_EPS) * g + b


def _outproj_kernel(*refs, n_in, alpha):
    a_refs, w_refs = refs[:n_in], refs[n_in:2 * n_in]
    x_ref, gate_ref, lng_ref, lnb_ref, sc_ref, sh_ref, wr_ref, br_ref, xo_ref, u_ref, lg_ref = refs[2 * n_in:]
    mix = _dot(a_refs[0][0], w_refs[0][...])
    for a_ref, w_ref in zip(a_refs[1:], w_refs[1:]):
        mix += _dot(a_ref[0], w_ref[...])
    xn = _layer_norm(alpha * x_ref[0] + gate_ref[0] * mix, lng_ref[...], lnb_ref[...])
    xo_ref[0] = xn
    u = xn * sc_ref[0] + sh_ref[0]
    u_ref[0] = u
    lg_ref[...] = lax.dot_general(wr_ref[...], u, (((1,), (1,)), ((), ())), preferred_element_type=F32,
                                  precision=HIGHEST) + br_ref[...]


def _outproj(parts, weights, x, gate1p, ln_g, ln_b, scale1p, shift, w_router_t, b_router, alpha):
    b, s, d = x.shape
    tm = min(TOKEN_TILE, s)
    n_in = len(parts)
    vec = pl.BlockSpec((1, 1, d), lambda i, j: (i, 0, 0))
    row = pl.BlockSpec((1, d), lambda i, j: (0, 0))
    in_specs = [pl.BlockSpec((1, tm, p.shape[-1]), lambda i, j: (i, j, 0)) for p in parts]
    in_specs += [pl.BlockSpec(w.shape, lambda i, j: (0, 0)) for w in weights]
    in_specs += [pl.BlockSpec((1, tm, d), lambda i, j: (i, j, 0)), vec, row, row, vec, vec,
                 pl.BlockSpec((ROUTER_ROWS, d), lambda i, j: (0, 0)),
                 pl.BlockSpec((ROUTER_ROWS, 1), lambda i, j: (0, 0))]
    nb = s // tm
    return pl.pallas_call(
        functools.partial(_outproj_kernel, n_in=n_in, alpha=alpha),
        grid=(b, nb),
        in_specs=in_specs,
        out_specs=[pl.BlockSpec((1, tm, d), lambda i, j: (i, j, 0)),
                   pl.BlockSpec((1, tm, d), lambda i, j: (i, j, 0)),
                   pl.BlockSpec((ROUTER_ROWS, tm), lambda i, j: (0, i * nb + j))],
        out_shape=[jax.ShapeDtypeStruct((b, s, d), F32), jax.ShapeDtypeStruct((b, s, d), F32),
                   jax.ShapeDtypeStruct((ROUTER_ROWS, b * s), F32)],
        compiler_params=_params("parallel", "parallel"),
        name="out_proj_norm",
    )(*parts, *[w.astype(MXU_DTYPE) for w in weights], x, gate1p, ln_g.reshape(1, d), ln_b.reshape(1, d),
      scale1p, shift, w_router_t, b_router)


def _route_kernel(lg_ref, ids_ref, gates_ref, cnt_ref, run_ref, *, tm):
    @pl.when(pl.program_id(0) == 0)
    def _():
        run_ref[...] = jnp.zeros_like(run_ref)

    lg = lg_ref[...]
    g0 = N_EXPERTS
    g_max = lg[g0:g0 + 1]
    grp = jnp.zeros((1, tm), I32)
    for i in range(1, N_GROUPS):
        gi = lg[g0 + i:g0 + i + 1]
        better = gi > g_max
        grp = jnp.where(better, i, grp)
        g_max = jnp.where(better, gi, g_max)
    den = jnp.exp(lg[g0:g0 + 1] - g_max)
    for i in range(1, N_GROUPS):
        den += jnp.exp(lg[g0 + i:g0 + i + 1] - g_max)
    p_grp = 1.0 / den

    cand = lg[0:EXPERTS_PER_GROUP]
    for g in range(1, N_GROUPS):
        cand = jnp.where(grp == g, lg[g * EXPERTS_PER_GROUP:(g + 1) * EXPERTS_PER_GROUP], cand)
    ridx = lax.broadcasted_iota(I32, (EXPERTS_PER_GROUP, tm), 0).astype(F32)
    none = float(EXPERTS_PER_GROUP)
    v1 = jnp.max(cand, axis=0, keepdims=True)
    i1 = jnp.min(jnp.where(cand == v1, ridx, none), axis=0, keepdims=True)
    rest = jnp.where(ridx == i1, -jnp.inf, cand)
    v2 = jnp.max(rest, axis=0, keepdims=True)
    i2 = jnp.min(jnp.where(rest == v2, ridx, none), axis=0, keepdims=True)
    e21 = jnp.exp(v2 - v1)
    gate1 = p_grp / (1.0 + e21)
    gate2 = p_grp * e21 / (1.0 + e21)
    ex1 = grp * EXPERTS_PER_GROUP + i1.astype(I32)
    ex2 = grp * EXPERTS_PER_GROUP + i2.astype(I32)

    eidx = lax.broadcasted_iota(I32, (N_EXPERTS, tm), 0)
    oh1 = jnp.where(eidx == ex1, 1.0, 0.0)
    oh2 = jnp.where(eidx == ex2, 1.0, 0.0)
    oh = (oh1 + oh2).astype(MXU_DTYPE)
    earlier = jnp.where(lax.broadcasted_iota(I32, (tm, tm), 0) < lax.broadcasted_iota(I32, (tm, tm), 1),
                        1.0, 0.0).astype(MXU_DTYPE)
    run = run_ref[...]
    before = _dot(oh, earlier) + jnp.concatenate([run] * (tm // LANES), axis=1)
    rank1 = jnp.sum(oh1 * before, axis=0, keepdims=True).astype(I32)
    rank2 = jnp.sum(oh2 * before, axis=0, keepdims=True).astype(I32)
    run = run + _dot(oh, jnp.ones((tm, LANES), MXU_DTYPE))
    run_ref[...] = run
    cnt_ref[...] = run
    ids_ref[...] = jnp.concatenate([ex1, ex2, rank1, rank2, jnp.zeros((4, tm), I32)], axis=0)
    gates_ref[...] = jnp.concatenate([gate1, gate2, jnp.zeros((6, tm), F32)], axis=0)


def _route(logits_t):
    t = logits_t.shape[1]
    tm = min(TOKEN_TILE, t)
    return pl.pallas_call(
        functools.partial(_route_kernel, tm=tm),
        grid=(t // tm,),
        in_specs=[pl.BlockSpec((ROUTER_ROWS, tm), lambda i: (0, i))],
        out_specs=[pl.BlockSpec((8, tm), lambda i: (0, i)), pl.BlockSpec((8, tm), lambda i: (0, i)),
                   pl.BlockSpec((N_EXPERTS, LANES), lambda i: (0, 0))],
        out_shape=[jax.ShapeDtypeStruct((8, t), I32), jax.ShapeDtypeStruct((8, t), F32),
                   jax.ShapeDtypeStruct((N_EXPERTS, LANES), F32)],
        scratch_shapes=[pltpu.VMEM((N_EXPERTS, LANES), F32)],
        compiler_params=_params("arbitrary"),
        name="route",
    )(logits_t)


def _row_copy(idx_ref, src_hbm, dst, sem, r):
    return pltpu.make_async_copy(src_hbm.at[pl.ds(idx_ref[0, 0, r], 1), :], dst.at[pl.ds(r, 1), :], sem)


def _gather_rows(idx_ref, src_hbm, dst, sem, n):
    def body(r, _):
        _row_copy(idx_ref, src_hbm, dst, sem, r).start()
        return 0

    lax.fori_loop(0, n, body, 0)


def _gather_rows_inline(idx_ref, src_hbm, dst, sem, n):
    for r in range(n):
        _row_copy(idx_ref, src_hbm, dst, sem, r).start(priority=r % 2)


def _wait_rows(src_hbm, dst, sem, n):
    pltpu.make_async_copy(src_hbm.at[pl.ds(0, n), :], dst, sem).wait()


def _expert_kernel(be_ref, idx_a_ref, idx_b_ref, idx_na_ref, idx_nb_ref, x_hbm, w1a_ref, w3a_ref, w2a_ref, w1b_ref,
                   w3b_ref, w2b_ref, y_ref, xa0, xb0, xa1, xb1, sem, *, tm):
    i = pl.program_id(0)
    last = pl.num_programs(0) - 1
    odd = (i % 2) == 1

    @pl.when(i == 0)
    def _():
        _gather_rows(idx_a_ref, x_hbm, xa0, sem.at[0], tm)
        _gather_rows(idx_b_ref, x_hbm, xb0, sem.at[1], tm)

    def block(cur, cur_sem, w1_ref, w3_ref, w2_ref, out_rows, nxt_idx_ref, nxt, nxt_sem):
        _gather_rows_inline(nxt_idx_ref, x_hbm, nxt, nxt_sem, tm)
        _wait_rows(x_hbm, cur, cur_sem, tm)
        xb = cur[...].astype(MXU_DTYPE)
        hidden = _silu(_dot(xb, w1_ref[0, 0].astype(MXU_DTYPE))) * _dot(xb, w3_ref[0, 0].astype(MXU_DTYPE))
        y_ref[out_rows, :] = _dot(hidden.astype(MXU_DTYPE), w2_ref[0, 0].astype(MXU_DTYPE))

    def step(a_cur, b_cur, a_nxt, b_nxt, s_cur, s_nxt):
        block(a_cur, sem.at[s_cur], w1a_ref, w3a_ref, w2a_ref, slice(0, tm), idx_na_ref, a_nxt, sem.at[s_nxt])
        block(b_cur, sem.at[s_cur + 1], w1b_ref, w3b_ref, w2b_ref, slice(tm, 2 * tm), idx_nb_ref, b_nxt,
              sem.at[s_nxt + 1])

        @pl.when(i == last)
        def _():
            _wait_rows(x_hbm, a_nxt, sem.at[s_nxt], tm)
            _wait_rows(x_hbm, b_nxt, sem.at[s_nxt + 1], tm)

    @pl.when(jnp.logical_not(odd))
    def _():
        step(xa0, xb0, xa1, xb1, 0, 2)

    @pl.when(odd)
    def _():
        step(xa1, xb1, xa0, xb0, 2, 0)


def _experts(u_flat, slot_tok, blk_expert, w1, w3, w2, layer, tm):
    t, d = u_flat.shape
    n_blk = slot_tok.shape[0] // tm
    assert n_blk % 2 == 0
    ff = w1.shape[-1]
    idx3 = slot_tok.reshape(n_blk, 1, tm)
    last = n_blk - 1

    def idx_spec(block_of):
        return pl.BlockSpec((1, 1, tm), lambda i, be: (block_of(i), 0, 0), memory_space=pltpu.SMEM)

    def weight_specs(which):
        return [pl.BlockSpec((1, 1, d, ff), lambda i, be: (layer, be[2 * i + which], 0, 0)),
                pl.BlockSpec((1, 1, d, ff), lambda i, be: (layer, be[2 * i + which], 0, 0)),
                pl.BlockSpec((1, 1, ff, d), lambda i, be: (layer, be[2 * i + which], 0, 0))]

    grid_spec = pltpu.PrefetchScalarGridSpec(
        num_scalar_prefetch=1,
        grid=(n_blk // 2,),
        in_specs=[idx_spec(lambda i: 0), idx_spec(lambda i: 1),
                  idx_spec(lambda i: jnp.minimum(2 * i + 2, last - 1)), idx_spec(lambda i: jnp.minimum(2 * i + 3, last)),
                  pl.BlockSpec(memory_space=pl.ANY)]
        + weight_specs(0) + weight_specs(1),
        out_specs=pl.BlockSpec((2 * tm, d), lambda i, be: (i, 0)),
        scratch_shapes=[pltpu.VMEM((tm, d), F32)] * 4 + [pltpu.SemaphoreType.DMA((4,))],
    )
    return pl.pallas_call(
        functools.partial(_expert_kernel, tm=tm),
        grid_spec=grid_spec,
        out_shape=jax.ShapeDtypeStruct((n_blk * tm, d), F32),
        compiler_params=_params("arbitrary"),
        name="experts",
    )(blk_expert, idx3, idx3, idx3, idx3, u_flat, w1, w3, w2, w1, w3, w2)


def _combine_kernel(d1_ref, d2_ref, d1n_ref, d2n_ref, y_hbm, gates_ref, x_ref, gate_ref, lng_ref, lnb_ref,
                    o_ref, ybuf, sem, *, tm, alpha):
    i = pl.program_id(0)
    n = pl.num_programs(0)
    slot = i % 2

    def wait(s):
        _wait_rows(y_hbm, ybuf.at[s, 0], sem.at[s, 0], tm)
        _wait_rows(y_hbm, ybuf.at[s, 1], sem.at[s, 1], tm)

    @pl.when(i == 0)
    def _():
        _gather_rows(d1_ref, y_hbm, ybuf.at[0, 0], sem.at[0, 0], tm)
        _gather_rows(d2_ref, y_hbm, ybuf.at[0, 1], sem.at[0, 1], tm)

    _gather_rows_inline(d1n_ref, y_hbm, ybuf.at[1 - slot, 0], sem.at[1 - slot, 0], tm)
    _gather_rows_inline(d2n_ref, y_hbm, ybuf.at[1 - slot, 1], sem.at[1 - slot, 1], tm)
    wait(slot)
    gates = gates_ref[...]
    f = gates[:, 0:1] * ybuf[slot, 0] + gates[:, 1:2] * ybuf[slot, 1]
    o_ref[...] = _layer_norm(alpha * x_ref[...] + gate_ref[0] * f, lng_ref[...], lnb_ref[...])

    @pl.when(i == n - 1)
    def _():
        wait(1 - slot)


def _combine(y_slots, dest1, dest2, gates, x_flat, gate1p, ln_g, ln_b, seq, alpha):
    t, d = x_flat.shape
    tm = min(COMBINE_TILE, seq)
    n_blk = t // tm
    per_seq = seq // tm
    d1 = dest1.reshape(n_blk, 1, tm)
    d2 = dest2.reshape(n_blk, 1, tm)
    cur = pl.BlockSpec((1, 1, tm), lambda i: (i, 0, 0), memory_space=pltpu.SMEM)
    nxt = pl.BlockSpec((1, 1, tm), lambda i: (jnp.minimum(i + 1, n_blk - 1), 0, 0), memory_space=pltpu.SMEM)
    row = pl.BlockSpec((1, d), lambda i: (0, 0))
    return pl.pallas_call(
        functools.partial(_combine_kernel, tm=tm, alpha=alpha),
        grid=(n_blk,),
        in_specs=[cur, cur, nxt, nxt, pl.BlockSpec(memory_space=pl.ANY),
                  pl.BlockSpec((tm, TOP_K), lambda i: (i, 0)),
                  pl.BlockSpec((tm, d), lambda i: (i, 0)),
                  pl.BlockSpec((1, 1, d), lambda i: (i // per_seq, 0, 0)), row, row],
        out_specs=pl.BlockSpec((tm, d), lambda i: (i, 0)),
        out_shape=jax.ShapeDtypeStruct((t, d), F32),
        scratch_shapes=[pltpu.VMEM((2, 2, tm, d), F32), pltpu.SemaphoreType.DMA((2, 2))],
        compiler_params=_params("arbitrary"),
        name="combine_norm",
    )(d1, d2, d1, d2, y_slots, gates, x_flat, gate1p, ln_g.reshape(1, d), ln_b.reshape(1, d))


EXPERT_TILE = MXU_TILE
SLOT_STRIDE = 7919

def _moe(u, logits_t, x, gate2p, ln_g, ln_b, w1, w3, w2, layer, alpha):
    b, s, d = x.shape
    t = b * s
    ids, gates8, cnt = _route(logits_t)
    counts = cnt[:, 0].astype(I32)
    padded = (counts + EXPERT_TILE - 1) // EXPERT_TILE * EXPERT_TILE
    pad_end = jnp.cumsum(padded)
    pad_start = pad_end - padded
    def slot(e, rank):
        return pad_start[e] + (rank * SLOT_STRIDE) % padded[e]

    dest1 = slot(ids[0], ids[2])
    dest2 = slot(ids[1], ids[3])
    n_slots = t * TOP_K + N_EXPERTS * EXPERT_TILE
    n_blk = n_slots // EXPERT_TILE
    tok = jnp.arange(t, dtype=I32)
    slot_tok = jnp.zeros((n_slots,), I32).at[jnp.concatenate([dest1, dest2])].set(
        jnp.concatenate([tok, tok]), unique_indices=True, mode='promise_in_bounds')
    blk_start = jnp.arange(n_blk, dtype=I32) * EXPERT_TILE
    blk_expert = jnp.minimum(jnp.sum((pad_end[None, :] <= blk_start[:, None]).astype(I32), axis=1), N_EXPERTS - 1)
    y_slots = _experts(u.reshape(t, d), slot_tok, blk_expert.astype(I32), w1, w3, w2, layer, EXPERT_TILE)
    out = _combine(y_slots, dest1, dest2, gates8[:TOP_K].T, x.reshape(t, d), gate2p, ln_g, ln_b, s, alpha)
    return out.reshape(b, s, d)


def kernel(x, c, ln1_g, ln1_b, ln2_g, ln2_b, w_ada, b_ada, even_w_in, even_w_out, ret_gn_g, odd_w_in, odd_w_out, lambda_q1, lambda_k1, lambda_q2, lambda_k2, diff_subln_g, moe_w_group, moe_b_group, moe_w_router, moe_b_router, moe_w1, moe_w3, moe_w2):
    b, s, d = x.shape
    depth = w_ada.shape[0]
    alpha = (2.0 * depth) ** 0.25
    mod = _ada(c, w_ada, b_ada)
    for l in range(depth):
        sh1, sc1, g1, sh2, sc2, g2 = [m[:, None, :] for m in jnp.split(mod[l], 6, axis=-1)]
        i = l // 2
        if l % 2 == 0:
            w_in = even_w_in[i]
            w_main = jnp.concatenate([w_in[:, :2 * SB_WIDTH], w_in[:, 3 * SB_WIDTH:]], axis=1)
            proj, v_t = _inproj(x, 1.0 + sc1, sh1, w_main, w_in[:, 2 * SB_WIDTH:3 * SB_WIDTH])
            parts = [_sb_attention(proj, v_t), _retention(proj, ret_gn_g[i], 2 * SB_WIDTH // LANES)]
            w_out = even_w_out[i]
            weights = [w_out[:SB_WIDTH], w_out[SB_WIDTH:]]
        else:
            w_in = odd_w_in[i]
            proj, v_t = _inproj(x, 1.0 + sc1, sh1, w_in[:, :2 * DIFF_QK], w_in[:, 2 * DIFF_QK:])
            lambda_init = 0.8 - 0.6 * math.exp(-0.3 * l)
            lam_rows = jnp.stack([lambda_q1[i], lambda_k1[i], lambda_q2[i], lambda_k2[i]])
            parts = [_diff_attention(proj, v_t, lam_rows, diff_subln_g[i], lambda_init)]
            weights = [odd_w_out[i]]
        w_router_t = jnp.zeros((ROUTER_ROWS, d), F32).at[:N_EXPERTS].set(moe_w_router[l].T.astype(F32))
        w_router_t = w_router_t.at[N_EXPERTS:N_EXPERTS + N_GROUPS].set(moe_w_group[l].T.astype(F32))
        b_router = jnp.zeros((ROUTER_ROWS, 1), F32).at[:N_EXPERTS, 0].set(moe_b_router[l].astype(F32))
        b_router = b_router.at[N_EXPERTS:N_EXPERTS + N_GROUPS, 0].set(moe_b_group[l].astype(F32))
        x, u, logits_t = _outproj(parts, weights, x, 1.0 + g1, ln1_g[l], ln1_b[l], 1.0 + sc2, sh2,
                                  w_router_t, b_router, alpha)
        x = _moe(u, logits_t, x, 1.0 + g2, ln2_g[l], ln2_b[l], moe_w1, moe_w3, moe_w2, l, alpha)
    return x
```
